```python
import math
import jax
import jax.numpy as jnp
from jax import lax
import numpy as np

D_MODEL = 1024
BATCH = 8
SEQ = 4096
DEPTH = 2

HEAD_DIM = 64
ROPE_THETA = 10000.0
A_HEADS = D_MODEL // (2 * HEAD_DIM)
A_BRANCHES = ((128, 1), (512, 4), (2048, 16))
B_HEADS = D_MODEL // (4 * HEAD_DIM)
B_VDIM = 2 * HEAD_DIM
A_WIDTH = A_HEADS * HEAD_DIM
B_WIDTH = B_HEADS * 2 * HEAD_DIM
EVEN_IN = 3 * A_WIDTH + 3 * B_WIDTH
Q_BLOCK = 128
C_HEADS = D_MODEL // 128
C_DK = 128
C_DV = 128
C_WIDTH = C_HEADS * C_DV
CONV_W = 4
CHUNK = 64
ODD_IN = 4 * C_WIDTH + 2 * C_HEADS
N_GROUPS = 4
EXPERTS_PER_GROUP = 8
N_EXPERTS = N_GROUPS * EXPERTS_PER_GROUP
TOP_K = 2
EXPERT_FF = D_MODEL // 2
MOE_BLOCK = 128
EPS = 1e-6
NEG_INF = -1e30

kernel_name = 'hybrid_dilated_diffattn_gdn_hmoe'


def rms_norm(x, gain):
    xf = x.astype(jnp.float32)
    y = xf * lax.rsqrt(jnp.mean(xf * xf, axis=-1, keepdims=True) + EPS)
    return (y * gain.astype(jnp.float32)).astype(x.dtype)


def rope_tables(seq_len, dim):
    inv_freq = 1.0 / (ROPE_THETA ** (jnp.arange(0, dim, 2, dtype=jnp.float32) / dim))
    ang = jnp.arange(seq_len, dtype=jnp.float32)[:, None] * inv_freq[None, :]
    return jnp.cos(ang), jnp.sin(ang)


def apply_rope(x, cos, sin):
    xf = x.astype(jnp.float32)
    x1, x2 = jnp.split(xf, 2, axis=-1)
    c = cos[None, :, None, :]
    s = sin[None, :, None, :]
    return jnp.concatenate([x1 * c - x2 * s, x2 * c + x1 * s], axis=-1).astype(x.dtype)


def qk_norm_rope(t, gain, cos, sin):
    return apply_rope(rms_norm(t, gain), cos, sin)


def dilated_branch(q, k, v, window, dilation):
    B, S, H, hd = q.shape
    n = window // dilation
    L = S // dilation
    nb = -(-L // n)
    Lp = nb * n

    def to_residue(t):
        t = t.reshape(B, L, dilation, H, hd).transpose(0, 2, 1, 3, 4)
        return jnp.pad(t, ((0, 0), (0, 0), (0, Lp - L), (0, 0), (0, 0)))

    def band(t):
        tp = jnp.pad(t, ((0, 0), (0, 0), (n, 0), (0, 0), (0, 0))).reshape(B, dilation, nb + 1, n, H, hd)
        return jnp.concatenate([tp[:, :, :-1], tp[:, :, 1:]], axis=3)

    qb = to_residue(q).reshape(B, dilation, nb, n, H, hd)
    kb = band(to_residue(k))
    vb = band(to_residue(v))
    s = jnp.einsum('brcihd,brcjhd->brchij', qb, kb).astype(jnp.float32) * (hd ** -0.5)
    i = jnp.arange(n)[:, None]
    j = jnp.arange(2 * n)[None, :]
    blk = jnp.arange(nb)[:, None, None]
    dist = n + i - j
    mask = (dist >= 0) & (dist <= n) & (blk * n + j >= n)
    s = jnp.where(mask[None, None, :, None], s, NEG_INF)
    m = jnp.max(s, axis=-1, keepdims=True)
    p = jnp.exp(s - m)
    den = jnp.sum(p, axis=-1)
    o = jnp.einsum('brchij,brcjhd->brcihd', p, vb.astype(jnp.float32))
    o = o / jnp.moveaxis(den, -1, 3)[..., None]
    lse = jnp.moveaxis(m[..., 0] + jnp.log(den), -1, 3)
    o = o.reshape(B, dilation, Lp, H, hd)[:, :, :L].transpose(0, 2, 1, 3, 4).reshape(B, S, H, hd)
    lse = lse.reshape(B, dilation, Lp, H)[:, :, :L].transpose(0, 2, 1, 3).reshape(B, S, H)
    return o, lse


def dilated_attention(q, k, v):
    outs, lses = [], []
    for window, dilation in A_BRANCHES:
        o, lse = dilated_branch(q, k, v, window, dilation)
        outs.append(o)
        lses.append(lse)
    wts = jax.nn.softmax(jnp.stack(lses, 0), axis=0)
    return jnp.einsum('gbsh,gbshd->bshd', wts, jnp.stack(outs, 0))


def diff_attention(q, k, v, lam, lam_init, out_gain):
    B, S, H, _, hd = q.shape
    scale = hd ** -0.5
    outs = []
    for blk in range(S // Q_BLOCK):
        lo, hi = blk * Q_BLOCK, (blk + 1) * Q_BLOCK
        s = jnp.einsum('bihcd,bjhcd->bhcij', q[:, lo:hi], k[:, :hi]).astype(jnp.float32) * scale
        causal = (lo + jnp.arange(Q_BLOCK))[:, None] >= jnp.arange(hi)[None, :]
        p = jax.nn.softmax(jnp.where(causal, s, NEG_INF), axis=-1)
        a = p[:, :, 0] - lam[None, :, None, None] * p[:, :, 1]
        outs.append(jnp.einsum('bhij,bjhe->bihe', a, v[:, :hi].astype(jnp.float32)))
    o = jnp.concatenate(outs, axis=1)
    return rms_norm(o, out_gain) * (1.0 - lam_init)


def even_mixer(h, w_in, a_q_gain, a_k_gain, b_q_gain, b_k_gain, lq1, lk1, lq2, lk2, b_out_gain, w_out,
               lam_init, cos, sin):
    B, S, _ = h.shape
    proj = h @ w_in
    cuts = [A_WIDTH, 2 * A_WIDTH, 3 * A_WIDTH, 3 * A_WIDTH + B_WIDTH, 3 * A_WIDTH + 2 * B_WIDTH]
    aq, ak, av, bq, bk, bv = jnp.split(proj, cuts, axis=-1)
    aq = qk_norm_rope(aq.reshape(B, S, A_HEADS, HEAD_DIM), a_q_gain, cos, sin)
    ak = qk_norm_rope(ak.reshape(B, S, A_HEADS, HEAD_DIM), a_k_gain, cos, sin)
    av = av.reshape(B, S, A_HEADS, HEAD_DIM)
    oa = dilated_attention(aq, ak, av).reshape(B, S, A_WIDTH)
    bq = qk_norm_rope(bq.reshape(B, S, 2 * B_HEADS, HEAD_DIM), b_q_gain, cos, sin).reshape(B, S, B_HEADS, 2, HEAD_DIM)
    bk = qk_norm_rope(bk.reshape(B, S, 2 * B_HEADS, HEAD_DIM), b_k_gain, cos, sin).reshape(B, S, B_HEADS, 2, HEAD_DIM)
    bv = bv.reshape(B, S, B_HEADS, B_VDIM)
    f32 = jnp.float32
    lam = (jnp.exp(jnp.sum(lq1.astype(f32) * lk1.astype(f32), -1))
           - jnp.exp(jnp.sum(lq2.astype(f32) * lk2.astype(f32), -1)) + lam_init)
    ob = diff_attention(bq, bk, bv, lam, lam_init, b_out_gain).reshape(B, S, B_WIDTH)
    return jnp.concatenate([oa, ob], axis=-1).astype(h.dtype) @ w_out


def causal_depthwise_conv(x, w):
    return lax.conv_general_dilated(x, w[:, None, :].astype(x.dtype), window_strides=(1,),
                                    padding=((w.shape[0] - 1, 0),),
                                    dimension_numbers=('NWC', 'WIO', 'NWC'),
                                    feature_group_count=x.shape[-1])


def l2_normalize(x):
    xf = x.astype(jnp.float32)
    return xf * lax.rsqrt(jnp.sum(xf * xf, axis=-1, keepdims=True) + EPS)


def chunk_gated_delta_rule(q, k, v, g, beta):
    B, S, H, dk = q.shape
    dv = v.shape[-1]
    nc = S // CHUNK
    f32 = jnp.float32

    def chunks(t):
        t = jnp.moveaxis(t.astype(f32), 2, 1)
        return t.reshape(B, H, nc, CHUNK, *t.shape[3:])

    q = chunks(q) * (dk ** -0.5)
    k = chunks(k)
    v = chunks(v)
    beta = chunks(beta)
    g = jnp.cumsum(chunks(g), axis=-1)
    tril = jnp.tril(jnp.ones((CHUNK, CHUNK), bool))
    strict = jnp.tril(jnp.ones((CHUNK, CHUNK), bool), k=-1)
    gdiff = g[..., :, None] - g[..., None, :]
    decay = jnp.where(tril, jnp.exp(jnp.where(tril, gdiff, 0.0)), 0.0)
    k_beta = k * beta[..., None]
    v_beta = v * beta[..., None]
    lmat = jnp.where(strict, jnp.einsum('bhnid,bhnjd->bhnij', k_beta, k) * decay, 0.0)
    eye = jnp.eye(CHUNK, dtype=f32)
    t_inv = lax.linalg.triangular_solve(eye + lmat, jnp.broadcast_to(eye, lmat.shape),
                                        left_side=True, lower=True)
    u = t_inv @ v_beta
    w = t_inv @ (k_beta * jnp.exp(g)[..., None])
    a_intra = jnp.where(tril, jnp.einsum('bhnid,bhnjd->bhnij', q, k) * decay, 0.0)

    def step(state, xs):
        q_c, k_c, u_c, w_c, g_c, a_c = xs
        v_new = u_c - w_c @ state
        o = (q_c * jnp.exp(g_c)[..., None]) @ state + a_c @ v_new
        g_last = g_c[..., -1]
        k_dec = k_c * jnp.exp(g_last[..., None] - g_c)[..., None]
        state = state * jnp.exp(g_last)[..., None, None] + jnp.einsum('bhcd,bhce->bhde', k_dec, v_new)
        return state, o

    xs = (jnp.moveaxis(q, 2, 0), jnp.moveaxis(k, 2, 0), jnp.moveaxis(u, 2, 0),
          jnp.moveaxis(w, 2, 0), jnp.moveaxis(g, 2, 0), jnp.moveaxis(a_intra, 2, 0))
    state0 = jnp.zeros((B, H, dk, dv), f32)
    _, o = lax.scan(step, state0, xs)
    o = jnp.moveaxis(o, 0, 2).reshape(B, H, S, dv)
    return jnp.moveaxis(o, 1, 2)


def odd_mixer(h, w_in, conv_w, a_log, dt_bias, out_gain, w_out):
    B, S, _ = h.shape
    f32 = jnp.float32
    proj = h @ w_in
    qkv = jax.nn.silu(causal_depthwise_conv(proj[..., :3 * C_WIDTH], conv_w))
    q, k, v = jnp.split(qkv, 3, axis=-1)
    z = proj[..., 3 * C_WIDTH:4 * C_WIDTH].reshape(B, S, C_HEADS, C_DV)
    a = proj[..., 4 * C_WIDTH:4 * C_WIDTH + C_HEADS].astype(f32)
    b = proj[..., 4 * C_WIDTH + C_HEADS:].astype(f32)
    q = l2_normalize(q.reshape(B, S, C_HEADS, C_DK))
    k = l2_normalize(k.reshape(B, S, C_HEADS, C_DK))
    v = v.reshape(B, S, C_HEADS, C_DV)
    g = -jnp.exp(a_log.astype(f32)) * jax.nn.softplus(a + dt_bias.astype(f32))
    beta = jax.nn.sigmoid(b)
    o = chunk_gated_delta_rule(q, k, v, g, beta)
    o = rms_norm(o, out_gain) * jax.nn.silu(z.astype(f32))
    return o.reshape(B, S, C_WIDTH).astype(h.dtype) @ w_out


def grouped_expert_ffn(xf, expert_ids, weights, w_gate, w_up, w_down):
    N, D = xf.shape
    E = w_gate.shape[0]
    A = N * TOP_K
    flat_e = expert_ids.reshape(-1).astype(jnp.int32)
    flat_tok = jnp.repeat(jnp.arange(N, dtype=jnp.int32), TOP_K)
    flat_w = weights.reshape(-1).astype(jnp.float32)
    order = jnp.argsort(flat_e)
    se, stok, sw = flat_e[order], flat_tok[order], flat_w[order]
    counts = jnp.bincount(flat_e, length=E)
    padded = (counts + MOE_BLOCK - 1) // MOE_BLOCK * MOE_BLOCK
    pad_end = jnp.cumsum(padded)
    pad_start = pad_end - padded
    seg_start = jnp.cumsum(counts) - counts
    dest = pad_start[se] + (jnp.arange(A, dtype=jnp.int32) - seg_start[se])
    n_blocks = -(-(A + E * (MOE_BLOCK - 1)) // MOE_BLOCK)
    P = n_blocks * MOE_BLOCK
    row_tok = jnp.full((P,), N, jnp.int32).at[dest].set(stok)
    row_w = jnp.zeros((P,), jnp.float32).at[dest].set(sw)
    block_start = jnp.arange(n_blocks, dtype=jnp.int32) * MOE_BLOCK
    block_e = jnp.minimum(jnp.searchsorted(pad_end, block_start, side='right'), E - 1)
    x_pad = jnp.concatenate([xf, jnp.zeros((1, D), xf.dtype)], axis=0)

    def block_ffn(args):
        toks, e = args
        xb = x_pad[toks]
        hb = jax.nn.silu(xb @ w_gate[e]) * (xb @ w_up[e])
        return hb @ w_down[e]

    yb = lax.map(block_ffn, (row_tok.reshape(n_blocks, MOE_BLOCK), block_e))
    y = (yb.reshape(P, D).astype(jnp.float32) * row_w[:, None]).astype(xf.dtype)
    return jnp.zeros((N + 1, D), xf.dtype).at[row_tok].add(y)[:N]


def hier_moe(h, w_group, b_group, w_erouter, b_erouter, w_gate, w_up, w_down):
    B, S, D = h.shape
    f32 = jnp.float32
    xf = h.reshape(B * S, D)
    gl = (xf @ w_group).astype(f32) + b_group.astype(f32)
    grp = jnp.argmax(gl, axis=-1)
    g_w = jnp.max(jax.nn.softmax(gl, axis=-1), axis=-1)
    el = jnp.einsum('nd,gde->nge', xf, w_erouter).astype(f32) + b_erouter.astype(f32)
    el = jnp.einsum('nge,ng->ne', el, jax.nn.one_hot(grp, N_GROUPS, dtype=f32))
    top_v, top_i = lax.top_k(el, TOP_K)
    weights = g_w[:, None] * jax.nn.softmax(top_v, axis=-1)
    expert_ids = grp[:, None].astype(jnp.int32) * EXPERTS_PER_GROUP + top_i.astype(jnp.int32)
    y = grouped_expert_ffn(xf, expert_ids, weights, w_gate, w_up, w_down)
    return y.reshape(B, S, D)


def setup_inputs(seed: int = 0) -> dict:
    key = jax.random.key(seed)
    ks = list(jax.random.split(key, 32))
    n_even = (DEPTH + 1) // 2
    n_odd = DEPTH // 2
    f32 = jnp.float32

    def nrm(shape, scale):
        return scale * jax.random.normal(ks.pop(0), shape, f32)

    def gain(shape):
        return 1.0 + 0.02 * jax.random.normal(ks.pop(0), shape, f32)

    u = jax.random.uniform(ks.pop(0), (n_odd, C_HEADS), f32)
    dt = jnp.exp(u * (math.log(0.1) - math.log(0.001)) + math.log(0.001))
    dt_bias = dt + jnp.log(-jnp.expm1(-dt))
    a_log = jnp.log(jax.random.uniform(ks.pop(0), (n_odd, C_HEADS), f32, minval=1.0, maxval=16.0))
    return {
        'x': nrm((BATCH, SEQ, D_MODEL), 1.0),
        'even_attn_norm': gain((n_even, D_MODEL)),
        'even_w_in': nrm((n_even, D_MODEL, EVEN_IN), D_MODEL ** -0.5),
        'a_q_norm': gain((n_even, HEAD_DIM)),
        'a_k_norm': gain((n_even, HEAD_DIM)),
        'b_q_norm': gain((n_even, HEAD_DIM)),
        'b_k_norm': gain((n_even, HEAD_DIM)),
        'b_lambda_q1': nrm((n_even, B_HEADS, HEAD_DIM), 0.1),
        'b_lambda_k1': nrm((n_even, B_HEADS, HEAD_DIM), 0.1),
        'b_lambda_q2': nrm((n_even, B_HEADS, HEAD_DIM), 0.1),
        'b_lambda_k2': nrm((n_even, B_HEADS, HEAD_DIM), 0.1),
        'b_out_norm': gain((n_even, B_VDIM)),
        'even_w_out': nrm((n_even, A_WIDTH + B_WIDTH, D_MODEL), (A_WIDTH + B_WIDTH) ** -0.5),
        'odd_attn_norm': gain((n_odd, D_MODEL)),
        'odd_w_in': nrm((n_odd, D_MODEL, ODD_IN), D_MODEL ** -0.5),
        'odd_conv_w': nrm((n_odd, CONV_W, 3 * C_WIDTH), CONV_W ** -0.5),
        'odd_a_log': a_log,
        'odd_dt_bias': dt_bias,
        'odd_out_norm': gain((n_odd, C_DV)),
        'odd_w_out': nrm((n_odd, C_WIDTH, D_MODEL), C_WIDTH ** -0.5),
        'ffn_norm': gain((DEPTH, D_MODEL)),
        'router_group_w': nrm((DEPTH, D_MODEL, N_GROUPS), D_MODEL ** -0.5),
        'router_group_b': nrm((DEPTH, N_GROUPS), 0.01),
        'router_expert_w': nrm((DEPTH, N_GROUPS, D_MODEL, EXPERTS_PER_GROUP), D_MODEL ** -0.5),
        'router_expert_b': nrm((DEPTH, N_GROUPS, EXPERTS_PER_GROUP), 0.01),
        'expert_w_gate': nrm((DEPTH, N_EXPERTS, D_MODEL, EXPERT_FF), D_MODEL ** -0.5),
        'expert_w_up': nrm((DEPTH, N_EXPERTS, D_MODEL, EXPERT_FF), D_MODEL ** -0.5),
        'expert_w_down': nrm((DEPTH, N_EXPERTS, EXPERT_FF, D_MODEL), EXPERT_FF ** -0.5),
    }


def reference(x, even_attn_norm, even_w_in, a_q_norm, a_k_norm, b_q_norm, b_k_norm,
              b_lambda_q1, b_lambda_k1, b_lambda_q2, b_lambda_k2, b_out_norm, even_w_out,
              odd_attn_norm, odd_w_in, odd_conv_w, odd_a_log, odd_dt_bias, odd_out_norm, odd_w_out,
              ffn_norm, router_group_w, router_group_b, router_expert_w, router_expert_b,
              expert_w_gate, expert_w_up, expert_w_down):
    S = x.shape[1]
    cos, sin = rope_tables(S, HEAD_DIM)
    for layer in range(DEPTH):
        i = layer // 2
        if layer % 2 == 0:
            lam_init = 0.8 - 0.6 * math.exp(-0.3 * layer)
            h = rms_norm(x, even_attn_norm[i])
            x = x + even_mixer(h, even_w_in[i], a_q_norm[i], a_k_norm[i], b_q_norm[i], b_k_norm[i],
                               b_lambda_q1[i], b_lambda_k1[i], b_lambda_q2[i], b_lambda_k2[i],
                               b_out_norm[i], even_w_out[i], lam_init, cos, sin)
        else:
            h = rms_norm(x, odd_attn_norm[i])
            x = x + odd_mixer(h, odd_w_in[i], odd_conv_w[i], odd_a_log[i], odd_dt_bias[i],
                              odd_out_norm[i], odd_w_out[i])
        h = rms_norm(x, ffn_norm[layer])
        x = x + hier_moe(h, router_group_w[layer], router_group_b[layer], router_expert_w[layer],
                         router_expert_b[layer], expert_w_gate[layer], expert_w_up[layer],
                         expert_w_down[layer])
    return x
```

```python
import functools
import math

import jax
import jax.numpy as jnp
from jax import lax
from jax.experimental import pallas as pl
from jax.experimental.pallas import tpu as pltpu

F32 = jnp.float32
BF16 = jnp.bfloat16

D_MODEL = 1024
HEAD_DIM = 64
ROPE_THETA = 10000.0
A_HEADS = 8
A_BRANCHES = ((128, 1), (512, 4), (2048, 16))
A_BLOCK = 128
B_HEADS = 4
A_WIDTH = 512
B_WIDTH = 512
EVEN_IN = 3 * A_WIDTH + 3 * B_WIDTH
C_HEADS = 8
C_DK = 128
C_WIDTH = 1024
CONV_W = 4
CHUNK = 64
N_GROUPS = 4
EXPERTS_PER_GROUP = 8
N_EXPERTS = 32
TOP_K = 2
EXPERT_FF = 512
EPS = 1e-6
NEG_INF = -1e30

LANES = 128
VMEM_LIMIT = 48 * 1024 * 1024

ROW_TILE = 512
MOE_TILE = 256
DIFF_TQ = 256
GDN_ROWS = 512
SOLVE_LANES = 128


def _params(*sem):
    return pltpu.CompilerParams(dimension_semantics=sem, vmem_limit_bytes=VMEM_LIMIT)


def _dot(a, b):
    return jnp.dot(a, b, preferred_element_type=F32)


def _dot_nt(a, b):
    return lax.dot_general(a, b, (((1,), (1,)), ((), ())), preferred_element_type=F32)


def _dot_tn(a, b):
    return lax.dot_general(a, b, (((0,), (0,)), ((), ())), preferred_element_type=F32)


def _split_dot(a_f32, b_bf16):
    hi = a_f32.astype(BF16)
    lo = (a_f32 - hi.astype(F32)).astype(BF16)
    return _dot(hi, b_bf16) + _dot(lo, b_bf16)


def _sigmoid(x):
    return 1.0 / (1.0 + jnp.exp(-x))


def _rms_rows(xf, gain):
    ms = jnp.mean(xf * xf, axis=-1, keepdims=True)
    return xf * lax.rsqrt(ms + EPS) * gain


def _norm_matmul_kernel(x_ref, g_ref, w_ref, o_ref, *, col_chunk):
    h = _rms_rows(x_ref[...], g_ref[...]).astype(BF16)
    for c in range(0, o_ref.shape[1], col_chunk):
        o_ref[:, c:c + col_chunk] = _dot(h, w_ref[:, c:c + col_chunk]).astype(o_ref.dtype)


def norm_matmul(x, gain, w_bf16, col_chunk):
    n, d = x.shape
    m = w_bf16.shape[1]
    return pl.pallas_call(
        functools.partial(_norm_matmul_kernel, col_chunk=col_chunk),
        grid=(n // ROW_TILE,),
        in_specs=[pl.BlockSpec((ROW_TILE, d), lambda i: (i, 0)),
                  pl.BlockSpec((1, d), lambda i: (0, 0)),
                  pl.BlockSpec((d, m), lambda i: (0, 0))],
        out_specs=pl.BlockSpec((ROW_TILE, m), lambda i: (i, 0)),
        out_shape=jax.ShapeDtypeStruct((n, m), BF16),
        compiler_params=_params("parallel"),
        name="norm_matmul",
    )(x, gain.reshape(1, d), w_bf16)


def _qk_prep_kernel(p_ref, g_ref, cos_ref, sin_ref, o_ref):
    lane = lax.broadcasted_iota(jnp.int32, (1, LANES), 1)
    first_half = (lane % HEAD_DIM) < (HEAD_DIM // 2)
    r = lax.broadcasted_iota(jnp.int32, (LANES, LANES), 0) // HEAD_DIM
    c = lax.broadcasted_iota(jnp.int32, (LANES, LANES), 1) // HEAD_DIM
    same_head = jnp.where(r == c, 1.0, 0.0).astype(BF16)
    cos = cos_ref[...]
    sin = sin_ref[...]
    gain = g_ref[0]
    for s in range(p_ref.shape[1] // LANES):
        sl = slice(s * LANES, (s + 1) * LANES)
        x = p_ref[:, sl].astype(F32)
        ssq = _split_dot(x * x, same_head)
        y = x * lax.rsqrt(ssq * (1.0 / HEAD_DIM) + EPS) * gain[:, sl]
        partner = jnp.where(first_half,
                            pltpu.roll(y, LANES - HEAD_DIM // 2, axis=1),
                            pltpu.roll(y, HEAD_DIM // 2, axis=1))
        o_ref[:, sl] = (y * cos + partner * sin).astype(o_ref.dtype)


def qk_prep(proj, gains, cos_t, sin_t, seq):
    n = proj.shape[0]
    tiles_per_seq = seq // ROW_TILE
    return pl.pallas_call(
        _qk_prep_kernel,
        grid=(n // ROW_TILE, 4),
        in_specs=[pl.BlockSpec((ROW_TILE, 512), lambda i, g: (i, g + g // 2)),
                  pl.BlockSpec((1, 1, 512), lambda i, g: (g, 0, 0)),
                  pl.BlockSpec((ROW_TILE, LANES), lambda i, g: (i % tiles_per_seq, 0)),
                  pl.BlockSpec((ROW_TILE, LANES), lambda i, g: (i % tiles_per_seq, 0))],
        out_specs=pl.BlockSpec((ROW_TILE, 512), lambda i, g: (i, g)),
        out_shape=jax.ShapeDtypeStruct((n, 2048), BF16),
        compiler_params=_params("parallel", "parallel"),
        name="qk_prep",
    )(proj, gains, cos_t, sin_t)


def _dilated_kernel(*refs, has_prev, is_last):
    q_ref, kp_ref, kc_ref, vp_ref, vc_ref = refs[:5]
    refs = refs[5:]
    if has_prev:
        op_ref, lp_ref = refs[:2]
        refs = refs[2:]
    o_ref = refs[0]
    l_ref = None if is_last else refs[1]

    blk = pl.program_id(2)
    nb = A_BLOCK
    lane = lax.broadcasted_iota(jnp.int32, (1, LANES), 1)
    low = lane < HEAD_DIM
    qi = lax.broadcasted_iota(jnp.int32, (2 * nb, 2 * nb), 0) % nb
    kj = lax.broadcasted_iota(jnp.int32, (2 * nb, 2 * nb), 1)
    valid = (kj >= qi) & (kj <= qi + nb) & ((blk > 0) | (kj >= nb))
    zero = jnp.zeros((), BF16)
    for hp in range(A_WIDTH // LANES):
        sl = slice(hp * LANES, (hp + 1) * LANES)
        q = q_ref[:, sl]
        q2 = jnp.concatenate([jnp.where(low, q, zero), jnp.where(low, zero, q)], axis=0)
        kk = jnp.concatenate([kp_ref[:, sl], kc_ref[:, sl]], axis=0)
        vv = jnp.concatenate([vp_ref[:, sl], vc_ref[:, sl]], axis=0)
        s = jnp.where(valid, _dot_nt(q2, kk), NEG_INF)
        m = jnp.max(s, axis=1, keepdims=True)
        p = jnp.exp(s - m)
        den = jnp.sum(p, axis=1, keepdims=True)
        o2 = _dot(p.astype(BF16), vv) / den
        lse2 = m + jnp.log(den)
        o = jnp.where(low, o2[:nb], o2[nb:])
        lse = jnp.where(low, lse2[:nb], lse2[nb:])
        if has_prev:
            o_prev = op_ref[:, sl]
            l_prev = lp_ref[:, sl]
            top = jnp.maximum(lse, l_prev)
            wa = jnp.exp(lse - top)
            wb = jnp.exp(l_prev - top)
            tot = wa + wb
            o = (wa * o + wb * o_prev) / tot
            lse = top + jnp.log(tot)
        o_ref[:, sl] = o.astype(o_ref.dtype)
        if not is_last:
            l_ref[:, sl] = lse


def dilated_branch(qk, proj, prev, dilation, batch, seq, is_last):
    n = batch * seq
    ln = seq // dilation
    nblk = ln // A_BLOCK
    qk3 = qk.reshape(batch, ln, dilation * 2048)
    pj3 = proj.reshape(batch, ln, dilation * EVEN_IN)
    blk = (None, A_BLOCK, A_WIDTH)

    def cur(col, per):
        return pl.BlockSpec(blk, lambda b, r, c: (b, c, r * per + col))

    def prv(col, per):
        return pl.BlockSpec(blk, lambda b, r, c: (b, jnp.maximum(c - 1, 0), r * per + col))

    plain = pl.BlockSpec(blk, lambda b, r, c: (b, c, r))
    in_specs = [cur(0, 4), prv(1, 4), cur(1, 4), prv(2, 6), cur(2, 6)]
    args = [qk3, qk3, qk3, pj3, pj3]
    if prev is not None:
        in_specs += [plain, plain]
        args += [prev[0].reshape(batch, ln, dilation * A_WIDTH),
                 prev[1].reshape(batch, ln, dilation * A_WIDTH)]
    o_shape = jax.ShapeDtypeStruct((batch, ln, dilation * A_WIDTH), BF16 if is_last else F32)
    if is_last:
        out_specs, out_shape = plain, o_shape
    else:
        out_specs = [plain, plain]
        out_shape = [o_shape, jax.ShapeDtypeStruct((batch, ln, dilation * A_WIDTH), F32)]
    res = pl.pallas_call(
        functools.partial(_dilated_kernel, has_prev=prev is not None, is_last=is_last),
        grid=(batch, dilation, nblk),
        in_specs=in_specs, out_specs=out_specs, out_shape=out_shape,
        compiler_params=_params("parallel", "parallel", "parallel"),
        name=f"dilated_d{dilation}",
    )(*args)
    if is_last:
        return res.reshape(n, A_WIDTH)
    return res[0].reshape(n, A_WIDTH), res[1].reshape(n, A_WIDTH)


def _diff_attn_kernel(q_ref, k_ref, v_ref, lq1_ref, lk1_ref, lq2_ref, lk2_ref, g_ref, o_ref,
                      m_scr, l_scr, acc_scr, *, lam_init):
    h = pl.program_id(1)
    i = pl.program_id(2)
    tq = DIFF_TQ
    lane = lax.broadcasted_iota(jnp.int32, (1, LANES), 1)
    low = lane < HEAD_DIM
    zero = jnp.zeros((), BF16)
    q = q_ref[...]
    q2 = jnp.concatenate([jnp.where(low, q, zero), jnp.where(low, zero, q)], axis=0)
    m_scr[...] = jnp.full(m_scr.shape, NEG_INF, F32)
    l_scr[...] = jnp.zeros(l_scr.shape, F32)
    acc_scr[...] = jnp.zeros(acc_scr.shape, F32)

    def step(j, masked):
        start = pl.multiple_of(j * tq, tq)
        k = k_ref[pl.ds(start, tq), :]
        v = v_ref[pl.ds(start, tq), :]
        s = _dot_nt(q2, k)
        if masked:
            qi = lax.broadcasted_iota(jnp.int32, (2 * tq, tq), 0) % tq
            kj = lax.broadcasted_iota(jnp.int32, (2 * tq, tq), 1)
            s = jnp.where(kj <= qi, s, NEG_INF)
        m_old = m_scr[...]
        m_new = jnp.maximum(m_old, jnp.max(s, axis=1, keepdims=True))
        alpha = jnp.exp(m_old - m_new)
        p = jnp.exp(s - m_new)
        l_scr[...] = alpha * l_scr[...] + jnp.sum(p, axis=1, keepdims=True)
        acc_scr[...] = alpha * acc_scr[...] + _dot(p.astype(BF16), v)
        m_scr[...] = m_new

    def body(j, carry):
        step(j, False)
        return carry

    lax.fori_loop(0, i, body, 0)
    step(i, True)

    acc = acc_scr[...] / l_scr[...]
    d1 = jnp.sum(lq1_ref[pl.ds(h, 1), :] * lk1_ref[pl.ds(h, 1), :], axis=1, keepdims=True)
    d2 = jnp.sum(lq2_ref[pl.ds(h, 1), :] * lk2_ref[pl.ds(h, 1), :], axis=1, keepdims=True)
    lam = jnp.exp(d1) - jnp.exp(d2) + lam_init
    o = acc[:tq] - lam * acc[tq:]
    o_ref[...] = (_rms_rows(o, g_ref[...]) * (1.0 - lam_init)).astype(o_ref.dtype)


def diff_attention(qk, proj, lq1, lk1, lq2, lk2, out_gain, lam_init, batch, seq):
    n = batch * seq
    qk3 = qk.reshape(batch, seq, 2048)
    pj3 = proj.reshape(batch, seq, EVEN_IN)
    qcol = 2 * A_WIDTH // LANES
    kcol = (2 * A_WIDTH + B_WIDTH) // LANES
    vcol = (3 * A_WIDTH + 2 * B_WIDTH) // LANES
    small = pl.BlockSpec((B_HEADS, HEAD_DIM), lambda b, h, i: (0, 0))
    out = pl.pallas_call(
        functools.partial(_diff_attn_kernel, lam_init=lam_init),
        grid=(batch, B_HEADS, seq // DIFF_TQ),
        in_specs=[pl.BlockSpec((None, DIFF_TQ, LANES), lambda b, h, i: (b, i, qcol + h)),
                  pl.BlockSpec((None, seq, LANES), lambda b, h, i: (b, 0, kcol + h)),
                  pl.BlockSpec((None, seq, LANES), lambda b, h, i: (b, 0, vcol + h)),
                  small, small, small, small,
                  pl.BlockSpec((1, LANES), lambda b, h, i: (0, 0))],
        out_specs=pl.BlockSpec((None, DIFF_TQ, LANES), lambda b, h, i: (b, i, h)),
        out_shape=jax.ShapeDtypeStruct((batch, seq, B_WIDTH), BF16),
        scratch_shapes=[pltpu.VMEM((2 * DIFF_TQ, 1), F32),
                        pltpu.VMEM((2 * DIFF_TQ, 1), F32),
                        pltpu.VMEM((2 * DIFF_TQ, LANES), F32)],
        compiler_params=_params("parallel", "parallel", "parallel"),
        name="diff_attention",
    )(qk3, qk3, pj3, lq1, lk1, lq2, lk2, out_gain.reshape(1, LANES))
    return out.reshape(n, B_WIDTH)


ROUTE_E1, ROUTE_E2, ROUTE_W1, ROUTE_W2 = 0, 1, 2, 3
EXPERT_LANE0 = 8


def _outproj_router_kernel(x_ref, a_ref, b_ref, wa_ref, wb_ref, g_ref, rhi_ref, rlo_ref, rb_ref,
                           xo_ref, h_ref, rt_ref):
    y = x_ref[...] + _dot(a_ref[...], wa_ref[...]) + _dot(b_ref[...], wb_ref[...])
    xo_ref[...] = y
    hn = _rms_rows(y, g_ref[...])
    h_ref[...] = hn
    hi = hn.astype(BF16)
    lo = (hn - hi.astype(F32)).astype(BF16)
    logits = (_dot(hi, rhi_ref[...]) + _dot(lo, rhi_ref[...]) + _dot(hi, rlo_ref[...])) + rb_ref[...]

    lane = lax.broadcasted_iota(jnp.int32, logits.shape, 1)
    big = jnp.int32(LANES)
    is_group = lane < N_GROUPS
    gl = jnp.where(is_group, logits, NEG_INF)
    gmax = jnp.max(gl, axis=1, keepdims=True)
    grp = jnp.min(jnp.where(is_group & (gl == gmax), lane, big), axis=1, keepdims=True)
    g_w = 1.0 / jnp.sum(jnp.where(is_group, jnp.exp(gl - gmax), 0.0), axis=1, keepdims=True)
    first = EXPERT_LANE0 + grp * EXPERTS_PER_GROUP
    in_group = (lane >= first) & (lane < first + EXPERTS_PER_GROUP)
    el = jnp.where(in_group, logits, NEG_INF)
    v1 = jnp.max(el, axis=1, keepdims=True)
    i1 = jnp.min(jnp.where(in_group & (el == v1), lane, big), axis=1, keepdims=True)
    rest = in_group & (lane != i1)
    el2 = jnp.where(rest, logits, NEG_INF)
    v2 = jnp.max(el2, axis=1, keepdims=True)
    i2 = jnp.min(jnp.where(rest & (el2 == v2), lane, big), axis=1, keepdims=True)
    e2w = jnp.exp(v2 - v1)
    w1 = g_w / (1.0 + e2w)
    w2 = g_w * e2w / (1.0 + e2w)
    e1 = (i1 - EXPERT_LANE0).astype(F32)
    e2 = (i2 - EXPERT_LANE0).astype(F32)
    rt_ref[...] = jnp.where(lane == ROUTE_E1, e1,
                            jnp.where(lane == ROUTE_E2, e2,
                                      jnp.where(lane == ROUTE_W1, w1,
                                                jnp.where(lane == ROUTE_W2, w2, 0.0))))


def outproj_router(x, mix_a, blk_a, mix_b, blk_b, w_out_bf16, ffn_gain, r_hi, r_lo, r_bias):
    n, d = x.shape
    half = w_out_bf16.shape[0] // 2
    row = lambda i: (i, 0)
    const = lambda i: (0, 0)
    return pl.pallas_call(
        _outproj_router_kernel,
        grid=(n // ROW_TILE,),
        in_specs=[pl.BlockSpec((ROW_TILE, d), row),
                  pl.BlockSpec((ROW_TILE, half), lambda i: (i, blk_a)),
                  pl.BlockSpec((ROW_TILE, half), lambda i: (i, blk_b)),
                  pl.BlockSpec((half, d), lambda i: (0, 0)),
                  pl.BlockSpec((half, d), lambda i: (1, 0)),
                  pl.BlockSpec((1, d), const),
                  pl.BlockSpec((d, LANES), const),
                  pl.BlockSpec((d, LANES), const),
                  pl.BlockSpec((1, LANES), const)],
        out_specs=[pl.BlockSpec((ROW_TILE, d), row),
                   pl.BlockSpec((ROW_TILE, d), row),
                   pl.BlockSpec((ROW_TILE, LANES), row)],
        out_shape=[jax.ShapeDtypeStruct((n, d), F32),
                   jax.ShapeDtypeStruct((n, d), F32),
                   jax.ShapeDtypeStruct((n, LANES), F32)],
        compiler_params=_params("parallel"),
        name="outproj_router",
    )(x, mix_a, mix_b, w_out_bf16, w_out_bf16, ffn_gain.reshape(1, d), r_hi, r_lo, r_bias)


def _rank_kernel(rt_ref, tri_ref, rk_ref, cnt_ref, carry_scr):
    @pl.when(pl.program_id(0) == 0)
    def _():
        carry_scr[...] = jnp.zeros(carry_scr.shape, F32)

    rt = rt_ref[...]
    lane = lax.broadcasted_iota(jnp.int32, rt.shape, 1)
    e1 = rt[:, ROUTE_E1:ROUTE_E1 + 1].astype(jnp.int32)
    e2 = rt[:, ROUTE_E2:ROUTE_E2 + 1].astype(jnp.int32)
    oh1 = lane == e1
    oh2 = lane == e2
    both = jnp.where(oh1 | oh2, 1.0, 0.0)
    before = _dot(tri_ref[...], both.astype(BF16)) + carry_scr[0:1, :]
    r1 = jnp.sum(jnp.where(oh1, before, 0.0), axis=1, keepdims=True)
    r2 = jnp.sum(jnp.where(oh2, before, 0.0), axis=1, keepdims=True)
    rk_ref[...] = jnp.where(lane == 0, r1, jnp.where(lane == 1, r2, 0.0))
    total = carry_scr[0:1, :] + jnp.sum(both, axis=0, keepdims=True)
    carry_scr[...] = jnp.broadcast_to(total, carry_scr.shape)
    cnt_ref[...] = jnp.broadcast_to(total, cnt_ref.shape)


def rank_slots(route):
    n = route.shape[0]
    tri = jnp.tril(jnp.ones((ROW_TILE, ROW_TILE), BF16), k=-1)
    return pl.pallas_call(
        _rank_kernel,
        grid=(n // ROW_TILE,),
        in_specs=[pl.BlockSpec((ROW_TILE, LANES), lambda i: (i, 0)),
                  pl.BlockSpec((ROW_TILE, ROW_TILE), lambda i: (0, 0))],
        out_specs=[pl.BlockSpec((ROW_TILE, LANES), lambda i: (i, 0)),
                   pl.BlockSpec((8, LANES), lambda i: (0, 0))],
        out_shape=[jax.ShapeDtypeStruct((n, LANES), F32),
                   jax.ShapeDtypeStruct((8, LANES), F32)],
        scratch_shapes=[pltpu.VMEM((8, LANES), F32)],
        compiler_params=_params("arbitrary"),
        name="moe_rank",
    )(route, tri)


DISPATCH_TOKENS = 2048


def _dispatch_kernel(dest_ref, h_ref, zero_ref, xs_ref, sem):
    del zero_ref
    base = pl.program_id(0) * DISPATCH_TOKENS

    def row_copy(t, k):
        return pltpu.make_async_copy(h_ref.at[pl.ds(t, 1)],
                                     xs_ref.at[pl.ds(dest_ref[TOP_K * t + k], 1)], sem)

    def issue(i, carry):
        for k in range(TOP_K):
            row_copy(base + i, k).start()
        return carry

    lax.fori_loop(0, DISPATCH_TOKENS, issue, 0)

    def drain(i, carry):
        for k in range(TOP_K):
            row_copy(base + i, k).wait()
        return carry

    lax.fori_loop(0, DISPATCH_TOKENS, drain, 0)


def dispatch_rows(h, dest, n_rows):
    n, d = h.shape
    return pl.pallas_call(
        _dispatch_kernel,
        grid_spec=pltpu.PrefetchScalarGridSpec(
            num_scalar_prefetch=1,
            grid=(n // DISPATCH_TOKENS,),
            in_specs=[pl.BlockSpec(memory_space=pl.ANY), pl.BlockSpec(memory_space=pl.ANY)],
            out_specs=pl.BlockSpec(memory_space=pl.ANY),
            scratch_shapes=[pltpu.SemaphoreType.DMA(())]),
        out_shape=jax.ShapeDtypeStruct((n_rows, d), F32),
        input_output_aliases={2: 0},
        compiler_params=_params("arbitrary"),
        name="moe_dispatch",
    )(dest, h, jnp.zeros((n_rows, d), F32))


def _expert_kernel(be_ref, nb_ref, x_ref, wg_ref, wu_ref, wd_ref, y_ref):
    del be_ref

    @pl.when(pl.program_id(0) < nb_ref[0])
    def _():
        xb = x_ref[...].astype(BF16)
        gate = _dot(xb, wg_ref[...])
        up = _dot(xb, wu_ref[...])
        hidden = (gate * _sigmoid(gate) * up).astype(BF16)
        y_ref[...] = _dot(hidden, wd_ref[...])

    @pl.when(pl.program_id(0) >= nb_ref[0])
    def _():
        y_ref[...] = jnp.zeros(y_ref.shape, F32)


def expert_ffn(xs, block_expert, n_used, w_gate, w_up, w_down):
    p, d = xs.shape
    ff = w_gate.shape[2]
    return pl.pallas_call(
        _expert_kernel,
        grid_spec=pltpu.PrefetchScalarGridSpec(
            num_scalar_prefetch=2,
            grid=(p // MOE_TILE,),
            in_specs=[pl.BlockSpec((MOE_TILE, d), lambda i, be, nb: (i, 0)),
                      pl.BlockSpec((None, d, ff), lambda i, be, nb: (be[i], 0, 0)),
                      pl.BlockSpec((None, d, ff), lambda i, be, nb: (be[i], 0, 0)),
                      pl.BlockSpec((None, ff, d), lambda i, be, nb: (be[i], 0, 0))],
            out_specs=pl.BlockSpec((MOE_TILE, d), lambda i, be, nb: (i, 0))),
        out_shape=jax.ShapeDtypeStruct((p, d), F32),
        compiler_params=_params("arbitrary"),
        name="moe_experts",
    )(block_expert, n_used, xs, w_gate, w_up, w_down)


COMBINE_TOKENS = 512


def _combine_kernel(dest_ref, x_ref, rt_ref, y_ref, o_ref, buf0, buf1, sem):
    base = pl.program_id(0) * COMBINE_TOKENS
    bufs = (buf0, buf1)

    def row_copy(i, k):
        return pltpu.make_async_copy(y_ref.at[pl.ds(dest_ref[TOP_K * (base + i) + k], 1)],
                                     bufs[k].at[pl.ds(i, 1)], sem)

    def issue(i, carry):
        for k in range(TOP_K):
            row_copy(i, k).start()
        return carry

    lax.fori_loop(0, COMBINE_TOKENS, issue, 0)

    def drain(i, carry):
        for k in range(TOP_K):
            row_copy(i, k).wait()
        return carry

    lax.fori_loop(0, COMBINE_TOKENS, drain, 0)
    rt = rt_ref[...]
    w1 = rt[:, ROUTE_W1:ROUTE_W1 + 1]
    w2 = rt[:, ROUTE_W2:ROUTE_W2 + 1]
    o_ref[...] = x_ref[...] + (w1 * buf0[...] + w2 * buf1[...])


def combine_rows(x, route, y, dest):
    n, d = x.shape
    return pl.pallas_call(
        _combine_kernel,
        grid_spec=pltpu.PrefetchScalarGridSpec(
            num_scalar_prefetch=1,
            grid=(n // COMBINE_TOKENS,),
            in_specs=[pl.BlockSpec((COMBINE_TOKENS, d), lambda i, dst: (i, 0)),
                      pl.BlockSpec((COMBINE_TOKENS, LANES), lambda i, dst: (i, 0)),
                      pl.BlockSpec(memory_space=pl.ANY)],
            out_specs=pl.BlockSpec((COMBINE_TOKENS, d), lambda i, dst: (i, 0)),
            scratch_shapes=[pltpu.VMEM((COMBINE_TOKENS, d), F32),
                            pltpu.VMEM((COMBINE_TOKENS, d), F32),
                            pltpu.SemaphoreType.DMA(())]),
        out_shape=jax.ShapeDtypeStruct((n, d), F32),
        compiler_params=_params("arbitrary"),
        name="moe_combine",
    )(dest, x, route, y)


def moe_layer(x, h, route, w_gate, w_up, w_down):
    n = x.shape[0]
    ranks, counts = rank_slots(route)
    counts = counts[0, :N_EXPERTS].astype(jnp.int32)
    padded = (counts + MOE_TILE - 1) // MOE_TILE * MOE_TILE
    pad_end = jnp.cumsum(padded)
    pad_start = pad_end - padded
    n_blocks = -(-(n * TOP_K + N_EXPERTS * (MOE_TILE - 1)) // MOE_TILE)
    eids = route[:, ROUTE_E1:ROUTE_E2 + 1].astype(jnp.int32)
    slot0 = jnp.sum(jnp.where(eids[..., None] == jnp.arange(N_EXPERTS), pad_start, 0), axis=-1)
    dest = (slot0 + ranks[:, :TOP_K].astype(jnp.int32)).reshape(-1)
    block_start = jnp.arange(n_blocks, dtype=jnp.int32) * MOE_TILE
    block_expert = jnp.minimum(
        jnp.sum(block_start[:, None] >= pad_end[None, :], axis=1), N_EXPERTS - 1).astype(jnp.int32)
    n_used = (pad_end[-1:] // MOE_TILE).astype(jnp.int32)
    xs = dispatch_rows(h, dest, n_blocks * MOE_TILE)
    y = expert_ffn(xs, block_expert, n_used, w_gate, w_up, w_down)
    return combine_rows(x, route, y, dest)


def router_tables(w_group, b_group, w_erouter, b_erouter):
    d = w_group.shape[0]
    w_exp = jnp.transpose(w_erouter, (1, 0, 2)).reshape(d, N_EXPERTS)
    w = jnp.zeros((d, LANES), F32)
    w = w.at[:, :N_GROUPS].set(w_group.astype(F32))
    w = w.at[:, EXPERT_LANE0:EXPERT_LANE0 + N_EXPERTS].set(w_exp.astype(F32))
    bias = jnp.zeros((1, LANES), F32)
    bias = bias.at[0, :N_GROUPS].set(b_group.astype(F32))
    bias = bias.at[0, EXPERT_LANE0:EXPERT_LANE0 + N_EXPERTS].set(b_erouter.reshape(-1).astype(F32))
    hi = w.astype(BF16)
    lo = (w - hi.astype(F32)).astype(BF16)
    return hi, lo, bias


def _gates_kernel(ab_ref, a_ref, dt_ref, o_ref):
    x = ab_ref[...].astype(F32)
    lane = lax.broadcasted_iota(jnp.int32, x.shape, 1)
    z = x + dt_ref[...]
    softplus = jnp.maximum(z, 0.0) + jnp.log(1.0 + jnp.exp(-jnp.abs(z)))
    o_ref[...] = jnp.where(lane < C_HEADS, -jnp.exp(a_ref[...]) * softplus, _sigmoid(x))


def gdn_gates(proj, a_log, dt_bias):
    n = proj.shape[0]
    pad = lambda v: jnp.zeros((1, LANES), F32).at[0, :C_HEADS].set(v.astype(F32))
    return pl.pallas_call(
        _gates_kernel,
        grid=(n // ROW_TILE,),
        in_specs=[pl.BlockSpec((ROW_TILE, LANES), lambda i: (i, 4 * C_WIDTH // LANES)),
                  pl.BlockSpec((1, LANES), lambda i: (0, 0)),
                  pl.BlockSpec((1, LANES), lambda i: (0, 0))],
        out_specs=pl.BlockSpec((ROW_TILE, LANES), lambda i: (i, 0)),
        out_shape=jax.ShapeDtypeStruct((n, LANES), F32),
        compiler_params=_params("parallel"),
        name="gdn_gates",
    )(proj, pad(a_log), pad(dt_bias))


CONV_COLS = 256
CONV_ROWS = 512
CONV_HALO = 8


def _conv_kernel(x_ref, w_ref, o_ref, pad_scr, *, normalize, scale, scale_blocks):
    seq = x_ref.shape[0]
    out_scale = jnp.where(pl.program_id(1) < scale_blocks, scale, 1.0)
    pad_scr[0:CONV_HALO, :] = jnp.zeros((CONV_HALO, CONV_COLS), F32)
    pad_scr[CONV_HALO:, :] = x_ref[...].astype(F32)
    w = w_ref[...]
    for r0 in range(0, seq, CONV_ROWS):
        acc = None
        for j in range(CONV_W):
            start = CONV_HALO + r0 - (CONV_W - 1) + j
            term = pad_scr[start:start + CONV_ROWS, :] * w[j:j + 1, :]
            acc = term if acc is None else acc + term
        y = acc * _sigmoid(acc)
        if normalize:
            for s in range(CONV_COLS // LANES):
                ys = y[:, s * LANES:(s + 1) * LANES]
                ssq = jnp.sum(ys * ys, axis=1, keepdims=True)
                o_ref[r0:r0 + CONV_ROWS, s * LANES:(s + 1) * LANES] = (
                    ys * lax.rsqrt(ssq + EPS) * out_scale).astype(o_ref.dtype)
        else:
            o_ref[r0:r0 + CONV_ROWS, :] = y.astype(o_ref.dtype)


def conv_silu(proj3, conv_w, col0, width, normalize, scale=1.0, scale_width=0):
    batch, seq, _ = proj3.shape
    cb0 = col0 // CONV_COLS
    return pl.pallas_call(
        functools.partial(_conv_kernel, normalize=normalize, scale=scale,
                          scale_blocks=scale_width // CONV_COLS),
        grid=(batch, width // CONV_COLS),
        in_specs=[pl.BlockSpec((None, seq, CONV_COLS), lambda b, c: (b, 0, cb0 + c)),
                  pl.BlockSpec((CONV_W, CONV_COLS), lambda b, c: (0, cb0 + c))],
        out_specs=pl.BlockSpec((None, seq, CONV_COLS), lambda b, c: (b, 0, c)),
        out_shape=jax.ShapeDtypeStruct((batch, seq, width), BF16),
        scratch_shapes=[pltpu.VMEM((seq + CONV_HALO, CONV_COLS), F32)],
        compiler_params=_params("parallel", "parallel"),
        name="conv_silu",
    )(proj3, conv_w)


def _head_column(gb, lane_index):
    lane = lax.broadcasted_iota(jnp.int32, gb.shape, 1)
    return jnp.sum(jnp.where(lane == lane_index, gb, 0.0), axis=1, keepdims=True)


def _chunk_decay(g_col, width):
    r = lax.broadcasted_iota(jnp.int32, (CHUNK, CHUNK), 0)
    c = lax.broadcasted_iota(jnp.int32, (CHUNK, CHUNK), 1)
    lower = jnp.where(c <= r, 1.0, 0.0).astype(BF16)
    gdiff = _split3_dot(lower, jnp.where(r > c, g_col, 0.0))
    gcum = _split3_dot(lower, jnp.broadcast_to(g_col, (CHUNK, width)))
    return gdiff, gcum, r, c


def _split3_dot(a_bf16, b_f32):
    b0 = b_f32.astype(BF16)
    r1 = b_f32 - b0.astype(F32)
    b1 = r1.astype(BF16)
    b2 = (r1 - b1.astype(F32)).astype(BF16)
    return _dot(a_bf16, b0) + _dot(a_bf16, b1) + _dot(a_bf16, b2)


def _lmat_kernel(k_ref, gb_ref, l_ref):
    h = pl.program_id(1)
    gb = gb_ref[...]
    g_all = _head_column(gb, h)
    beta_all = _head_column(gb, C_HEADS + h)
    for ci in range(GDN_ROWS // CHUNK):
        rows = slice(ci * CHUNK, (ci + 1) * CHUNK)
        gdiff, _, r, c = _chunk_decay(g_all[rows], CHUNK)
        k = k_ref[rows, :]
        kk = _dot_nt(k, k)
        l_ref[rows, :] = jnp.where(r > c, beta_all[rows] * kk * jnp.exp(gdiff), 0.0)


def gdn_lmat(k3, gb3):
    batch, seq, _ = k3.shape
    return pl.pallas_call(
        _lmat_kernel,
        grid=(batch, C_HEADS, seq // GDN_ROWS),
        in_specs=[pl.BlockSpec((None, GDN_ROWS, LANES), lambda b, h, i: (b, i, C_HEADS + h)),
                  pl.BlockSpec((None, GDN_ROWS, LANES), lambda b, h, i: (b, i, 0))],
        out_specs=pl.BlockSpec((None, None, GDN_ROWS, CHUNK), lambda b, h, i: (b, h, i, 0)),
        out_shape=jax.ShapeDtypeStruct((batch, C_HEADS, seq, CHUNK), F32),
        compiler_params=_params("parallel", "parallel", "parallel"),
        name="gdn_lmat",
    )(k3, gb3)


def _solve_kernel(l_ref, t_ref):
    col = lax.broadcasted_iota(jnp.int32, (CHUNK, SOLVE_LANES), 0)
    for i in range(CHUNK):
        acc = jnp.where(col == i, 1.0, 0.0)
        for j in range(i):
            acc = acc - l_ref[i, j:j + 1, :] * t_ref[j]
        t_ref[i] = acc


def gdn_solve(lmat_t):
    g = lmat_t.shape[2]
    return pl.pallas_call(
        _solve_kernel,
        grid=(g // SOLVE_LANES,),
        in_specs=[pl.BlockSpec((CHUNK, CHUNK, SOLVE_LANES), lambda i: (0, 0, i))],
        out_specs=pl.BlockSpec((CHUNK, CHUNK, SOLVE_LANES), lambda i: (0, 0, i)),
        out_shape=jax.ShapeDtypeStruct(lmat_t.shape, F32),
        compiler_params=_params("parallel"),
        name="gdn_solve",
    )(lmat_t)


def _delta_kernel(q_ref, k_ref, v_ref, z_ref, gb_ref, t_ref, gain_ref, o_ref, state_scr):
    h = pl.program_id(1)

    @pl.when(pl.program_id(2) == 0)
    def _():
        state_scr[...] = jnp.zeros(state_scr.shape, F32)

    gb = gb_ref[...]
    g_all = _head_column(gb, h)
    beta_all = _head_column(gb, C_HEADS + h)
    gain = gain_ref[...]
    for ci in range(GDN_ROWS // CHUNK):
        rows = slice(ci * CHUNK, (ci + 1) * CHUNK)
        gdiff, gcum, r, c = _chunk_decay(g_all[rows], C_DK)
        beta = beta_all[rows]
        q = q_ref[rows, :]
        k = k_ref[rows, :]
        kf = k.astype(F32)
        decay = jnp.where(c <= r, jnp.exp(gdiff), 0.0)
        eg = jnp.exp(gcum)
        g_last = gcum[CHUNK - 1:CHUNK, :]
        tinv = t_ref[rows, :].astype(BF16)
        u = _dot(tinv, (v_ref[rows, :].astype(F32) * beta).astype(BF16))
        w = _dot(tinv, (kf * beta * eg).astype(BF16))
        a_intra = (_dot_nt(q, k) * decay).astype(BF16)
        state = state_scr[...]
        state_b = state.astype(BF16)
        v_new = u - _dot(w.astype(BF16), state_b)
        v_new_b = v_new.astype(BF16)
        o = _dot((q.astype(F32) * eg).astype(BF16), state_b) + _dot(a_intra, v_new_b)
        k_dec = (kf * jnp.exp(g_last - gcum)).astype(BF16)
        state_scr[...] = state * jnp.exp(g_last) + _dot_tn(k_dec, v_new_b)
        z = z_ref[rows, :].astype(F32)
        o_ref[rows, :] = (_rms_rows(o, gain) * (z * _sigmoid(z))).astype(o_ref.dtype)


def gdn_delta(qk3, v3, proj3, gb3, tinv, out_gain):
    batch, seq, _ = v3.shape
    zcol = 3 * C_WIDTH // LANES
    blk = (None, GDN_ROWS, LANES)
    return pl.pallas_call(
        _delta_kernel,
        grid=(batch, C_HEADS, seq // GDN_ROWS),
        in_specs=[pl.BlockSpec(blk, lambda b, h, i: (b, i, h)),
                  pl.BlockSpec(blk, lambda b, h, i: (b, i, C_HEADS + h)),
                  pl.BlockSpec(blk, lambda b, h, i: (b, i, h)),
                  pl.BlockSpec(blk, lambda b, h, i: (b, i, zcol + h)),
                  pl.BlockSpec(blk, lambda b, h, i: (b, i, 0)),
                  pl.BlockSpec((None, None, GDN_ROWS, CHUNK), lambda b, h, i: (b, h, i, 0)),
                  pl.BlockSpec((1, LANES), lambda b, h, i: (0, 0))],
        out_specs=pl.BlockSpec(blk, lambda b, h, i: (b, i, h)),
        out_shape=jax.ShapeDtypeStruct((batch, seq, C_WIDTH), BF16),
        scratch_shapes=[pltpu.VMEM((C_DK, LANES), F32)],
        compiler_params=_params("parallel", "parallel", "arbitrary"),
        name="gdn_delta",
    )(qk3, qk3, v3, proj3, gb3, tinv, out_gain.reshape(1, LANES))


def rope_tables(seq):
    half = HEAD_DIM // 2
    inv_freq = 1.0 / (ROPE_THETA ** (jnp.arange(0, HEAD_DIM, 2, dtype=F32) / HEAD_DIM))
    ang = jnp.arange(seq, dtype=F32)[:, None] * inv_freq[None, :]
    cos, sin = jnp.cos(ang), jnp.sin(ang)
    cos_t = jnp.tile(cos, (1, LANES // half))
    sin_t = jnp.tile(jnp.concatenate([-sin, sin], axis=1), (1, LANES // HEAD_DIM))
    return cos_t, sin_t


def even_layer(x, batch, seq, layer, attn_norm, w_in, a_q_gain, a_k_gain, b_q_gain, b_k_gain,
               lq1, lk1, lq2, lk2, b_out_gain, w_out, ffn_gain, router):
    lam_init = 0.8 - 0.6 * math.exp(-0.3 * layer)
    proj = norm_matmul(x, attn_norm, w_in.astype(BF16), 512)
    scale = HEAD_DIM ** -0.5
    tile = lambda g, s: jnp.tile(g.astype(F32) * s, A_WIDTH // HEAD_DIM)
    gains = jnp.stack([tile(a_q_gain, scale), tile(a_k_gain, 1.0),
                       tile(b_q_gain, scale), tile(b_k_gain, 1.0)]).reshape(4, 1, 512)
    cos_t, sin_t = rope_tables(seq)
    qk = qk_prep(proj, gains, cos_t, sin_t, seq)
    state = None
    for idx, (window, dilation) in enumerate(A_BRANCHES):
        assert window // dilation == A_BLOCK
        state = dilated_branch(qk, proj, state, dilation, batch, seq, idx == len(A_BRANCHES) - 1)
    oa = state
    ob = diff_attention(qk, proj, lq1.astype(F32), lk1.astype(F32), lq2.astype(F32), lk2.astype(F32),
                        b_out_gain, lam_init, batch, seq)
    return outproj_router(x, oa, 0, ob, 0, w_out.astype(BF16), ffn_gain, *router)


def odd_layer(x, batch, seq, attn_norm, w_in, conv_w, a_log, dt_bias, out_gain, w_out, ffn_gain, router):
    n = batch * seq
    w_pad = jnp.pad(w_in, ((0, 0), (0, LANES - 2 * C_HEADS))).astype(BF16)
    proj = norm_matmul(x, attn_norm, w_pad, 1408)
    gb = gdn_gates(proj, a_log, dt_bias)
    proj3 = proj.reshape(batch, seq, proj.shape[1])
    conv_w = conv_w.astype(F32)
    qk3 = conv_silu(proj3, conv_w, 0, 2 * C_WIDTH, True, C_DK ** -0.5, C_WIDTH)
    v3 = conv_silu(proj3, conv_w, 2 * C_WIDTH, C_WIDTH, False)
    gb3 = gb.reshape(batch, seq, LANES)
    lmat = gdn_lmat(qk3, gb3)
    g = batch * C_HEADS * (seq // CHUNK)
    lmat_t = jnp.transpose(lmat.reshape(g, CHUNK, CHUNK), (1, 2, 0))
    tinv = jnp.transpose(gdn_solve(lmat_t), (2, 0, 1)).reshape(batch, C_HEADS, seq, CHUNK)
    mix = gdn_delta(qk3, v3, proj3, gb3, tinv, out_gain).reshape(n, C_WIDTH)
    return outproj_router(x, mix, 0, mix, 1, w_out.astype(BF16), ffn_gain, *router)


def kernel(x, even_attn_norm, even_w_in, a_q_norm, a_k_norm, b_q_norm, b_k_norm, b_lambda_q1, b_lambda_k1, b_lambda_q2, b_lambda_k2, b_out_norm, even_w_out, odd_attn_norm, odd_w_in, odd_conv_w, odd_a_log, odd_dt_bias, odd_out_norm, odd_w_out, ffn_norm, router_group_w, router_group_b, router_expert_w, router_expert_b, expert_w_gate, expert_w_up, expert_w_down):
    batch, seq, d = x.shape
    depth = ffn_norm.shape[0]
    xf = x.reshape(batch * seq, d).astype(F32)
    for layer in range(depth):
        i = layer // 2
        router = router_tables(router_group_w[layer], router_group_b[layer],
                               router_expert_w[layer], router_expert_b[layer])
        if layer % 2 == 0:
            xf, h, route = even_layer(
                xf, batch, seq, layer, even_attn_norm[i], even_w_in[i], a_q_norm[i], a_k_norm[i],
                b_q_norm[i], b_k_norm[i], b_lambda_q1[i], b_lambda_k1[i], b_lambda_q2[i],
                b_lambda_k2[i], b_out_norm[i], even_w_out[i], ffn_norm[layer], router)
        else:
            xf, h, route = odd_layer(
                xf, batch, seq, odd_attn_norm[i], odd_w_in[i], odd_conv_w[i], odd_a_log[i],
                odd_dt_bias[i], odd_out_norm[i], odd_w_out[i], ffn_norm[layer], router)
        xf = moe_layer(xf, h, route, expert_w_gate[layer].astype(BF16),
                       expert_w_up[layer].astype(BF16), expert_w_down[layer].astype(BF16))
    return xf.reshape(batch, seq, d).astype(x.dtype)
```

```python
import functools
import math

import jax
import jax.numpy as jnp
from jax import lax
from jax.experimental import pallas as pl
from jax.experimental.pallas import tpu as pltpu

F32 = jnp.float32
BF16 = jnp.bfloat16

D_MODEL = 1024
HEAD_DIM = 64
ROPE_THETA = 10000.0
A_HEADS = 8
A_BRANCHES = ((128, 1), (512, 4), (2048, 16))
A_BLOCK = 128
B_HEADS = 4
A_WIDTH = 512
B_WIDTH = 512
EVEN_IN = 3 * A_WIDTH + 3 * B_WIDTH
C_HEADS = 8
C_DK = 128
C_WIDTH = 1024
CONV_W = 4
CHUNK = 64
N_GROUPS = 4
EXPERTS_PER_GROUP = 8
N_EXPERTS = 32
TOP_K = 2
EXPERT_FF = 512
EPS = 1e-6
NEG_INF = -1e30

LANES = 128
VMEM_LIMIT = 48 * 1024 * 1024

ROW_TILE = 512
MOE_TILE = 256
DIFF_TQ = 256
GDN_ROWS = 512
SOLVE_LANES = 128


def _params(*sem):
    return pltpu.CompilerParams(dimension_semantics=sem, vmem_limit_bytes=VMEM_LIMIT)


def _dot(a, b):
    return jnp.dot(a, b, preferred_element_type=F32)


def _dot_nt(a, b):
    return lax.dot_general(a, b, (((1,), (1,)), ((), ())), preferred_element_type=F32)


def _dot_tn(a, b):
    return lax.dot_general(a, b, (((0,), (0,)), ((), ())), preferred_element_type=F32)


def _split_dot(a_f32, b_bf16):
    hi = a_f32.astype(BF16)
    lo = (a_f32 - hi.astype(F32)).astype(BF16)
    return _dot(hi, b_bf16) + _dot(lo, b_bf16)


def _sigmoid(x):
    return 1.0 / (1.0 + jnp.exp(-x))


def _rms_rows(xf, gain):
    ms = jnp.mean(xf * xf, axis=-1, keepdims=True)
    return xf * lax.rsqrt(ms + EPS) * gain


def _norm_matmul_kernel(x_ref, g_ref, w_ref, o_ref, *, col_chunk):
    h = _rms_rows(x_ref[...], g_ref[...]).astype(BF16)
    for c in range(0, o_ref.shape[1], col_chunk):
        o_ref[:, c:c + col_chunk] = _dot(h, w_ref[:, c:c + col_chunk]).astype(o_ref.dtype)


def norm_matmul(x, gain, w_bf16, col_chunk):
    n, d = x.shape
    m = w_bf16.shape[1]
    return pl.pallas_call(
        functools.partial(_norm_matmul_kernel, col_chunk=col_chunk),
        grid=(n // ROW_TILE,),
        in_specs=[pl.BlockSpec((ROW_TILE, d), lambda i: (i, 0)),
                  pl.BlockSpec((1, d), lambda i: (0, 0)),
                  pl.BlockSpec((d, m), lambda i: (0, 0))],
        out_specs=pl.BlockSpec((ROW_TILE, m), lambda i: (i, 0)),
        out_shape=jax.ShapeDtypeStruct((n, m), BF16),
        compiler_params=_params("parallel"),
        name="norm_matmul",
    )(x, gain.reshape(1, d), w_bf16)


def _qk_prep_kernel(p_ref, g_ref, cos_ref, sin_ref, o_ref):
    lane = lax.broadcasted_iota(jnp.int32, (1, LANES), 1)
    first_half = (lane % HEAD_DIM) < (HEAD_DIM // 2)
    r = lax.broadcasted_iota(jnp.int32, (LANES, LANES), 0) // HEAD_DIM
    c = lax.broadcasted_iota(jnp.int32, (LANES, LANES), 1) // HEAD_DIM
    same_head = jnp.where(r == c, 1.0, 0.0).astype(BF16)
    cos = cos_ref[...]
    sin = sin_ref[...]
    gain = g_ref[0]
    for s in range(p_ref.shape[1] // LANES):
        sl = slice(s * LANES, (s + 1) * LANES)
        x = p_ref[:, sl].astype(F32)
        ssq = _split_dot(x * x, same_head)
        y = x * lax.rsqrt(ssq * (1.0 / HEAD_DIM) + EPS) * gain[:, sl]
        partner = jnp.where(first_half,
                            pltpu.roll(y, LANES - HEAD_DIM // 2, axis=1),
                            pltpu.roll(y, HEAD_DIM // 2, axis=1))
        o_ref[:, sl] = (y * cos + partner * sin).astype(o_ref.dtype)


def qk_prep(proj, gains, cos_t, sin_t, seq):
    n = proj.shape[0]
    tiles_per_seq = seq // ROW_TILE
    return pl.pallas_call(
        _qk_prep_kernel,
        grid=(n // ROW_TILE, 4),
        in_specs=[pl.BlockSpec((ROW_TILE, 512), lambda i, g: (i, g + g // 2)),
                  pl.BlockSpec((1, 1, 512), lambda i, g: (g, 0, 0)),
                  pl.BlockSpec((ROW_TILE, LANES), lambda i, g: (i % tiles_per_seq, 0)),
                  pl.BlockSpec((ROW_TILE, LANES), lambda i, g: (i % tiles_per_seq, 0))],
        out_specs=pl.BlockSpec((ROW_TILE, 512), lambda i, g: (i, g)),
        out_shape=jax.ShapeDtypeStruct((n, 2048), BF16),
        compiler_params=_params("parallel", "parallel"),
        name="qk_prep",
    )(proj, gains, cos_t, sin_t)


def _dilated_kernel(*refs, has_prev, is_last):
    q_ref, kp_ref, kc_ref, vp_ref, vc_ref = refs[:5]
    refs = refs[5:]
    if has_prev:
        op_ref, lp_ref = refs[:2]
        refs = refs[2:]
    o_ref = refs[0]
    l_ref = None if is_last else refs[1]

    blk = pl.program_id(2)
    nb = A_BLOCK
    lane = lax.broadcasted_iota(jnp.int32, (1, LANES), 1)
    low = lane < HEAD_DIM
    qi = lax.broadcasted_iota(jnp.int32, (2 * nb, 2 * nb), 0) % nb
    kj = lax.broadcasted_iota(jnp.int32, (2 * nb, 2 * nb), 1)
    valid = (kj >= qi) & (kj <= qi + nb) & ((blk > 0) | (kj >= nb))
    zero = jnp.zeros((), BF16)
    for hp in range(A_WIDTH // LANES):
        sl = slice(hp * LANES, (hp + 1) * LANES)
        q = q_ref[:, sl]
        q2 = jnp.concatenate([jnp.where(low, q, zero), jnp.where(low, zero, q)], axis=0)
        kk = jnp.concatenate([kp_ref[:, sl], kc_ref[:, sl]], axis=0)
        vv = jnp.concatenate([vp_ref[:, sl], vc_ref[:, sl]], axis=0)
        s = jnp.where(valid, _dot_nt(q2, kk), NEG_INF)
        m = jnp.max(s, axis=1, keepdims=True)
        p = jnp.exp(s - m)
        den = jnp.sum(p, axis=1, keepdims=True)
        o2 = _dot(p.astype(BF16), vv) / den
        lse2 = m + jnp.log(den)
        o = jnp.where(low, o2[:nb], o2[nb:])
        lse = jnp.where(low, lse2[:nb], lse2[nb:])
        if has_prev:
            o_prev = op_ref[:, sl]
            l_prev = lp_ref[:, sl]
            top = jnp.maximum(lse, l_prev)
            wa = jnp.exp(lse - top)
            wb = jnp.exp(l_prev - top)
            tot = wa + wb
            o = (wa * o + wb * o_prev) / tot
            lse = top + jnp.log(tot)
        o_ref[:, sl] = o.astype(o_ref.dtype)
        if not is_last:
            l_ref[:, sl] = lse


def dilated_branch(qk, proj, prev, dilation, batch, seq, is_last):
    n = batch * seq
    ln = seq // dilation
    nblk = ln // A_BLOCK
    qk3 = qk.reshape(batch, ln, dilation * 2048)
    pj3 = proj.reshape(batch, ln, dilation * EVEN_IN)
    blk = (None, A_BLOCK, A_WIDTH)

    def cur(col, per):
        return pl.BlockSpec(blk, lambda b, r, c: (b, c, r * per + col))

    def prv(col, per):
        return pl.BlockSpec(blk, lambda b, r, c: (b, jnp.maximum(c - 1, 0), r * per + col))

    plain = pl.BlockSpec(blk, lambda b, r, c: (b, c, r))
    in_specs = [cur(0, 4), prv(1, 4), cur(1, 4), prv(2, 6), cur(2, 6)]
    args = [qk3, qk3, qk3, pj3, pj3]
    if prev is not None:
        in_specs += [plain, plain]
        args += [prev[0].reshape(batch, ln, dilation * A_WIDTH),
                 prev[1].reshape(batch, ln, dilation * A_WIDTH)]
    o_shape = jax.ShapeDtypeStruct((batch, ln, dilation * A_WIDTH), BF16 if is_last else F32)
    if is_last:
        out_specs, out_shape = plain, o_shape
    else:
        out_specs = [plain, plain]
        out_shape = [o_shape, jax.ShapeDtypeStruct((batch, ln, dilation * A_WIDTH), F32)]
    res = pl.pallas_call(
        functools.partial(_dilated_kernel, has_prev=prev is not None, is_last=is_last),
        grid=(batch, dilation, nblk),
        in_specs=in_specs, out_specs=out_specs, out_shape=out_shape,
        compiler_params=_params("parallel", "parallel", "parallel"),
        name=f"dilated_d{dilation}",
    )(*args)
    if is_last:
        return res.reshape(n, A_WIDTH)
    return res[0].reshape(n, A_WIDTH), res[1].reshape(n, A_WIDTH)


def _diff_attn_kernel(q_ref, k_ref, v_ref, lq1_ref, lk1_ref, lq2_ref, lk2_ref, g_ref, o_ref,
                      m_scr, l_scr, acc_scr, *, lam_init):
    h = pl.program_id(1)
    i = pl.program_id(2)
    tq = DIFF_TQ
    lane = lax.broadcasted_iota(jnp.int32, (1, LANES), 1)
    low = lane < HEAD_DIM
    zero = jnp.zeros((), BF16)
    q = q_ref[...]
    q2 = jnp.concatenate([jnp.where(low, q, zero), jnp.where(low, zero, q)], axis=0)
    m_scr[...] = jnp.full(m_scr.shape, NEG_INF, F32)
    l_scr[...] = jnp.zeros(l_scr.shape, F32)
    acc_scr[...] = jnp.zeros(acc_scr.shape, F32)

    def step(j, masked):
        start = pl.multiple_of(j * tq, tq)
        k = k_ref[pl.ds(start, tq), :]
        v = v_ref[pl.ds(start, tq), :]
        s = _dot_nt(q2, k)
        if masked:
            qi = lax.broadcasted_iota(jnp.int32, (2 * tq, tq), 0) % tq
            kj = lax.broadcasted_iota(jnp.int32, (2 * tq, tq), 1)
            s = jnp.where(kj <= qi, s, NEG_INF)
        m_old = m_scr[...]
        m_new = jnp.maximum(m_old, jnp.max(s, axis=1, keepdims=True))
        alpha = jnp.exp(m_old - m_new)
        p = jnp.exp(s - jnp.concatenate([m_new] * (tq // LANES), axis=1))
        l_scr[...] = alpha * l_scr[...] + jnp.sum(p, axis=1, keepdims=True)
        acc_scr[...] = alpha * acc_scr[...] + _dot(p.astype(BF16), v)
        m_scr[...] = m_new

    def body(j, carry):
        step(j, False)
        return carry

    lax.fori_loop(0, i, body, 0)
    step(i, True)

    acc = acc_scr[...] / l_scr[...]
    d1 = jnp.sum(lq1_ref[pl.ds(h, 1), :] * lk1_ref[pl.ds(h, 1), :], axis=1, keepdims=True)
    d2 = jnp.sum(lq2_ref[pl.ds(h, 1), :] * lk2_ref[pl.ds(h, 1), :], axis=1, keepdims=True)
    lam = jnp.exp(d1) - jnp.exp(d2) + lam_init
    o = acc[:tq] - lam * acc[tq:]
    o_ref[...] = (_rms_rows(o, g_ref[...]) * (1.0 - lam_init)).astype(o_ref.dtype)


def diff_attention(qk, proj, lq1, lk1, lq2, lk2, out_gain, lam_init, batch, seq):
    n = batch * seq
    qk3 = qk.reshape(batch, seq, 2048)
    pj3 = proj.reshape(batch, seq, EVEN_IN)
    qcol = 2 * A_WIDTH // LANES
    kcol = (2 * A_WIDTH + B_WIDTH) // LANES
    vcol = (3 * A_WIDTH + 2 * B_WIDTH) // LANES
    small = pl.BlockSpec((B_HEADS, HEAD_DIM), lambda b, h, i: (0, 0))
    out = pl.pallas_call(
        functools.partial(_diff_attn_kernel, lam_init=lam_init),
        grid=(batch, B_HEADS, seq // DIFF_TQ),
        in_specs=[pl.BlockSpec((None, DIFF_TQ, LANES), lambda b, h, i: (b, i, qcol + h)),
                  pl.BlockSpec((None, seq, LANES), lambda b, h, i: (b, 0, kcol + h)),
                  pl.BlockSpec((None, seq, LANES), lambda b, h, i: (b, 0, vcol + h)),
                  small, small, small, small,
                  pl.BlockSpec((1, LANES), lambda b, h, i: (0, 0))],
        out_specs=pl.BlockSpec((None, DIFF_TQ, LANES), lambda b, h, i: (b, i, h)),
        out_shape=jax.ShapeDtypeStruct((batch, seq, B_WIDTH), BF16),
        scratch_shapes=[pltpu.VMEM((2 * DIFF_TQ, LANES), F32),
                        pltpu.VMEM((2 * DIFF_TQ, LANES), F32),
                        pltpu.VMEM((2 * DIFF_TQ, LANES), F32)],
        compiler_params=_params("parallel", "parallel", "parallel"),
        name="diff_attention",
    )(qk3, qk3, pj3, lq1, lk1, lq2, lk2, out_gain.reshape(1, LANES))
    return out.reshape(n, B_WIDTH)


ROUTE_E1, ROUTE_E2, ROUTE_W1, ROUTE_W2 = 0, 1, 2, 3
EXPERT_LANE0 = 8


def _outproj_router_kernel(x_ref, a_ref, b_ref, wa_ref, wb_ref, g_ref, rhi_ref, rlo_ref, rb_ref,
                           xo_ref, h_ref, rt_ref):
    y = x_ref[...] + _dot(a_ref[...], wa_ref[...]) + _dot(b_ref[...], wb_ref[...])
    xo_ref[...] = y
    hn = _rms_rows(y, g_ref[...])
    h_ref[...] = hn
    hi = hn.astype(BF16)
    lo = (hn - hi.astype(F32)).astype(BF16)
    logits = (_dot(hi, rhi_ref[...]) + _dot(lo, rhi_ref[...]) + _dot(hi, rlo_ref[...])) + rb_ref[...]

    lane = lax.broadcasted_iota(jnp.int32, logits.shape, 1)
    big = jnp.int32(LANES)
    is_group = lane < N_GROUPS
    gl = jnp.where(is_group, logits, NEG_INF)
    gmax = jnp.max(gl, axis=1, keepdims=True)
    grp = jnp.min(jnp.where(is_group & (gl == gmax), lane, big), axis=1, keepdims=True)
    g_w = 1.0 / jnp.sum(jnp.where(is_group, jnp.exp(gl - gmax), 0.0), axis=1, keepdims=True)
    first = EXPERT_LANE0 + grp * EXPERTS_PER_GROUP
    in_group = (lane >= first) & (lane < first + EXPERTS_PER_GROUP)
    el = jnp.where(in_group, logits, NEG_INF)
    v1 = jnp.max(el, axis=1, keepdims=True)
    i1 = jnp.min(jnp.where(in_group & (el == v1), lane, big), axis=1, keepdims=True)
    rest = in_group & (lane != i1)
    el2 = jnp.where(rest, logits, NEG_INF)
    v2 = jnp.max(el2, axis=1, keepdims=True)
    i2 = jnp.min(jnp.where(rest & (el2 == v2), lane, big), axis=1, keepdims=True)
    e2w = jnp.exp(v2 - v1)
    w1 = g_w / (1.0 + e2w)
    w2 = g_w * e2w / (1.0 + e2w)
    e1 = (i1 - EXPERT_LANE0).astype(F32)
    e2 = (i2 - EXPERT_LANE0).astype(F32)
    rt_ref[...] = jnp.where(lane == ROUTE_E1, e1,
                            jnp.where(lane == ROUTE_E2, e2,
                                      jnp.where(lane == ROUTE_W1, w1,
                                                jnp.where(lane == ROUTE_W2, w2, 0.0))))


def outproj_router(x, mix_a, blk_a, mix_b, blk_b, w_out_bf16, ffn_gain, r_hi, r_lo, r_bias):
    n, d = x.shape
    half = w_out_bf16.shape[0] // 2
    row = lambda i: (i, 0)
    const = lambda i: (0, 0)
    return pl.pallas_call(
        _outproj_router_kernel,
        grid=(n // ROW_TILE,),
        in_specs=[pl.BlockSpec((ROW_TILE, d), row),
                  pl.BlockSpec((ROW_TILE, half), lambda i: (i, blk_a)),
                  pl.BlockSpec((ROW_TILE, half), lambda i: (i, blk_b)),
                  pl.BlockSpec((half, d), lambda i: (0, 0)),
                  pl.BlockSpec((half, d), lambda i: (1, 0)),
                  pl.BlockSpec((1, d), const),
                  pl.BlockSpec((d, LANES), const),
                  pl.BlockSpec((d, LANES), const),
                  pl.BlockSpec((1, LANES), const)],
        out_specs=[pl.BlockSpec((ROW_TILE, d), row),
                   pl.BlockSpec((ROW_TILE, d), row),
                   pl.BlockSpec((ROW_TILE, LANES), row)],
        out_shape=[jax.ShapeDtypeStruct((n, d), F32),
                   jax.ShapeDtypeStruct((n, d), F32),
                   jax.ShapeDtypeStruct((n, LANES), F32)],
        compiler_params=_params("parallel"),
        name="outproj_router",
    )(x, mix_a, mix_b, w_out_bf16, w_out_bf16, ffn_gain.reshape(1, d), r_hi, r_lo, r_bias)


def _rank_kernel(rt_ref, tri_ref, rk_ref, cnt_ref, carry_scr):
    @pl.when(pl.program_id(0) == 0)
    def _():
        carry_scr[...] = jnp.zeros(carry_scr.shape, F32)

    rt = rt_ref[...]
    lane = lax.broadcasted_iota(jnp.int32, rt.shape, 1)
    e1 = rt[:, ROUTE_E1:ROUTE_E1 + 1].astype(jnp.int32)
    e2 = rt[:, ROUTE_E2:ROUTE_E2 + 1].astype(jnp.int32)
    oh1 = lane == e1
    oh2 = lane == e2
    both = jnp.where(oh1 | oh2, 1.0, 0.0)
    before = _dot(tri_ref[...], both.astype(BF16)) + carry_scr[0:1, :]
    r1 = jnp.sum(jnp.where(oh1, before, 0.0), axis=1, keepdims=True)
    r2 = jnp.sum(jnp.where(oh2, before, 0.0), axis=1, keepdims=True)
    rk_ref[...] = jnp.where(lane == 0, r1, jnp.where(lane == 1, r2, 0.0))
    total = carry_scr[0:1, :] + jnp.sum(both, axis=0, keepdims=True)
    carry_scr[...] = jnp.broadcast_to(total, carry_scr.shape)
    cnt_ref[...] = jnp.broadcast_to(total, cnt_ref.shape)


def rank_slots(route):
    n = route.shape[0]
    tri = jnp.tril(jnp.ones((ROW_TILE, ROW_TILE), BF16), k=-1)
    return pl.pallas_call(
        _rank_kernel,
        grid=(n // ROW_TILE,),
        in_specs=[pl.BlockSpec((ROW_TILE, LANES), lambda i: (i, 0)),
                  pl.BlockSpec((ROW_TILE, ROW_TILE), lambda i: (0, 0))],
        out_specs=[pl.BlockSpec((ROW_TILE, LANES), lambda i: (i, 0)),
                   pl.BlockSpec((8, LANES), lambda i: (0, 0))],
        out_shape=[jax.ShapeDtypeStruct((n, LANES), F32),
                   jax.ShapeDtypeStruct((8, LANES), F32)],
        scratch_shapes=[pltpu.VMEM((8, LANES), F32)],
        compiler_params=_params("arbitrary"),
        name="moe_rank",
    )(route, tri)


DISPATCH_TOKENS = 512
DMA_UNROLL = 8


def _dispatch_kernel(dest_ref, h_ref, zero_ref, xs_ref, sem):
    del zero_ref
    base = pl.program_id(0) * DISPATCH_TOKENS

    def row_copy(i, k):
        return pltpu.make_async_copy(h_ref.at[pl.ds(i, 1)],
                                     xs_ref.at[pl.ds(dest_ref[TOP_K * (base + i) + k], 1)], sem)

    def issue(i, carry):
        for k in range(TOP_K):
            row_copy(i, k).start()
        return carry

    lax.fori_loop(0, DISPATCH_TOKENS, issue, 0, unroll=DMA_UNROLL)

    def drain(i, carry):
        for k in range(TOP_K):
            row_copy(i, k).wait()
        return carry

    lax.fori_loop(0, DISPATCH_TOKENS, drain, 0, unroll=DMA_UNROLL)


def dispatch_rows(h, dest, n_rows):
    n, d = h.shape
    return pl.pallas_call(
        _dispatch_kernel,
        grid_spec=pltpu.PrefetchScalarGridSpec(
            num_scalar_prefetch=1,
            grid=(n // DISPATCH_TOKENS,),
            in_specs=[pl.BlockSpec((DISPATCH_TOKENS, d), lambda i, dst: (i, 0)),
                      pl.BlockSpec(memory_space=pl.ANY)],
            out_specs=pl.BlockSpec(memory_space=pl.ANY),
            scratch_shapes=[pltpu.SemaphoreType.DMA(())]),
        out_shape=jax.ShapeDtypeStruct((n_rows, d), F32),
        input_output_aliases={2: 0},
        compiler_params=_params("arbitrary"),
        name="moe_dispatch",
    )(dest, h, jnp.zeros((n_rows, d), F32))


def _expert_kernel(be_ref, nb_ref, x_ref, wg_ref, wu_ref, wd_ref, y_ref):
    del be_ref

    @pl.when(pl.program_id(0) < nb_ref[0])
    def _():
        xb = x_ref[...].astype(BF16)
        gate = _dot(xb, wg_ref[...])
        up = _dot(xb, wu_ref[...])
        hidden = (gate * _sigmoid(gate) * up).astype(BF16)
        y_ref[...] = _dot(hidden, wd_ref[...])

    @pl.when(pl.program_id(0) >= nb_ref[0])
    def _():
        y_ref[...] = jnp.zeros(y_ref.shape, F32)


def expert_ffn(xs, block_expert, n_used, w_gate, w_up, w_down):
    p, d = xs.shape
    ff = w_gate.shape[2]
    return pl.pallas_call(
        _expert_kernel,
        grid_spec=pltpu.PrefetchScalarGridSpec(
            num_scalar_prefetch=2,
            grid=(p // MOE_TILE,),
            in_specs=[pl.BlockSpec((MOE_TILE, d), lambda i, be, nb: (i, 0)),
                      pl.BlockSpec((None, d, ff), lambda i, be, nb: (be[i], 0, 0)),
                      pl.BlockSpec((None, d, ff), lambda i, be, nb: (be[i], 0, 0)),
                      pl.BlockSpec((None, ff, d), lambda i, be, nb: (be[i], 0, 0))],
            out_specs=pl.BlockSpec((MOE_TILE, d), lambda i, be, nb: (i, 0))),
        out_shape=jax.ShapeDtypeStruct((p, d), F32),
        compiler_params=_params("arbitrary"),
        name="moe_experts",
    )(block_expert, n_used, xs, w_gate, w_up, w_down)


COMBINE_TOKENS = 512


def _combine_kernel(dest_ref, x_ref, rt_ref, y_ref, o_ref, buf0, buf1, sem):
    base = pl.program_id(0) * COMBINE_TOKENS
    bufs = (buf0, buf1)

    def row_copy(i, k):
        return pltpu.make_async_copy(y_ref.at[pl.ds(dest_ref[TOP_K * (base + i) + k], 1)],
                                     bufs[k].at[pl.ds(i, 1)], sem)

    def issue(i, carry):
        for k in range(TOP_K):
            row_copy(i, k).start()
        return carry

    lax.fori_loop(0, COMBINE_TOKENS, issue, 0, unroll=DMA_UNROLL)

    def drain(i, carry):
        for k in range(TOP_K):
            row_copy(i, k).wait()
        return carry

    lax.fori_loop(0, COMBINE_TOKENS, drain, 0, unroll=DMA_UNROLL)
    rt = rt_ref[...]
    w1 = rt[:, ROUTE_W1:ROUTE_W1 + 1]
    w2 = rt[:, ROUTE_W2:ROUTE_W2 + 1]
    o_ref[...] = x_ref[...] + (w1 * buf0[...] + w2 * buf1[...])


def combine_rows(x, route, y, dest):
    n, d = x.shape
    return pl.pallas_call(
        _combine_kernel,
        grid_spec=pltpu.PrefetchScalarGridSpec(
            num_scalar_prefetch=1,
            grid=(n // COMBINE_TOKENS,),
            in_specs=[pl.BlockSpec((COMBINE_TOKENS, d), lambda i, dst: (i, 0)),
                      pl.BlockSpec((COMBINE_TOKENS, LANES), lambda i, dst: (i, 0)),
                      pl.BlockSpec(memory_space=pl.ANY)],
            out_specs=pl.BlockSpec((COMBINE_TOKENS, d), lambda i, dst: (i, 0)),
            scratch_shapes=[pltpu.VMEM((COMBINE_TOKENS, d), F32),
                            pltpu.VMEM((COMBINE_TOKENS, d), F32),
                            pltpu.SemaphoreType.DMA(())]),
        out_shape=jax.ShapeDtypeStruct((n, d), F32),
        compiler_params=_params("arbitrary"),
        name="moe_combine",
    )(dest, x, route, y)


def moe_layer(x, h, route, w_gate, w_up, w_down):
    n = x.shape[0]
    ranks, counts = rank_slots(route)
    counts = counts[0, :N_EXPERTS].astype(jnp.int32)
    padded = (counts + MOE_TILE - 1) // MOE_TILE * MOE_TILE
    pad_end = jnp.cumsum(padded)
    pad_start = pad_end - padded
    n_blocks = -(-(n * TOP_K + N_EXPERTS * (MOE_TILE - 1)) // MOE_TILE)
    eids = route[:, ROUTE_E1:ROUTE_E2 + 1].astype(jnp.int32)
    slot0 = jnp.sum(jnp.where(eids[..., None] == jnp.arange(N_EXPERTS), pad_start, 0), axis=-1)
    dest = (slot0 + ranks[:, :TOP_K].astype(jnp.int32)).reshape(-1)
    block_start = jnp.arange(n_blocks, dtype=jnp.int32) * MOE_TILE
    block_expert = jnp.minimum(
        jnp.sum(block_start[:, None] >= pad_end[None, :], axis=1), N_EXPERTS - 1).astype(jnp.int32)
    n_used = (pad_end[-1:] // MOE_TILE).astype(jnp.int32)
    xs = dispatch_rows(h, dest, n_blocks * MOE_TILE)
    y = expert_ffn(xs, block_expert, n_used, w_gate, w_up, w_down)
    return combine_rows(x, route, y, dest)


def router_tables(w_group, b_group, w_erouter, b_erouter):
    d = w_group.shape[0]
    w_exp = jnp.transpose(w_erouter, (1, 0, 2)).reshape(d, N_EXPERTS)
    w = jnp.zeros((d, LANES), F32)
    w = w.at[:, :N_GROUPS].set(w_group.astype(F32))
    w = w.at[:, EXPERT_LANE0:EXPERT_LANE0 + N_EXPERTS].set(w_exp.astype(F32))
    bias = jnp.zeros((1, LANES), F32)
    bias = bias.at[0, :N_GROUPS].set(b_group.astype(F32))
    bias = bias.at[0, EXPERT_LANE0:EXPERT_LANE0 + N_EXPERTS].set(b_erouter.reshape(-1).astype(F32))
    hi = w.astype(BF16)
    lo = (w - hi.astype(F32)).astype(BF16)
    return hi, lo, bias


def _gates_kernel(ab_ref, a_ref, dt_ref, tri_ref, o_ref):
    x = ab_ref[...].astype(F32)
    lane = lax.broadcasted_iota(jnp.int32, x.shape, 1)
    z = x + dt_ref[...]
    softplus = jnp.maximum(z, 0.0) + jnp.log(1.0 + jnp.exp(-jnp.abs(z)))
    g = -jnp.exp(a_ref[...]) * softplus
    gcum = _split3_dot(tri_ref[...], g)
    o_ref[...] = jnp.where(lane < C_HEADS, gcum, _sigmoid(x))


def gdn_gates(proj, a_log, dt_bias):
    n = proj.shape[0]
    pad = lambda v: jnp.zeros((1, LANES), F32).at[0, :C_HEADS].set(v.astype(F32))
    r = jnp.arange(ROW_TILE)
    tri = ((r[:, None] >= r[None, :]) & (r[:, None] // CHUNK == r[None, :] // CHUNK)).astype(BF16)
    return pl.pallas_call(
        _gates_kernel,
        grid=(n // ROW_TILE,),
        in_specs=[pl.BlockSpec((ROW_TILE, LANES), lambda i: (i, 4 * C_WIDTH // LANES)),
                  pl.BlockSpec((1, LANES), lambda i: (0, 0)),
                  pl.BlockSpec((1, LANES), lambda i: (0, 0)),
                  pl.BlockSpec((ROW_TILE, ROW_TILE), lambda i: (0, 0))],
        out_specs=pl.BlockSpec((ROW_TILE, LANES), lambda i: (i, 0)),
        out_shape=jax.ShapeDtypeStruct((n, LANES), F32),
        compiler_params=_params("parallel"),
        name="gdn_gates",
    )(proj, pad(a_log), pad(dt_bias), tri)


CONV_COLS = 256
CONV_ROWS = 512
CONV_HALO = 8


def _conv_kernel(x_ref, w_ref, o_ref, pad_scr, *, normalize, scale, scale_blocks):
    seq = x_ref.shape[0]
    out_scale = jnp.where(pl.program_id(1) < scale_blocks, scale, 1.0)
    pad_scr[0:CONV_HALO, :] = jnp.zeros((CONV_HALO, CONV_COLS), F32)
    pad_scr[CONV_HALO:, :] = x_ref[...].astype(F32)
    w = w_ref[...]
    for r0 in range(0, seq, CONV_ROWS):
        acc = None
        for j in range(CONV_W):
            start = CONV_HALO + r0 - (CONV_W - 1) + j
            term = pad_scr[start:start + CONV_ROWS, :] * w[j:j + 1, :]
            acc = term if acc is None else acc + term
        y = acc * _sigmoid(acc)
        if normalize:
            for s in range(CONV_COLS // LANES):
                ys = y[:, s * LANES:(s + 1) * LANES]
                ssq = jnp.sum(ys * ys, axis=1, keepdims=True)
                o_ref[r0:r0 + CONV_ROWS, s * LANES:(s + 1) * LANES] = (
                    ys * lax.rsqrt(ssq + EPS) * out_scale).astype(o_ref.dtype)
        else:
            o_ref[r0:r0 + CONV_ROWS, :] = y.astype(o_ref.dtype)


def conv_silu(proj3, conv_w, col0, width, normalize, scale=1.0, scale_width=0):
    batch, seq, _ = proj3.shape
    cb0 = col0 // CONV_COLS
    return pl.pallas_call(
        functools.partial(_conv_kernel, normalize=normalize, scale=scale,
                          scale_blocks=scale_width // CONV_COLS),
        grid=(batch, width // CONV_COLS),
        in_specs=[pl.BlockSpec((None, seq, CONV_COLS), lambda b, c: (b, 0, cb0 + c)),
                  pl.BlockSpec((CONV_W, CONV_COLS), lambda b, c: (0, cb0 + c))],
        out_specs=pl.BlockSpec((None, seq, CONV_COLS), lambda b, c: (b, 0, c)),
        out_shape=jax.ShapeDtypeStruct((batch, seq, width), BF16),
        scratch_shapes=[pltpu.VMEM((seq + CONV_HALO, CONV_COLS), F32)],
        compiler_params=_params("parallel", "parallel"),
        name="conv_silu",
    )(proj3, conv_w)


def _head_column(gb, lane_index):
    lane = lax.broadcasted_iota(jnp.int32, gb.shape, 1)
    return jnp.sum(jnp.where(lane == lane_index, gb, 0.0), axis=1, keepdims=True)


def _chunk_decay(gcum_col):
    r = lax.broadcasted_iota(jnp.int32, (CHUNK, CHUNK), 0)
    c = lax.broadcasted_iota(jnp.int32, (CHUNK, CHUNK), 1)
    gcum = jnp.broadcast_to(gcum_col, (CHUNK, LANES))
    gcum_rows = jnp.transpose(gcum)[:CHUNK, :]
    gdiff = jnp.where(c <= r, gcum[:, :CHUNK] - gcum_rows, 0.0)
    return gcum, gdiff, r, c


def _split3_dot(a_bf16, b_f32):
    b0 = b_f32.astype(BF16)
    r1 = b_f32 - b0.astype(F32)
    b1 = r1.astype(BF16)
    b2 = (r1 - b1.astype(F32)).astype(BF16)
    return _dot(a_bf16, b0) + _dot(a_bf16, b1) + _dot(a_bf16, b2)


def _lmat_kernel(k_ref, gb_ref, l_ref):
    h = pl.program_id(1)
    gb = gb_ref[...]
    gcum_all = _head_column(gb, h)
    beta_all = _head_column(gb, C_HEADS + h)
    for ci in range(GDN_ROWS // CHUNK):
        rows = slice(ci * CHUNK, (ci + 1) * CHUNK)
        _, gdiff, r, c = _chunk_decay(gcum_all[rows])
        k = k_ref[rows, :]
        kk = _dot_nt(k, k)
        l_ref[rows, :] = jnp.where(r > c, beta_all[rows] * kk * jnp.exp(gdiff), 0.0)


def gdn_lmat(k3, gb3):
    batch, seq, _ = k3.shape
    return pl.pallas_call(
        _lmat_kernel,
        grid=(batch, C_HEADS, seq // GDN_ROWS),
        in_specs=[pl.BlockSpec((None, GDN_ROWS, LANES), lambda b, h, i: (b, i, C_HEADS + h)),
                  pl.BlockSpec((None, GDN_ROWS, LANES), lambda b, h, i: (b, i, 0))],
        out_specs=pl.BlockSpec((None, None, GDN_ROWS, CHUNK), lambda b, h, i: (b, h, i, 0)),
        out_shape=jax.ShapeDtypeStruct((batch, C_HEADS, seq, CHUNK), F32),
        compiler_params=_params("parallel", "parallel", "parallel"),
        name="gdn_lmat",
    )(k3, gb3)


def _solve_kernel(l_ref, t_ref):
    col = lax.broadcasted_iota(jnp.int32, (CHUNK, SOLVE_LANES), 0)
    for i in range(CHUNK):
        acc = jnp.where(col == i, 1.0, 0.0)
        for j in range(i):
            acc = acc - l_ref[i, j:j + 1, :] * t_ref[j]
        t_ref[i] = acc


def gdn_solve(lmat_t):
    g = lmat_t.shape[2]
    return pl.pallas_call(
        _solve_kernel,
        grid=(g // SOLVE_LANES,),
        in_specs=[pl.BlockSpec((CHUNK, CHUNK, SOLVE_LANES), lambda i: (0, 0, i))],
        out_specs=pl.BlockSpec((CHUNK, CHUNK, SOLVE_LANES), lambda i: (0, 0, i)),
        out_shape=jax.ShapeDtypeStruct(lmat_t.shape, F32),
        compiler_params=_params("parallel"),
        name="gdn_solve",
    )(lmat_t)


DELTA_HEADS = 4
DELTA_ROWS = 256


def _delta_kernel(q_ref, k_ref, v_ref, z_ref, gb_ref, t_ref, gain_ref, o_ref, *state_scrs):
    hg = pl.program_id(1)

    @pl.when(pl.program_id(2) == 0)
    def _():
        for scr in state_scrs:
            scr[...] = jnp.zeros(scr.shape, F32)

    states = [scr[...] for scr in state_scrs]
    gb = gb_ref[...]
    gain = gain_ref[...]
    for ci in range(DELTA_ROWS // CHUNK):
        rows = slice(ci * CHUNK, (ci + 1) * CHUNK)
        gb_c = gb[rows]
        for hh in range(DELTA_HEADS):
            cols = slice(hh * LANES, (hh + 1) * LANES)
            head = hg * DELTA_HEADS + hh
            gcum, gdiff, r, c = _chunk_decay(_head_column(gb_c, head))
            beta = _head_column(gb_c, C_HEADS + head)
            q = q_ref[rows, cols]
            k = k_ref[rows, cols]
            kf = k.astype(F32)
            decay = jnp.where(c <= r, jnp.exp(gdiff), 0.0)
            eg = jnp.exp(gcum)
            g_last = gcum[CHUNK - 1:CHUNK, :]
            tinv = t_ref[hh, rows, :].astype(BF16)
            rhs = jnp.concatenate([(v_ref[rows, cols].astype(F32) * beta).astype(BF16),
                                   (kf * beta * eg).astype(BF16)], axis=1)
            uw = _dot(tinv, rhs).astype(BF16)
            a_intra = (_dot_nt(q, k) * decay).astype(BF16)
            a_uw = _dot(a_intra, uw)
            k_dec = (kf * jnp.exp(g_last - gcum)).astype(BF16)
            kd_uw = _dot_tn(k_dec, uw)
            q_eff = q.astype(F32) * eg - a_uw[:, LANES:]
            lhs = jnp.concatenate([q_eff.astype(BF16), kd_uw[:, LANES:].astype(BF16)], axis=0)
            state = states[hh]
            prod = _dot(lhs, state.astype(BF16))
            o = prod[:CHUNK] + a_uw[:, :LANES]
            states[hh] = state * jnp.exp(g_last) - prod[CHUNK:] + kd_uw[:, :LANES]
            z = z_ref[rows, cols].astype(F32)
            o_ref[rows, cols] = (_rms_rows(o, gain) * (z * _sigmoid(z))).astype(o_ref.dtype)
    for scr, state in zip(state_scrs, states):
        scr[...] = state


def gdn_delta(qk3, v3, proj3, gb3, tinv, out_gain):
    batch, seq, _ = v3.shape
    width = DELTA_HEADS * LANES
    groups = C_HEADS // DELTA_HEADS
    zcol = 3 * C_WIDTH // width
    blk = (None, DELTA_ROWS, width)
    return pl.pallas_call(
        _delta_kernel,
        grid=(batch, groups, seq // DELTA_ROWS),
        in_specs=[pl.BlockSpec(blk, lambda b, h, i: (b, i, h)),
                  pl.BlockSpec(blk, lambda b, h, i: (b, i, groups + h)),
                  pl.BlockSpec(blk, lambda b, h, i: (b, i, h)),
                  pl.BlockSpec(blk, lambda b, h, i: (b, i, zcol + h)),
                  pl.BlockSpec((None, DELTA_ROWS, LANES), lambda b, h, i: (b, i, 0)),
                  pl.BlockSpec((None, DELTA_HEADS, DELTA_ROWS, CHUNK), lambda b, h, i: (b, h, i, 0)),
                  pl.BlockSpec((1, LANES), lambda b, h, i: (0, 0))],
        out_specs=pl.BlockSpec(blk, lambda b, h, i: (b, i, h)),
        out_shape=jax.ShapeDtypeStruct((batch, seq, C_WIDTH), BF16),
        scratch_shapes=[pltpu.VMEM((C_DK, LANES), F32) for _ in range(DELTA_HEADS)],
        compiler_params=_params("parallel", "parallel", "arbitrary"),
        name="gdn_delta",
    )(qk3, qk3, v3, proj3, gb3, tinv, out_gain.reshape(1, LANES))


def rope_tables(seq):
    half = HEAD_DIM // 2
    inv_freq = 1.0 / (ROPE_THETA ** (jnp.arange(0, HEAD_DIM, 2, dtype=F32) / HEAD_DIM))
    ang = jnp.arange(seq, dtype=F32)[:, None] * inv_freq[None, :]
    cos, sin = jnp.cos(ang), jnp.sin(ang)
    cos_t = jnp.tile(cos, (1, LANES // half))
    sin_t = jnp.tile(jnp.concatenate([-sin, sin], axis=1), (1, LANES // HEAD_DIM))
    return cos_t, sin_t


def even_layer(x, batch, seq, layer, attn_norm, w_in, a_q_gain, a_k_gain, b_q_gain, b_k_gain,
               lq1, lk1, lq2, lk2, b_out_gain, w_out, ffn_gain, router):
    lam_init = 0.8 - 0.6 * math.exp(-0.3 * layer)
    proj = norm_matmul(x, attn_norm, w_in.astype(BF16), 512)
    scale = HEAD_DIM ** -0.5
    tile = lambda g, s: jnp.tile(g.astype(F32) * s, A_WIDTH // HEAD_DIM)
    gains = jnp.stack([tile(a_q_gain, scale), tile(a_k_gain, 1.0),
                       tile(b_q_gain, scale), tile(b_k_gain, 1.0)]).reshape(4, 1, 512)
    cos_t, sin_t = rope_tables(seq)
    qk = qk_prep(proj, gains, cos_t, sin_t, seq)
    state = None
    for idx, (window, dilation) in enumerate(A_BRANCHES):
        assert window // dilation == A_BLOCK
        state = dilated_branch(qk, proj, state, dilation, batch, seq, idx == len(A_BRANCHES) - 1)
    oa = state
    ob = diff_attention(qk, proj, lq1.astype(F32), lk1.astype(F32), lq2.astype(F32), lk2.astype(F32),
                        b_out_gain, lam_init, batch, seq)
    return outproj_router(x, oa, 0, ob, 0, w_out.astype(BF16), ffn_gain, *router)


def odd_layer(x, batch, seq, attn_norm, w_in, conv_w, a_log, dt_bias, out_gain, w_out, ffn_gain, router):
    n = batch * seq
    w_pad = jnp.pad(w_in, ((0, 0), (0, LANES - 2 * C_HEADS))).astype(BF16)
    proj = norm_matmul(x, attn_norm, w_pad, 1408)
    gb = gdn_gates(proj, a_log, dt_bias)
    proj3 = proj.reshape(batch, seq, proj.shape[1])
    conv_w = conv_w.astype(F32)
    qk3 = conv_silu(proj3, conv_w, 0, 2 * C_WIDTH, True, C_DK ** -0.5, C_WIDTH)
    v3 = conv_silu(proj3, conv_w, 2 * C_WIDTH, C_WIDTH, False)
    gb3 = gb.reshape(batch, seq, LANES)
    lmat = gdn_lmat(qk3, gb3)
    g = batch * C_HEADS * (seq // CHUNK)
    lmat_t = jnp.transpose(lmat.reshape(g, CHUNK, CHUNK), (1, 2, 0))
    tinv = jnp.transpose(gdn_solve(lmat_t), (2, 0, 1)).reshape(batch, C_HEADS, seq, CHUNK)
    mix = gdn_delta(qk3, v3, proj3, gb3, tinv, out_gain).reshape(n, C_WIDTH)
    return outproj_router(x, mix, 0, mix, 1, w_out.astype(BF16), ffn_gain, *router)


def kernel(x, even_attn_norm, even_w_in, a_q_norm, a_k_norm, b_q_norm, b_k_norm, b_lambda_q1, b_lambda_k1, b_lambda_q2, b_lambda_k2, b_out_norm, even_w_out, odd_attn_norm, odd_w_in, odd_conv_w, odd_a_log, odd_dt_bias, odd_out_norm, odd_w_out, ffn_norm, router_group_w, router_group_b, router_expert_w, router_expert_b, expert_w_gate, expert_w_up, expert_w_down):
    batch, seq, d = x.shape
    depth = ffn_norm.shape[0]
    xf = x.reshape(batch * seq, d).astype(F32)
    for layer in range(depth):
        i = layer // 2
        router = router_tables(router_group_w[layer], router_group_b[layer],
                               router_expert_w[layer], router_expert_b[layer])
        if layer % 2 == 0:
            xf, h, route = even_layer(
                xf, batch, seq, layer, even_attn_norm[i], even_w_in[i], a_q_norm[i], a_k_norm[i],
                b_q_norm[i], b_k_norm[i], b_lambda_q1[i], b_lambda_k1[i], b_lambda_q2[i],
                b_lambda_k2[i], b_out_norm[i], even_w_out[i], ffn_norm[layer], router)
        else:
            xf, h, route = odd_layer(
                xf, batch, seq, odd_attn_norm[i], odd_w_in[i], odd_conv_w[i], odd_a_log[i],
                odd_dt_bias[i], odd_out_norm[i], odd_w_out[i], ffn_norm[layer], router)
        xf = moe_layer(xf, h, route, expert_w_gate[layer].astype(BF16),
                       expert_w_up[layer].astype(BF16), expert_w_down[layer].astype(BF16))
    return xf.reshape(batch, seq, d).astype(x.dtype)
```

```python
import functools
import math

import jax
import jax.numpy as jnp
from jax import lax
from jax.experimental import pallas as pl
from jax.experimental.pallas import tpu as pltpu

F32 = jnp.float32
BF16 = jnp.bfloat16

D_MODEL = 1024
HEAD_DIM = 64
ROPE_THETA = 10000.0
A_HEADS = 8
A_BRANCHES = ((128, 1), (512, 4), (2048, 16))
A_BLOCK = 128
B_HEADS = 4
A_WIDTH = 512
B_WIDTH = 512
EVEN_IN = 3 * A_WIDTH + 3 * B_WIDTH
C_HEADS = 8
C_DK = 128
C_WIDTH = 1024
CONV_W = 4
CHUNK = 64
N_GROUPS = 4
EXPERTS_PER_GROUP = 8
N_EXPERTS = 32
TOP_K = 2
EXPERT_FF = 512
EPS = 1e-6
NEG_INF = -1e30

LANES = 128
VMEM_LIMIT = 48 * 1024 * 1024

ROW_TILE = 512
MOE_TILE = 256
DIFF_TQ = 512
GDN_ROWS = 512
SOLVE_LANES = 128


def _params(*sem):
    return pltpu.CompilerParams(dimension_semantics=sem, vmem_limit_bytes=VMEM_LIMIT)


def _dot(a, b):
    return jnp.dot(a, b, preferred_element_type=F32)


def _dot_nt(a, b):
    return lax.dot_general(a, b, (((1,), (1,)), ((), ())), preferred_element_type=F32)


def _dot_tn(a, b):
    return lax.dot_general(a, b, (((0,), (0,)), ((), ())), preferred_element_type=F32)


def _split_dot(a_f32, b_bf16):
    hi = a_f32.astype(BF16)
    lo = (a_f32 - hi.astype(F32)).astype(BF16)
    return _dot(hi, b_bf16) + _dot(lo, b_bf16)


def _sigmoid(x):
    return 1.0 / (1.0 + jnp.exp(-x))


def _rms_rows(xf, gain):
    ms = jnp.mean(xf * xf, axis=-1, keepdims=True)
    return xf * lax.rsqrt(ms + EPS) * gain


def _norm_matmul_kernel(x_ref, g_ref, w_ref, o_ref, *, col_chunk):
    h = _rms_rows(x_ref[...], g_ref[...]).astype(BF16)
    for c in range(0, o_ref.shape[1], col_chunk):
        o_ref[:, c:c + col_chunk] = _dot(h, w_ref[:, c:c + col_chunk]).astype(o_ref.dtype)


def norm_matmul(x, gain, w_bf16, col_chunk):
    n, d = x.shape
    m = w_bf16.shape[1]
    return pl.pallas_call(
        functools.partial(_norm_matmul_kernel, col_chunk=col_chunk),
        grid=(n // ROW_TILE,),
        in_specs=[pl.BlockSpec((ROW_TILE, d), lambda i: (i, 0)),
                  pl.BlockSpec((1, d), lambda i: (0, 0)),
                  pl.BlockSpec((d, m), lambda i: (0, 0))],
        out_specs=pl.BlockSpec((ROW_TILE, m), lambda i: (i, 0)),
        out_shape=jax.ShapeDtypeStruct((n, m), BF16),
        compiler_params=_params("parallel"),
        name="norm_matmul",
    )(x, gain.reshape(1, d), w_bf16)


def _qk_prep_kernel(p_ref, g_ref, cos_ref, sin_ref, o_ref):
    lane = lax.broadcasted_iota(jnp.int32, (1, LANES), 1)
    first_half = (lane % HEAD_DIM) < (HEAD_DIM // 2)
    r = lax.broadcasted_iota(jnp.int32, (LANES, LANES), 0) // HEAD_DIM
    c = lax.broadcasted_iota(jnp.int32, (LANES, LANES), 1) // HEAD_DIM
    same_head = jnp.where(r == c, 1.0, 0.0).astype(BF16)
    cos = cos_ref[...]
    sin = sin_ref[...]
    gain = g_ref[0]
    for s in range(p_ref.shape[1] // LANES):
        sl = slice(s * LANES, (s + 1) * LANES)
        x = p_ref[:, sl].astype(F32)
        ssq = _split_dot(x * x, same_head)
        y = x * lax.rsqrt(ssq * (1.0 / HEAD_DIM) + EPS) * gain[:, sl]
        partner = jnp.where(first_half,
                            pltpu.roll(y, LANES - HEAD_DIM // 2, axis=1),
                            pltpu.roll(y, HEAD_DIM // 2, axis=1))
        o_ref[:, sl] = (y * cos + partner * sin).astype(o_ref.dtype)


def qk_prep(proj, gains, cos_t, sin_t, seq):
    n = proj.shape[0]
    tiles_per_seq = seq // ROW_TILE
    return pl.pallas_call(
        _qk_prep_kernel,
        grid=(n // ROW_TILE, 4),
        in_specs=[pl.BlockSpec((ROW_TILE, 512), lambda i, g: (i, g + g // 2)),
                  pl.BlockSpec((1, 1, 512), lambda i, g: (g, 0, 0)),
                  pl.BlockSpec((ROW_TILE, LANES), lambda i, g: (i % tiles_per_seq, 0)),
                  pl.BlockSpec((ROW_TILE, LANES), lambda i, g: (i % tiles_per_seq, 0))],
        out_specs=pl.BlockSpec((ROW_TILE, 512), lambda i, g: (i, g)),
        out_shape=jax.ShapeDtypeStruct((n, 2048), BF16),
        compiler_params=_params("parallel", "parallel"),
        name="qk_prep",
    )(proj, gains, cos_t, sin_t)


DILATED_GROUP = 4


def _dilated_kernel(q_ref, k_ref, v_ref, o_ref, qf, kf, vf, of, lf, bias):
    seq = q_ref.shape[0]
    nb = A_BLOCK
    qf[...] = q_ref[...].astype(F32)
    kf[...] = k_ref[...].astype(F32)
    vf[...] = v_ref[...].astype(F32)
    lane = lax.broadcasted_iota(jnp.int32, (1, LANES), 1)
    low = lane < HEAD_DIM
    qi = lax.broadcasted_iota(jnp.int32, (2 * nb, 2 * nb), 0) % nb
    kj = lax.broadcasted_iota(jnp.int32, (2 * nb, 2 * nb), 1)
    band = (kj >= qi) & (kj <= qi + nb)
    bias[0] = jnp.where(band & (kj >= nb), 0.0, NEG_INF)
    bias[1] = jnp.where(band, 0.0, NEG_INF)
    zero = jnp.zeros((), BF16)
    for idx, (window, d) in enumerate(A_BRANCHES):
        assert window // d == nb
        nblk = seq // (d * nb)
        first, last = idx == 0, idx == len(A_BRANCHES) - 1

        group = min(DILATED_GROUP, nblk)
        nres = DILATED_GROUP // group
        tiles = [(a, j) for a in range(nres) for j in range(group)]

        def blocks(t, carry, d=d, nblk=nblk, first=first, last=last, group=group, nres=nres, tiles=tiles):
            r0 = (t // (nblk // group)) * nres
            c0 = (t % (nblk // group)) * group

            def rows(a, c):
                return pl.ds(r0 + a + c * (nb * d), nb, stride=d)

            kb = {(a, j): kf[rows(a, jnp.maximum(c0 - 1 + j, 0)), :].astype(BF16)
                  for a in range(nres) for j in range(group + 1)}
            vb = {(a, j): vf[rows(a, jnp.maximum(c0 - 1 + j, 0)), :].astype(BF16)
                  for a in range(nres) for j in range(group + 1)}
            qb = {(a, j): qf[rows(a, c0 + j), :].astype(BF16) for a, j in tiles}
            s = {(a, j): _dot_nt(
                jnp.concatenate([jnp.where(low, qb[a, j], zero), jnp.where(low, zero, qb[a, j])], axis=0),
                jnp.concatenate([kb[a, j], kb[a, j + 1]], axis=0))
                + bias[jnp.minimum(c0 + j, 1)] for a, j in tiles}
            m = {t_: jnp.max(s[t_], axis=1, keepdims=True) for t_ in tiles}
            p = {t_: jnp.exp(s[t_] - m[t_]) for t_ in tiles}
            den = {t_: jnp.sum(p[t_], axis=1, keepdims=True) for t_ in tiles}
            o2 = {(a, j): _dot(p[a, j].astype(BF16), jnp.concatenate([vb[a, j], vb[a, j + 1]], axis=0))
                  / den[a, j] for a, j in tiles}
            for a, j in tiles:
                cur = rows(a, c0 + j)
                lse2 = m[a, j] + jnp.log(den[a, j])
                o = jnp.where(low, o2[a, j][:nb], o2[a, j][nb:])
                lse = jnp.where(low, lse2[:nb], lse2[nb:])
                if not first:
                    o_prev = of[cur, :]
                    l_prev = lf[cur, :]
                    top = jnp.maximum(lse, l_prev)
                    wa = jnp.exp(lse - top)
                    wb = jnp.exp(l_prev - top)
                    tot = wa + wb
                    o = (wa * o + wb * o_prev) / tot
                    lse = top + jnp.log(tot)
                of[cur, :] = o
                if not last:
                    lf[cur, :] = lse
            return carry

        lax.fori_loop(0, seq // (nb * DILATED_GROUP), blocks, 0)
    o_ref[...] = of[...].astype(o_ref.dtype)


def dilated_attention(qk, proj, batch, seq):
    qk3 = qk.reshape(batch, seq, 2048)
    pj3 = proj.reshape(batch, seq, EVEN_IN)
    slabs = A_WIDTH // LANES
    blk = (None, seq, LANES)
    out = pl.pallas_call(
        _dilated_kernel,
        grid=(batch, slabs),
        in_specs=[pl.BlockSpec(blk, lambda b, s: (b, 0, s)),
                  pl.BlockSpec(blk, lambda b, s: (b, 0, slabs + s)),
                  pl.BlockSpec(blk, lambda b, s: (b, 0, 2 * slabs + s))],
        out_specs=pl.BlockSpec(blk, lambda b, s: (b, 0, s)),
        out_shape=jax.ShapeDtypeStruct((batch, seq, A_WIDTH), BF16),
        scratch_shapes=[pltpu.VMEM((seq, LANES), F32) for _ in range(5)]
        + [pltpu.VMEM((2, 2 * A_BLOCK, 2 * A_BLOCK), F32)],
        compiler_params=_params("parallel", "parallel"),
        name="dilated_attention",
    )(qk3, qk3, pj3)
    return out.reshape(batch * seq, A_WIDTH)


def _diff_attn_kernel(q_ref, k_ref, v_ref, lq1_ref, lk1_ref, lq2_ref, lk2_ref, g_ref, o_ref,
                      m_scr, l_scr, acc_scr, *, lam_init):
    h = pl.program_id(1)
    i = pl.program_id(2)
    tq = DIFF_TQ
    lane = lax.broadcasted_iota(jnp.int32, (1, LANES), 1)
    low = lane < HEAD_DIM
    zero = jnp.zeros((), BF16)
    q = q_ref[...]
    q2 = jnp.concatenate([jnp.where(low, q, zero), jnp.where(low, zero, q)], axis=0)
    m_scr[...] = jnp.full(m_scr.shape, NEG_INF, F32)
    l_scr[...] = jnp.zeros(l_scr.shape, F32)
    acc_scr[...] = jnp.zeros(acc_scr.shape, F32)

    def step(j, masked):
        start = pl.multiple_of(j * tq, tq)
        k = k_ref[pl.ds(start, tq), :]
        v = v_ref[pl.ds(start, tq), :]
        s = _dot_nt(q2, k)
        if masked:
            qi = lax.broadcasted_iota(jnp.int32, (2 * tq, tq), 0) % tq
            kj = lax.broadcasted_iota(jnp.int32, (2 * tq, tq), 1)
            s = jnp.where(kj <= qi, s, NEG_INF)
        m_old = m_scr[...]
        m_new = jnp.maximum(m_old, jnp.max(s, axis=1, keepdims=True))
        alpha = jnp.exp(m_old - m_new)
        p = jnp.exp(s - jnp.concatenate([m_new] * (tq // LANES), axis=1))
        l_scr[...] = alpha * l_scr[...] + jnp.sum(p, axis=1, keepdims=True)
        acc_scr[...] = alpha * acc_scr[...] + _dot(p.astype(BF16), v)
        m_scr[...] = m_new

    def body(j, carry):
        step(j, False)
        return carry

    lax.fori_loop(0, i, body, 0)
    step(i, True)

    acc = acc_scr[...] / l_scr[...]
    d1 = jnp.sum(lq1_ref[pl.ds(h, 1), :] * lk1_ref[pl.ds(h, 1), :], axis=1, keepdims=True)
    d2 = jnp.sum(lq2_ref[pl.ds(h, 1), :] * lk2_ref[pl.ds(h, 1), :], axis=1, keepdims=True)
    lam = jnp.exp(d1) - jnp.exp(d2) + lam_init
    o = acc[:tq] - lam * acc[tq:]
    o_ref[...] = (_rms_rows(o, g_ref[...]) * (1.0 - lam_init)).astype(o_ref.dtype)


def diff_attention(qk, proj, lq1, lk1, lq2, lk2, out_gain, lam_init, batch, seq):
    n = batch * seq
    qk3 = qk.reshape(batch, seq, 2048)
    pj3 = proj.reshape(batch, seq, EVEN_IN)
    qcol = 2 * A_WIDTH // LANES
    kcol = (2 * A_WIDTH + B_WIDTH) // LANES
    vcol = (3 * A_WIDTH + 2 * B_WIDTH) // LANES
    small = pl.BlockSpec((B_HEADS, HEAD_DIM), lambda b, h, i: (0, 0))
    out = pl.pallas_call(
        functools.partial(_diff_attn_kernel, lam_init=lam_init),
        grid=(batch, B_HEADS, seq // DIFF_TQ),
        in_specs=[pl.BlockSpec((None, DIFF_TQ, LANES), lambda b, h, i: (b, i, qcol + h)),
                  pl.BlockSpec((None, seq, LANES), lambda b, h, i: (b, 0, kcol + h)),
                  pl.BlockSpec((None, seq, LANES), lambda b, h, i: (b, 0, vcol + h)),
                  small, small, small, small,
                  pl.BlockSpec((1, LANES), lambda b, h, i: (0, 0))],
        out_specs=pl.BlockSpec((None, DIFF_TQ, LANES), lambda b, h, i: (b, i, h)),
        out_shape=jax.ShapeDtypeStruct((batch, seq, B_WIDTH), BF16),
        scratch_shapes=[pltpu.VMEM((2 * DIFF_TQ, LANES), F32),
                        pltpu.VMEM((2 * DIFF_TQ, LANES), F32),
                        pltpu.VMEM((2 * DIFF_TQ, LANES), F32)],
        compiler_params=_params("parallel", "parallel", "parallel"),
        name="diff_attention",
    )(qk3, qk3, pj3, lq1, lk1, lq2, lk2, out_gain.reshape(1, LANES))
    return out.reshape(n, B_WIDTH)


ROUTE_E1, ROUTE_E2, ROUTE_W1, ROUTE_W2 = 0, 1, 2, 3
EXPERT_LANE0 = 8


def _outproj_router_kernel(x_ref, a_ref, b_ref, wa_ref, wb_ref, g_ref, rhi_ref, rlo_ref, rb_ref,
                           xo_ref, h_ref, rt_ref):
    y = x_ref[...] + _dot(a_ref[...], wa_ref[...]) + _dot(b_ref[...], wb_ref[...])
    xo_ref[...] = y
    hn = _rms_rows(y, g_ref[...])
    h_ref[...] = hn
    hi = hn.astype(BF16)
    lo = (hn - hi.astype(F32)).astype(BF16)
    logits = (_dot(hi, rhi_ref[...]) + _dot(lo, rhi_ref[...]) + _dot(hi, rlo_ref[...])) + rb_ref[...]

    lane = lax.broadcasted_iota(jnp.int32, logits.shape, 1)
    big = jnp.int32(LANES)
    is_group = lane < N_GROUPS
    gl = jnp.where(is_group, logits, NEG_INF)
    gmax = jnp.max(gl, axis=1, keepdims=True)
    grp = jnp.min(jnp.where(is_group & (gl == gmax), lane, big), axis=1, keepdims=True)
    g_w = 1.0 / jnp.sum(jnp.where(is_group, jnp.exp(gl - gmax), 0.0), axis=1, keepdims=True)
    first = EXPERT_LANE0 + grp * EXPERTS_PER_GROUP
    in_group = (lane >= first) & (lane < first + EXPERTS_PER_GROUP)
    el = jnp.where(in_group, logits, NEG_INF)
    v1 = jnp.max(el, axis=1, keepdims=True)
    i1 = jnp.min(jnp.where(in_group & (el == v1), lane, big), axis=1, keepdims=True)
    rest = in_group & (lane != i1)
    el2 = jnp.where(rest, logits, NEG_INF)
    v2 = jnp.max(el2, axis=1, keepdims=True)
    i2 = jnp.min(jnp.where(rest & (el2 == v2), lane, big), axis=1, keepdims=True)
    e2w = jnp.exp(v2 - v1)
    w1 = g_w / (1.0 + e2w)
    w2 = g_w * e2w / (1.0 + e2w)
    e1 = (i1 - EXPERT_LANE0).astype(F32)
    e2 = (i2 - EXPERT_LANE0).astype(F32)
    rt_ref[...] = jnp.where(lane == ROUTE_E1, e1,
                            jnp.where(lane == ROUTE_E2, e2,
                                      jnp.where(lane == ROUTE_W1, w1,
                                                jnp.where(lane == ROUTE_W2, w2, 0.0))))


def outproj_router(x, mix_a, blk_a, mix_b, blk_b, w_out_bf16, ffn_gain, r_hi, r_lo, r_bias):
    n, d = x.shape
    half = w_out_bf16.shape[0] // 2
    row = lambda i: (i, 0)
    const = lambda i: (0, 0)
    return pl.pallas_call(
        _outproj_router_kernel,
        grid=(n // ROW_TILE,),
        in_specs=[pl.BlockSpec((ROW_TILE, d), row),
                  pl.BlockSpec((ROW_TILE, half), lambda i: (i, blk_a)),
                  pl.BlockSpec((ROW_TILE, half), lambda i: (i, blk_b)),
                  pl.BlockSpec((half, d), lambda i: (0, 0)),
                  pl.BlockSpec((half, d), lambda i: (1, 0)),
                  pl.BlockSpec((1, d), const),
                  pl.BlockSpec((d, LANES), const),
                  pl.BlockSpec((d, LANES), const),
                  pl.BlockSpec((1, LANES), const)],
        out_specs=[pl.BlockSpec((ROW_TILE, d), row),
                   pl.BlockSpec((ROW_TILE, d), row),
                   pl.BlockSpec((ROW_TILE, LANES), row)],
        out_shape=[jax.ShapeDtypeStruct((n, d), F32),
                   jax.ShapeDtypeStruct((n, d), F32),
                   jax.ShapeDtypeStruct((n, LANES), F32)],
        compiler_params=_params("parallel"),
        name="outproj_router",
    )(x, mix_a, mix_b, w_out_bf16, w_out_bf16, ffn_gain.reshape(1, d), r_hi, r_lo, r_bias)


def _rank_kernel(rt_ref, tri_ref, rk_ref, cnt_ref, carry_scr):
    @pl.when(pl.program_id(0) == 0)
    def _():
        carry_scr[...] = jnp.zeros(carry_scr.shape, F32)

    rt = rt_ref[...]
    lane = lax.broadcasted_iota(jnp.int32, rt.shape, 1)
    e1 = rt[:, ROUTE_E1:ROUTE_E1 + 1].astype(jnp.int32)
    e2 = rt[:, ROUTE_E2:ROUTE_E2 + 1].astype(jnp.int32)
    oh1 = lane == e1
    oh2 = lane == e2
    both = jnp.where(oh1 | oh2, 1.0, 0.0)
    before = _dot(tri_ref[...], both.astype(BF16)) + carry_scr[0:1, :]
    r1 = jnp.sum(jnp.where(oh1, before, 0.0), axis=1, keepdims=True)
    r2 = jnp.sum(jnp.where(oh2, before, 0.0), axis=1, keepdims=True)
    rk_ref[...] = jnp.where(lane == 0, r1, jnp.where(lane == 1, r2, 0.0))
    total = carry_scr[0:1, :] + jnp.sum(both, axis=0, keepdims=True)
    carry_scr[...] = jnp.broadcast_to(total, carry_scr.shape)
    cnt_ref[...] = jnp.broadcast_to(total, cnt_ref.shape)


def rank_slots(route):
    n = route.shape[0]
    tri = jnp.tril(jnp.ones((ROW_TILE, ROW_TILE), BF16), k=-1)
    return pl.pallas_call(
        _rank_kernel,
        grid=(n // ROW_TILE,),
        in_specs=[pl.BlockSpec((ROW_TILE, LANES), lambda i: (i, 0)),
                  pl.BlockSpec((ROW_TILE, ROW_TILE), lambda i: (0, 0))],
        out_specs=[pl.BlockSpec((ROW_TILE, LANES), lambda i: (i, 0)),
                   pl.BlockSpec((8, LANES), lambda i: (0, 0))],
        out_shape=[jax.ShapeDtypeStruct((n, LANES), F32),
                   jax.ShapeDtypeStruct((8, LANES), F32)],
        scratch_shapes=[pltpu.VMEM((8, LANES), F32)],
        compiler_params=_params("arbitrary"),
        name="moe_rank",
    )(route, tri)


DISPATCH_TOKENS = 512
DMA_UNROLL = 8


def _dispatch_kernel(dest_ref, h_ref, zero_ref, xs_ref, sem):
    del zero_ref
    base = pl.program_id(0) * DISPATCH_TOKENS

    def row_copy(i, k):
        return pltpu.make_async_copy(h_ref.at[pl.ds(i, 1)],
                                     xs_ref.at[pl.ds(dest_ref[TOP_K * (base + i) + k], 1)], sem)

    def issue(i, carry):
        for k in range(TOP_K):
            row_copy(i, k).start()
        return carry

    lax.fori_loop(0, DISPATCH_TOKENS, issue, 0, unroll=DMA_UNROLL)

    def drain(i, carry):
        for k in range(TOP_K):
            row_copy(i, k).wait()
        return carry

    lax.fori_loop(0, DISPATCH_TOKENS, drain, 0, unroll=DMA_UNROLL)


def dispatch_rows(h, dest, n_rows):
    n, d = h.shape
    return pl.pallas_call(
        _dispatch_kernel,
        grid_spec=pltpu.PrefetchScalarGridSpec(
            num_scalar_prefetch=1,
            grid=(n // DISPATCH_TOKENS,),
            in_specs=[pl.BlockSpec((DISPATCH_TOKENS, d), lambda i, dst: (i, 0)),
                      pl.BlockSpec(memory_space=pl.ANY)],
            out_specs=pl.BlockSpec(memory_space=pl.ANY),
            scratch_shapes=[pltpu.SemaphoreType.DMA(())]),
        out_shape=jax.ShapeDtypeStruct((n_rows, d), F32),
        input_output_aliases={2: 0},
        compiler_params=_params("arbitrary"),
        name="moe_dispatch",
    )(dest, h, jnp.zeros((n_rows, d), F32))


def _expert_kernel(be_ref, nb_ref, x_ref, wg_ref, wu_ref, wd_ref, y_ref):
    del be_ref

    @pl.when(pl.program_id(0) < nb_ref[0])
    def _():
        xb = x_ref[...].astype(BF16)
        gate = _dot(xb, wg_ref[...])
        up = _dot(xb, wu_ref[...])
        hidden = (gate * _sigmoid(gate) * up).astype(BF16)
        y_ref[...] = _dot(hidden, wd_ref[...])

    @pl.when(pl.program_id(0) >= nb_ref[0])
    def _():
        y_ref[...] = jnp.zeros(y_ref.shape, F32)


def expert_ffn(xs, block_expert, n_used, w_gate, w_up, w_down):
    p, d = xs.shape
    ff = w_gate.shape[2]
    return pl.pallas_call(
        _expert_kernel,
        grid_spec=pltpu.PrefetchScalarGridSpec(
            num_scalar_prefetch=2,
            grid=(p // MOE_TILE,),
            in_specs=[pl.BlockSpec((MOE_TILE, d), lambda i, be, nb: (i, 0)),
                      pl.BlockSpec((None, d, ff), lambda i, be, nb: (be[i], 0, 0)),
                      pl.BlockSpec((None, d, ff), lambda i, be, nb: (be[i], 0, 0)),
                      pl.BlockSpec((None, ff, d), lambda i, be, nb: (be[i], 0, 0))],
            out_specs=pl.BlockSpec((MOE_TILE, d), lambda i, be, nb: (i, 0))),
        out_shape=jax.ShapeDtypeStruct((p, d), F32),
        compiler_params=_params("arbitrary"),
        name="moe_experts",
    )(block_expert, n_used, xs, w_gate, w_up, w_down)


COMBINE_TOKENS = 512


def _combine_kernel(dest_ref, x_ref, rt_ref, y_ref, o_ref, buf0, buf1, sem):
    base = pl.program_id(0) * COMBINE_TOKENS
    bufs = (buf0, buf1)

    def row_copy(i, k):
        return pltpu.make_async_copy(y_ref.at[pl.ds(dest_ref[TOP_K * (base + i) + k], 1)],
                                     bufs[k].at[pl.ds(i, 1)], sem)

    def issue(i, carry):
        for k in range(TOP_K):
            row_copy(i, k).start()
        return carry

    lax.fori_loop(0, COMBINE_TOKENS, issue, 0, unroll=DMA_UNROLL)

    def drain(i, carry):
        for k in range(TOP_K):
            row_copy(i, k).wait()
        return carry

    lax.fori_loop(0, COMBINE_TOKENS, drain, 0, unroll=DMA_UNROLL)
    rt = rt_ref[...]
    w1 = rt[:, ROUTE_W1:ROUTE_W1 + 1]
    w2 = rt[:, ROUTE_W2:ROUTE_W2 + 1]
    o_ref[...] = x_ref[...] + (w1 * buf0[...] + w2 * buf1[...])


def combine_rows(x, route, y, dest):
    n, d = x.shape
    return pl.pallas_call(
        _combine_kernel,
        grid_spec=pltpu.PrefetchScalarGridSpec(
            num_scalar_prefetch=1,
            grid=(n // COMBINE_TOKENS,),
            in_specs=[pl.BlockSpec((COMBINE_TOKENS, d), lambda i, dst: (i, 0)),
                      pl.BlockSpec((COMBINE_TOKENS, LANES), lambda i, dst: (i, 0)),
                      pl.BlockSpec(memory_space=pl.ANY)],
            out_specs=pl.BlockSpec((COMBINE_TOKENS, d), lambda i, dst: (i, 0)),
            scratch_shapes=[pltpu.VMEM((COMBINE_TOKENS, d), F32),
                            pltpu.VMEM((COMBINE_TOKENS, d), F32),
                            pltpu.SemaphoreType.DMA(())]),
        out_shape=jax.ShapeDtypeStruct((n, d), F32),
        compiler_params=_params("arbitrary"),
        name="moe_combine",
    )(dest, x, route, y)


def moe_layer(x, h, route, w_gate, w_up, w_down):
    n = x.shape[0]
    ranks, counts = rank_slots(route)
    counts = counts[0, :N_EXPERTS].astype(jnp.int32)
    padded = (counts + MOE_TILE - 1) // MOE_TILE * MOE_TILE
    pad_end = jnp.cumsum(padded)
    pad_start = pad_end - padded
    n_blocks = -(-(n * TOP_K + N_EXPERTS * (MOE_TILE - 1)) // MOE_TILE)
    eids = route[:, ROUTE_E1:ROUTE_E2 + 1].astype(jnp.int32)
    slot0 = jnp.sum(jnp.where(eids[..., None] == jnp.arange(N_EXPERTS), pad_start, 0), axis=-1)
    dest = (slot0 + ranks[:, :TOP_K].astype(jnp.int32)).reshape(-1)
    block_start = jnp.arange(n_blocks, dtype=jnp.int32) * MOE_TILE
    block_expert = jnp.minimum(
        jnp.sum(block_start[:, None] >= pad_end[None, :], axis=1), N_EXPERTS - 1).astype(jnp.int32)
    n_used = (pad_end[-1:] // MOE_TILE).astype(jnp.int32)
    xs = dispatch_rows(h, dest, n_blocks * MOE_TILE)
    y = expert_ffn(xs, block_expert, n_used, w_gate, w_up, w_down)
    return combine_rows(x, route, y, dest)


def router_tables(w_group, b_group, w_erouter, b_erouter):
    d = w_group.shape[0]
    w_exp = jnp.transpose(w_erouter, (1, 0, 2)).reshape(d, N_EXPERTS)
    w = jnp.zeros((d, LANES), F32)
    w = w.at[:, :N_GROUPS].set(w_group.astype(F32))
    w = w.at[:, EXPERT_LANE0:EXPERT_LANE0 + N_EXPERTS].set(w_exp.astype(F32))
    bias = jnp.zeros((1, LANES), F32)
    bias = bias.at[0, :N_GROUPS].set(b_group.astype(F32))
    bias = bias.at[0, EXPERT_LANE0:EXPERT_LANE0 + N_EXPERTS].set(b_erouter.reshape(-1).astype(F32))
    hi = w.astype(BF16)
    lo = (w - hi.astype(F32)).astype(BF16)
    return hi, lo, bias


def _gates_kernel(ab_ref, a_ref, dt_ref, tri_ref, o_ref):
    x = ab_ref[...].astype(F32)
    lane = lax.broadcasted_iota(jnp.int32, x.shape, 1)
    z = x + dt_ref[...]
    softplus = jnp.maximum(z, 0.0) + jnp.log(1.0 + jnp.exp(-jnp.abs(z)))
    g = -jnp.exp(a_ref[...]) * softplus
    gcum = _split3_dot(tri_ref[...], g)
    o_ref[...] = jnp.where(lane < C_HEADS, gcum, _sigmoid(x))


def gdn_gates(proj, a_log, dt_bias):
    n = proj.shape[0]
    pad = lambda v: jnp.zeros((1, LANES), F32).at[0, :C_HEADS].set(v.astype(F32))
    r = jnp.arange(ROW_TILE)
    tri = ((r[:, None] >= r[None, :]) & (r[:, None] // CHUNK == r[None, :] // CHUNK)).astype(BF16)
    return pl.pallas_call(
        _gates_kernel,
        grid=(n // ROW_TILE,),
        in_specs=[pl.BlockSpec((ROW_TILE, LANES), lambda i: (i, 4 * C_WIDTH // LANES)),
                  pl.BlockSpec((1, LANES), lambda i: (0, 0)),
                  pl.BlockSpec((1, LANES), lambda i: (0, 0)),
                  pl.BlockSpec((ROW_TILE, ROW_TILE), lambda i: (0, 0))],
        out_specs=pl.BlockSpec((ROW_TILE, LANES), lambda i: (i, 0)),
        out_shape=jax.ShapeDtypeStruct((n, LANES), F32),
        compiler_params=_params("parallel"),
        name="gdn_gates",
    )(proj, pad(a_log), pad(dt_bias), tri)


CONV_COLS = 256
CONV_ROWS = 512
CONV_HALO = 8


def _conv_kernel(x_ref, w_ref, o_ref, pad_scr, *, normalize, scale, scale_blocks):
    seq = x_ref.shape[0]
    out_scale = jnp.where(pl.program_id(1) < scale_blocks, scale, 1.0)
    pad_scr[0:CONV_HALO, :] = jnp.zeros((CONV_HALO, CONV_COLS), F32)
    pad_scr[CONV_HALO:, :] = x_ref[...].astype(F32)
    w = w_ref[...]
    for r0 in range(0, seq, CONV_ROWS):
        acc = None
        for j in range(CONV_W):
            start = CONV_HALO + r0 - (CONV_W - 1) + j
            term = pad_scr[start:start + CONV_ROWS, :] * w[j:j + 1, :]
            acc = term if acc is None else acc + term
        y = acc * _sigmoid(acc)
        if normalize:
            for s in range(CONV_COLS // LANES):
                ys = y[:, s * LANES:(s + 1) * LANES]
                ssq = jnp.sum(ys * ys, axis=1, keepdims=True)
                o_ref[r0:r0 + CONV_ROWS, s * LANES:(s + 1) * LANES] = (
                    ys * lax.rsqrt(ssq + EPS) * out_scale).astype(o_ref.dtype)
        else:
            o_ref[r0:r0 + CONV_ROWS, :] = y.astype(o_ref.dtype)


def conv_silu(proj3, conv_w, col0, width, normalize, scale=1.0, scale_width=0):
    batch, seq, _ = proj3.shape
    cb0 = col0 // CONV_COLS
    return pl.pallas_call(
        functools.partial(_conv_kernel, normalize=normalize, scale=scale,
                          scale_blocks=scale_width // CONV_COLS),
        grid=(batch, width // CONV_COLS),
        in_specs=[pl.BlockSpec((None, seq, CONV_COLS), lambda b, c: (b, 0, cb0 + c)),
                  pl.BlockSpec((CONV_W, CONV_COLS), lambda b, c: (0, cb0 + c))],
        out_specs=pl.BlockSpec((None, seq, CONV_COLS), lambda b, c: (b, 0, c)),
        out_shape=jax.ShapeDtypeStruct((batch, seq, width), BF16),
        scratch_shapes=[pltpu.VMEM((seq + CONV_HALO, CONV_COLS), F32)],
        compiler_params=_params("parallel", "parallel"),
        name="conv_silu",
    )(proj3, conv_w)


def _head_column(gb, lane_index):
    lane = lax.broadcasted_iota(jnp.int32, gb.shape, 1)
    return jnp.sum(jnp.where(lane == lane_index, gb, 0.0), axis=1, keepdims=True)


def _chunk_decay(gcum_col):
    r = lax.broadcasted_iota(jnp.int32, (CHUNK, CHUNK), 0)
    c = lax.broadcasted_iota(jnp.int32, (CHUNK, CHUNK), 1)
    gcum = jnp.broadcast_to(gcum_col, (CHUNK, LANES))
    gcum_rows = jnp.transpose(gcum)[:CHUNK, :]
    gdiff = jnp.where(c <= r, gcum[:, :CHUNK] - gcum_rows, 0.0)
    return gcum, gdiff, r, c


def _split3_dot(a_bf16, b_f32):
    b0 = b_f32.astype(BF16)
    r1 = b_f32 - b0.astype(F32)
    b1 = r1.astype(BF16)
    b2 = (r1 - b1.astype(F32)).astype(BF16)
    return _dot(a_bf16, b0) + _dot(a_bf16, b1) + _dot(a_bf16, b2)


def _lmat_kernel(k_ref, gb_ref, l_ref):
    h = pl.program_id(1)
    gb = gb_ref[...]
    gcum_all = _head_column(gb, h)
    beta_all = _head_column(gb, C_HEADS + h)
    for ci in range(GDN_ROWS // CHUNK):
        rows = slice(ci * CHUNK, (ci + 1) * CHUNK)
        _, gdiff, r, c = _chunk_decay(gcum_all[rows])
        k = k_ref[rows, :]
        kk = _dot_nt(k, k)
        l_ref[rows, :] = jnp.where(r > c, beta_all[rows] * kk * jnp.exp(gdiff), 0.0)


def gdn_lmat(k3, gb3):
    batch, seq, _ = k3.shape
    return pl.pallas_call(
        _lmat_kernel,
        grid=(batch, C_HEADS, seq // GDN_ROWS),
        in_specs=[pl.BlockSpec((None, GDN_ROWS, LANES), lambda b, h, i: (b, i, C_HEADS + h)),
                  pl.BlockSpec((None, GDN_ROWS, LANES), lambda b, h, i: (b, i, 0))],
        out_specs=pl.BlockSpec((None, None, GDN_ROWS, CHUNK), lambda b, h, i: (b, h, i, 0)),
        out_shape=jax.ShapeDtypeStruct((batch, C_HEADS, seq, CHUNK), F32),
        compiler_params=_params("parallel", "parallel", "parallel"),
        name="gdn_lmat",
    )(k3, gb3)


def _solve_kernel(l_ref, t_ref):
    col = lax.broadcasted_iota(jnp.int32, (CHUNK, SOLVE_LANES), 0)
    for i in range(CHUNK):
        acc = jnp.where(col == i, 1.0, 0.0)
        for j in range(i):
            acc = acc - l_ref[i, j:j + 1, :] * t_ref[j]
        t_ref[i] = acc


def gdn_solve(lmat_t):
    g = lmat_t.shape[2]
    return pl.pallas_call(
        _solve_kernel,
        grid=(g // SOLVE_LANES,),
        in_specs=[pl.BlockSpec((CHUNK, CHUNK, SOLVE_LANES), lambda i: (0, 0, i))],
        out_specs=pl.BlockSpec((CHUNK, CHUNK, SOLVE_LANES), lambda i: (0, 0, i)),
        out_shape=jax.ShapeDtypeStruct(lmat_t.shape, F32),
        compiler_params=_params("parallel"),
        name="gdn_solve",
    )(lmat_t)


DELTA_HEADS = 4
DELTA_ROWS = 256


def _delta_kernel(q_ref, k_ref, v_ref, z_ref, gb_ref, t_ref, gain_ref, o_ref, *state_scrs):
    hg = pl.program_id(1)

    @pl.when(pl.program_id(2) == 0)
    def _():
        for scr in state_scrs:
            scr[...] = jnp.zeros(scr.shape, F32)

    states = [scr[...] for scr in state_scrs]
    gb = gb_ref[...]
    gain = gain_ref[...]
    n_chunks = DELTA_ROWS // CHUNK
    tiles = [(ci, hh) for ci in range(n_chunks) for hh in range(DELTA_HEADS)]
    rows = lambda ci: slice(ci * CHUNK, (ci + 1) * CHUNK)
    cols = lambda hh: slice(hh * LANES, (hh + 1) * LANES)

    pre = {}
    for ci, hh in tiles:
        head = hg * DELTA_HEADS + hh
        gb_c = gb[rows(ci)]
        gcum, gdiff, r, c = _chunk_decay(_head_column(gb_c, head))
        beta = _head_column(gb_c, C_HEADS + head)
        q = q_ref[rows(ci), cols(hh)]
        k = k_ref[rows(ci), cols(hh)]
        kf = k.astype(F32)
        eg = jnp.exp(gcum)
        g_last = gcum[CHUNK - 1:CHUNK, :]
        pre[ci, hh] = dict(
            q=q, k=k, eg=eg, e_last=jnp.exp(g_last),
            decay=jnp.where(c <= r, jnp.exp(gdiff), 0.0),
            k_dec=(kf * jnp.exp(g_last - gcum)).astype(BF16),
            rhs=jnp.concatenate([(v_ref[rows(ci), cols(hh)].astype(F32) * beta).astype(BF16),
                                 (kf * beta * eg).astype(BF16)], axis=1))
    uw = {t: _dot(t_ref[t[1], rows(t[0]), :].astype(BF16), pre[t]["rhs"]).astype(BF16)
          for t in tiles}
    a_intra = {t: (_dot_nt(pre[t]["q"], pre[t]["k"]) * pre[t]["decay"]).astype(BF16) for t in tiles}
    a_uw = {t: _dot(a_intra[t], uw[t]) for t in tiles}
    kd_uw = {t: _dot_tn(pre[t]["k_dec"], uw[t]) for t in tiles}
    lhs = {t: jnp.concatenate(
        [(pre[t]["q"].astype(F32) * pre[t]["eg"] - a_uw[t][:, LANES:]).astype(BF16),
         kd_uw[t][:, LANES:].astype(BF16)], axis=0) for t in tiles}

    for ci in range(n_chunks):
        for hh in range(DELTA_HEADS):
            t = (ci, hh)
            state = states[hh]
            prod = _dot(lhs[t], state.astype(BF16))
            o = prod[:CHUNK] + a_uw[t][:, :LANES]
            states[hh] = state * pre[t]["e_last"] - prod[CHUNK:] + kd_uw[t][:, :LANES]
            z = z_ref[rows(ci), cols(hh)].astype(F32)
            o_ref[rows(ci), cols(hh)] = (_rms_rows(o, gain) * (z * _sigmoid(z))).astype(o_ref.dtype)
    for scr, state in zip(state_scrs, states):
        scr[...] = state


def gdn_delta(qk3, v3, proj3, gb3, tinv, out_gain):
    batch, seq, _ = v3.shape
    width = DELTA_HEADS * LANES
    groups = C_HEADS // DELTA_HEADS
    zcol = 3 * C_WIDTH // width
    blk = (None, DELTA_ROWS, width)
    return pl.pallas_call(
        _delta_kernel,
        grid=(batch, groups, seq // DELTA_ROWS),
        in_specs=[pl.BlockSpec(blk, lambda b, h, i: (b, i, h)),
                  pl.BlockSpec(blk, lambda b, h, i: (b, i, groups + h)),
                  pl.BlockSpec(blk, lambda b, h, i: (b, i, h)),
                  pl.BlockSpec(blk, lambda b, h, i: (b, i, zcol + h)),
                  pl.BlockSpec((None, DELTA_ROWS, LANES), lambda b, h, i: (b, i, 0)),
                  pl.BlockSpec((None, DELTA_HEADS, DELTA_ROWS, CHUNK), lambda b, h, i: (b, h, i, 0)),
                  pl.BlockSpec((1, LANES), lambda b, h, i: (0, 0))],
        out_specs=pl.BlockSpec(blk, lambda b, h, i: (b, i, h)),
        out_shape=jax.ShapeDtypeStruct((batch, seq, C_WIDTH), BF16),
        scratch_shapes=[pltpu.VMEM((C_DK, LANES), F32) for _ in range(DELTA_HEADS)],
        compiler_params=_params("parallel", "parallel", "arbitrary"),
        name="gdn_delta",
    )(qk3, qk3, v3, proj3, gb3, tinv, out_gain.reshape(1, LANES))


def rope_tables(seq):
    half = HEAD_DIM // 2
    inv_freq = 1.0 / (ROPE_THETA ** (jnp.arange(0, HEAD_DIM, 2, dtype=F32) / HEAD_DIM))
    ang = jnp.arange(seq, dtype=F32)[:, None] * inv_freq[None, :]
    cos, sin = jnp.cos(ang), jnp.sin(ang)
    cos_t = jnp.tile(cos, (1, LANES // half))
    sin_t = jnp.tile(jnp.concatenate([-sin, sin], axis=1), (1, LANES // HEAD_DIM))
    return cos_t, sin_t


def even_layer(x, batch, seq, layer, attn_norm, w_in, a_q_gain, a_k_gain, b_q_gain, b_k_gain,
               lq1, lk1, lq2, lk2, b_out_gain, w_out, ffn_gain, router):
    lam_init = 0.8 - 0.6 * math.exp(-0.3 * layer)
    proj = norm_matmul(x, attn_norm, w_in.astype(BF16), 512)
    scale = HEAD_DIM ** -0.5
    tile = lambda g, s: jnp.tile(g.astype(F32) * s, A_WIDTH // HEAD_DIM)
    gains = jnp.stack([tile(a_q_gain, scale), tile(a_k_gain, 1.0),
                       tile(b_q_gain, scale), tile(b_k_gain, 1.0)]).reshape(4, 1, 512)
    cos_t, sin_t = rope_tables(seq)
    qk = qk_prep(proj, gains, cos_t, sin_t, seq)
    oa = dilated_attention(qk, proj, batch, seq)
    ob = diff_attention(qk, proj, lq1.astype(F32), lk1.astype(F32), lq2.astype(F32), lk2.astype(F32),
                        b_out_gain, lam_init, batch, seq)
    return outproj_router(x, oa, 0, ob, 0, w_out.astype(BF16), ffn_gain, *router)


def odd_layer(x, batch, seq, attn_norm, w_in, conv_w, a_log, dt_bias, out_gain, w_out, ffn_gain, router):
    n = batch * seq
    w_pad = jnp.pad(w_in, ((0, 0), (0, LANES - 2 * C_HEADS))).astype(BF16)
    proj = norm_matmul(x, attn_norm, w_pad, 1408)
    gb = gdn_gates(proj, a_log, dt_bias)
    proj3 = proj.reshape(batch, seq, proj.shape[1])
    conv_w = conv_w.astype(F32)
    qk3 = conv_silu(proj3, conv_w, 0, 2 * C_WIDTH, True, C_DK ** -0.5, C_WIDTH)
    v3 = conv_silu(proj3, conv_w, 2 * C_WIDTH, C_WIDTH, False)
    gb3 = gb.reshape(batch, seq, LANES)
    lmat = gdn_lmat(qk3, gb3)
    g = batch * C_HEADS * (seq // CHUNK)
    lmat_t = jnp.transpose(lmat.reshape(g, CHUNK, CHUNK), (1, 2, 0))
    tinv = jnp.transpose(gdn_solve(lmat_t), (2, 0, 1)).reshape(batch, C_HEADS, seq, CHUNK)
    mix = gdn_delta(qk3, v3, proj3, gb3, tinv, out_gain).reshape(n, C_WIDTH)
    return outproj_router(x, mix, 0, mix, 1, w_out.astype(BF16), ffn_gain, *router)


def kernel(x, even_attn_norm, even_w_in, a_q_norm, a_k_norm, b_q_norm, b_k_norm, b_lambda_q1, b_lambda_k1, b_lambda_q2, b_lambda_k2, b_out_norm, even_w_out, odd_attn_norm, odd_w_in, odd_conv_w, odd_a_log, odd_dt_bias, odd_out_norm, odd_w_out, ffn_norm, router_group_w, router_group_b, router_expert_w, router_expert_b, expert_w_gate, expert_w_up, expert_w_down):
    batch, seq, d = x.shape
    depth = ffn_norm.shape[0]
    xf = x.reshape(batch * seq, d).astype(F32)
    for layer in range(depth):
        i = layer // 2
        router = router_tables(router_group_w[layer], router_group_b[layer],
                               router_expert_w[layer], router_expert_b[layer])
        if layer % 2 == 0:
            xf, h, route = even_layer(
                xf, batch, seq, layer, even_attn_norm[i], even_w_in[i], a_q_norm[i], a_k_norm[i],
                b_q_norm[i], b_k_norm[i], b_lambda_q1[i], b_lambda_k1[i], b_lambda_q2[i],
                b_lambda_k2[i], b_out_norm[i], even_w_out[i], ffn_norm[layer], router)
        else:
            xf, h, route = odd_layer(
                xf, batch, seq, odd_attn_norm[i], odd_w_in[i], odd_conv_w[i], odd_a_log[i],
                odd_dt_bias[i], odd_out_norm[i], odd_w_out[i], ffn_norm[layer], router)
        xf = moe_layer(xf, h, route, expert_w_gate[layer].astype(BF16),
                       expert_w_up[layer].astype(BF16), expert_w_down[layer].astype(BF16))
    return xf.reshape(batch, seq, d).astype(x.dtype)
```

```python
import functools
import math

import jax
import jax.numpy as jnp
from jax import lax
from jax.experimental import pallas as pl
from jax.experimental.pallas import tpu as pltpu

F32 = jnp.float32
BF16 = jnp.bfloat16

D_MODEL = 1024
HEAD_DIM = 64
ROPE_THETA = 10000.0
A_HEADS = 8
A_BRANCHES = ((128, 1), (512, 4), (2048, 16))
A_BLOCK = 128
B_HEADS = 4
A_WIDTH = 512
B_WIDTH = 512
EVEN_IN = 3 * A_WIDTH + 3 * B_WIDTH
C_HEADS = 8
C_DK = 128
C_WIDTH = 1024
CONV_W = 4
CHUNK = 64
N_GROUPS = 4
EXPERTS_PER_GROUP = 8
N_EXPERTS = 32
TOP_K = 2
EXPERT_FF = 512
EPS = 1e-6
NEG_INF = -1e30

LANES = 128
VMEM_LIMIT = 48 * 1024 * 1024

ROW_TILE = 512
MOE_TILE = 256
DIFF_TQ = 512
GDN_ROWS = 512
SOLVE_LANES = 128


def _params(*sem):
    return pltpu.CompilerParams(dimension_semantics=sem, vmem_limit_bytes=VMEM_LIMIT)


def _dot(a, b):
    return jnp.dot(a, b, preferred_element_type=F32)


def _dot_nt(a, b):
    return lax.dot_general(a, b, (((1,), (1,)), ((), ())), preferred_element_type=F32)


def _dot_tn(a, b):
    return lax.dot_general(a, b, (((0,), (0,)), ((), ())), preferred_element_type=F32)


def _split_dot(a_f32, b_bf16):
    hi = a_f32.astype(BF16)
    lo = (a_f32 - hi.astype(F32)).astype(BF16)
    return _dot(hi, b_bf16) + _dot(lo, b_bf16)


def _sigmoid(x):
    return 1.0 / (1.0 + jnp.exp(-x))


def _rms_rows(xf, gain):
    ms = jnp.mean(xf * xf, axis=-1, keepdims=True)
    return xf * lax.rsqrt(ms + EPS) * gain


def _norm_matmul_kernel(x_ref, g_ref, w_ref, o_ref, *, col_chunk):
    h = _rms_rows(x_ref[...], g_ref[...]).astype(BF16)
    for c in range(0, o_ref.shape[1], col_chunk):
        o_ref[:, c:c + col_chunk] = _dot(h, w_ref[:, c:c + col_chunk]).astype(o_ref.dtype)


def norm_matmul(x, gain, w_bf16, col_chunk):
    n, d = x.shape
    m = w_bf16.shape[1]
    return pl.pallas_call(
        functools.partial(_norm_matmul_kernel, col_chunk=col_chunk),
        grid=(n // ROW_TILE,),
        in_specs=[pl.BlockSpec((ROW_TILE, d), lambda i: (i, 0)),
                  pl.BlockSpec((1, d), lambda i: (0, 0)),
                  pl.BlockSpec((d, m), lambda i: (0, 0))],
        out_specs=pl.BlockSpec((ROW_TILE, m), lambda i: (i, 0)),
        out_shape=jax.ShapeDtypeStruct((n, m), BF16),
        compiler_params=_params("parallel"),
        name="norm_matmul",
    )(x, gain.reshape(1, d), w_bf16)


def _even_in_kernel(x_ref, g_ref, w_ref, hg_ref, cos_ref, sin_ref, qk_ref, v_ref):
    h = _rms_rows(x_ref[...], g_ref[...]).astype(BF16)
    lane = lax.broadcasted_iota(jnp.int32, (1, LANES), 1)
    first_half = (lane % HEAD_DIM) < (HEAD_DIM // 2)
    r = lax.broadcasted_iota(jnp.int32, (LANES, LANES), 0) // HEAD_DIM
    c = lax.broadcasted_iota(jnp.int32, (LANES, LANES), 1) // HEAD_DIM
    same_head = jnp.where(r == c, 1.0, 0.0).astype(BF16)
    cos = cos_ref[...]
    sin = sin_ref[...]
    for chunk in range(EVEN_IN // A_WIDTH):
        y = _dot(h, w_ref[:, chunk * A_WIDTH:(chunk + 1) * A_WIDTH])
        if chunk % 3 == 2:
            v_ref[:, (chunk // 3) * A_WIDTH:(chunk // 3 + 1) * A_WIDTH] = y.astype(v_ref.dtype)
            continue
        group = chunk - chunk // 3
        for s in range(A_WIDTH // LANES):
            sl = slice(s * LANES, (s + 1) * LANES)
            x = y[:, sl]
            ssq = _split_dot(x * x, same_head)
            z = x * lax.rsqrt(ssq * (1.0 / HEAD_DIM) + EPS) * hg_ref[group:group + 1, sl]
            partner = jnp.where(first_half,
                                pltpu.roll(z, LANES - HEAD_DIM // 2, axis=1),
                                pltpu.roll(z, HEAD_DIM // 2, axis=1))
            qk_ref[:, group * A_WIDTH + s * LANES:group * A_WIDTH + (s + 1) * LANES] = (
                z * cos + partner * sin).astype(qk_ref.dtype)


def even_in_proj(x, gain, w_bf16, head_gains, cos_t, sin_t, seq):
    n, d = x.shape
    tiles_per_seq = seq // ROW_TILE
    return pl.pallas_call(
        _even_in_kernel,
        grid=(n // ROW_TILE,),
        in_specs=[pl.BlockSpec((ROW_TILE, d), lambda i: (i, 0)),
                  pl.BlockSpec((1, d), lambda i: (0, 0)),
                  pl.BlockSpec((d, EVEN_IN), lambda i: (0, 0)),
                  pl.BlockSpec((4, A_WIDTH), lambda i: (0, 0)),
                  pl.BlockSpec((ROW_TILE, LANES), lambda i: (i % tiles_per_seq, 0)),
                  pl.BlockSpec((ROW_TILE, LANES), lambda i: (i % tiles_per_seq, 0))],
        out_specs=[pl.BlockSpec((ROW_TILE, 4 * A_WIDTH), lambda i: (i, 0)),
                   pl.BlockSpec((ROW_TILE, 2 * A_WIDTH), lambda i: (i, 0))],
        out_shape=[jax.ShapeDtypeStruct((n, 4 * A_WIDTH), BF16),
                   jax.ShapeDtypeStruct((n, 2 * A_WIDTH), BF16)],
        compiler_params=_params("parallel"),
        name="even_in_proj",
    )(x, gain.reshape(1, d), w_bf16, head_gains, cos_t, sin_t)


DILATED_GROUP = 4


def _dilated_kernel(q_ref, k_ref, v_ref, o_ref, qf, kf, vf, of, lf, bias):
    seq = q_ref.shape[0]
    nb = A_BLOCK
    qf[...] = q_ref[...].astype(F32)
    kf[...] = k_ref[...].astype(F32)
    vf[...] = v_ref[...].astype(F32)
    lane = lax.broadcasted_iota(jnp.int32, (1, LANES), 1)
    low = lane < HEAD_DIM
    qi = lax.broadcasted_iota(jnp.int32, (2 * nb, 2 * nb), 0) % nb
    kj = lax.broadcasted_iota(jnp.int32, (2 * nb, 2 * nb), 1)
    band = (kj >= qi) & (kj <= qi + nb)
    bias[0] = jnp.where(band & (kj >= nb), 0.0, NEG_INF)
    bias[1] = jnp.where(band, 0.0, NEG_INF)
    zero = jnp.zeros((), BF16)
    for idx, (window, d) in enumerate(A_BRANCHES):
        assert window // d == nb
        nblk = seq // (d * nb)
        first, last = idx == 0, idx == len(A_BRANCHES) - 1

        group = min(DILATED_GROUP, nblk)
        nres = DILATED_GROUP // group
        tiles = [(a, j) for a in range(nres) for j in range(group)]

        def blocks(t, carry, d=d, nblk=nblk, first=first, last=last, group=group, nres=nres, tiles=tiles):
            r0 = (t // (nblk // group)) * nres
            c0 = (t % (nblk // group)) * group

            def rows(a, c):
                return pl.ds(r0 + a + c * (nb * d), nb, stride=d)

            kb = {(a, j): kf[rows(a, jnp.maximum(c0 - 1 + j, 0)), :].astype(BF16)
                  for a in range(nres) for j in range(group + 1)}
            vb = {(a, j): vf[rows(a, jnp.maximum(c0 - 1 + j, 0)), :].astype(BF16)
                  for a in range(nres) for j in range(group + 1)}
            qb = {(a, j): qf[rows(a, c0 + j), :].astype(BF16) for a, j in tiles}
            s = {(a, j): _dot_nt(
                jnp.concatenate([jnp.where(low, qb[a, j], zero), jnp.where(low, zero, qb[a, j])], axis=0),
                jnp.concatenate([kb[a, j], kb[a, j + 1]], axis=0))
                + bias[jnp.minimum(c0 + j, 1)] for a, j in tiles}
            m = {t_: jnp.max(s[t_], axis=1, keepdims=True) for t_ in tiles}
            p = {t_: jnp.exp(s[t_] - m[t_]) for t_ in tiles}
            den = {t_: jnp.sum(p[t_], axis=1, keepdims=True) for t_ in tiles}
            o2 = {(a, j): _dot(p[a, j].astype(BF16), jnp.concatenate([vb[a, j], vb[a, j + 1]], axis=0))
                  / den[a, j] for a, j in tiles}
            for a, j in tiles:
                cur = rows(a, c0 + j)
                lse2 = m[a, j] + jnp.log(den[a, j])
                o = jnp.where(low, o2[a, j][:nb], o2[a, j][nb:])
                lse = jnp.where(low, lse2[:nb], lse2[nb:])
                if not first:
                    o_prev = of[cur, :]
                    l_prev = lf[cur, :]
                    top = jnp.maximum(lse, l_prev)
                    wa = jnp.exp(lse - top)
                    wb = jnp.exp(l_prev - top)
                    tot = wa + wb
                    o = (wa * o + wb * o_prev) / tot
                    lse = top + jnp.log(tot)
                of[cur, :] = o
                if not last:
                    lf[cur, :] = lse
            return carry

        lax.fori_loop(0, seq // (nb * DILATED_GROUP), blocks, 0)
    o_ref[...] = of[...].astype(o_ref.dtype)


def dilated_attention(qk, vv, batch, seq):
    qk3 = qk.reshape(batch, seq, 2048)
    pj3 = vv.reshape(batch, seq, 2 * A_WIDTH)
    slabs = A_WIDTH // LANES
    blk = (None, seq, LANES)
    out = pl.pallas_call(
        _dilated_kernel,
        grid=(batch, slabs),
        in_specs=[pl.BlockSpec(blk, lambda b, s: (b, 0, s)),
                  pl.BlockSpec(blk, lambda b, s: (b, 0, slabs + s)),
                  pl.BlockSpec(blk, lambda b, s: (b, 0, s))],
        out_specs=pl.BlockSpec(blk, lambda b, s: (b, 0, s)),
        out_shape=jax.ShapeDtypeStruct((batch, seq, A_WIDTH), BF16),
        scratch_shapes=[pltpu.VMEM((seq, LANES), F32) for _ in range(5)]
        + [pltpu.VMEM((2, 2 * A_BLOCK, 2 * A_BLOCK), F32)],
        compiler_params=_params("parallel", "parallel"),
        name="dilated_attention",
    )(qk3, qk3, pj3)
    return out.reshape(batch * seq, A_WIDTH)


def _diff_attn_kernel(q_ref, k_ref, v_ref, lq1_ref, lk1_ref, lq2_ref, lk2_ref, g_ref, o_ref,
                      m_scr, l_scr, acc_scr, *, lam_init):
    h = pl.program_id(1)
    i = pl.program_id(2)
    tq = DIFF_TQ
    lane = lax.broadcasted_iota(jnp.int32, (1, LANES), 1)
    low = lane < HEAD_DIM
    zero = jnp.zeros((), BF16)
    q = q_ref[...]
    q2 = jnp.concatenate([jnp.where(low, q, zero), jnp.where(low, zero, q)], axis=0)
    m_scr[...] = jnp.full(m_scr.shape, NEG_INF, F32)
    l_scr[...] = jnp.zeros(l_scr.shape, F32)
    acc_scr[...] = jnp.zeros(acc_scr.shape, F32)

    def step(j, masked):
        start = pl.multiple_of(j * tq, tq)
        k = k_ref[pl.ds(start, tq), :]
        v = v_ref[pl.ds(start, tq), :]
        s = _dot_nt(q2, k)
        if masked:
            qi = lax.broadcasted_iota(jnp.int32, (2 * tq, tq), 0) % tq
            kj = lax.broadcasted_iota(jnp.int32, (2 * tq, tq), 1)
            s = jnp.where(kj <= qi, s, NEG_INF)
        m_old = m_scr[...]
        m_new = jnp.maximum(m_old, jnp.max(s, axis=1, keepdims=True))
        alpha = jnp.exp(m_old - m_new)
        p = jnp.exp(s - jnp.concatenate([m_new] * (tq // LANES), axis=1))
        l_scr[...] = alpha * l_scr[...] + jnp.sum(p, axis=1, keepdims=True)
        acc_scr[...] = alpha * acc_scr[...] + _dot(p.astype(BF16), v)
        m_scr[...] = m_new

    def body(j, carry):
        step(j, False)
        return carry

    lax.fori_loop(0, i, body, 0)
    step(i, True)

    acc = acc_scr[...] / l_scr[...]
    d1 = jnp.sum(lq1_ref[pl.ds(h, 1), :] * lk1_ref[pl.ds(h, 1), :], axis=1, keepdims=True)
    d2 = jnp.sum(lq2_ref[pl.ds(h, 1), :] * lk2_ref[pl.ds(h, 1), :], axis=1, keepdims=True)
    lam = jnp.exp(d1) - jnp.exp(d2) + lam_init
    o = acc[:tq] - lam * acc[tq:]
    o_ref[...] = (_rms_rows(o, g_ref[...]) * (1.0 - lam_init)).astype(o_ref.dtype)


def diff_attention(qk, vv, lq1, lk1, lq2, lk2, out_gain, lam_init, batch, seq):
    n = batch * seq
    qk3 = qk.reshape(batch, seq, 2048)
    pj3 = vv.reshape(batch, seq, 2 * A_WIDTH)
    qcol = 2 * A_WIDTH // LANES
    kcol = (2 * A_WIDTH + B_WIDTH) // LANES
    vcol = A_WIDTH // LANES
    small = pl.BlockSpec((B_HEADS, HEAD_DIM), lambda b, h, i: (0, 0))
    out = pl.pallas_call(
        functools.partial(_diff_attn_kernel, lam_init=lam_init),
        grid=(batch, B_HEADS, seq // DIFF_TQ),
        in_specs=[pl.BlockSpec((None, DIFF_TQ, LANES), lambda b, h, i: (b, i, qcol + h)),
                  pl.BlockSpec((None, seq, LANES), lambda b, h, i: (b, 0, kcol + h)),
                  pl.BlockSpec((None, seq, LANES), lambda b, h, i: (b, 0, vcol + h)),
                  small, small, small, small,
                  pl.BlockSpec((1, LANES), lambda b, h, i: (0, 0))],
        out_specs=pl.BlockSpec((None, DIFF_TQ, LANES), lambda b, h, i: (b, i, h)),
        out_shape=jax.ShapeDtypeStruct((batch, seq, B_WIDTH), BF16),
        scratch_shapes=[pltpu.VMEM((2 * DIFF_TQ, LANES), F32),
                        pltpu.VMEM((2 * DIFF_TQ, LANES), F32),
                        pltpu.VMEM((2 * DIFF_TQ, LANES), F32)],
        compiler_params=_params("parallel", "parallel", "parallel"),
        name="diff_attention",
    )(qk3, qk3, pj3, lq1, lk1, lq2, lk2, out_gain.reshape(1, LANES))
    return out.reshape(n, B_WIDTH)


ROUTE_E1, ROUTE_E2, ROUTE_W1, ROUTE_W2 = 0, 1, 2, 3
EXPERT_LANE0 = 8


def _outproj_router_kernel(x_ref, a_ref, b_ref, wa_ref, wb_ref, g_ref, rhi_ref, rlo_ref, rb_ref,
                           xo_ref, h_ref, rt_ref):
    y = x_ref[...] + _dot(a_ref[...], wa_ref[...]) + _dot(b_ref[...], wb_ref[...])
    xo_ref[...] = y
    hn = _rms_rows(y, g_ref[...])
    h_ref[...] = hn
    hi = hn.astype(BF16)
    lo = (hn - hi.astype(F32)).astype(BF16)
    logits = (_dot(hi, rhi_ref[...]) + _dot(lo, rhi_ref[...]) + _dot(hi, rlo_ref[...])) + rb_ref[...]

    lane = lax.broadcasted_iota(jnp.int32, logits.shape, 1)
    big = jnp.int32(LANES)
    is_group = lane < N_GROUPS
    gl = jnp.where(is_group, logits, NEG_INF)
    gmax = jnp.max(gl, axis=1, keepdims=True)
    grp = jnp.min(jnp.where(is_group & (gl == gmax), lane, big), axis=1, keepdims=True)
    g_w = 1.0 / jnp.sum(jnp.where(is_group, jnp.exp(gl - gmax), 0.0), axis=1, keepdims=True)
    first = EXPERT_LANE0 + grp * EXPERTS_PER_GROUP
    in_group = (lane >= first) & (lane < first + EXPERTS_PER_GROUP)
    el = jnp.where(in_group, logits, NEG_INF)
    v1 = jnp.max(el, axis=1, keepdims=True)
    i1 = jnp.min(jnp.where(in_group & (el == v1), lane, big), axis=1, keepdims=True)
    rest = in_group & (lane != i1)
    el2 = jnp.where(rest, logits, NEG_INF)
    v2 = jnp.max(el2, axis=1, keepdims=True)
    i2 = jnp.min(jnp.where(rest & (el2 == v2), lane, big), axis=1, keepdims=True)
    e2w = jnp.exp(v2 - v1)
    w1 = g_w / (1.0 + e2w)
    w2 = g_w * e2w / (1.0 + e2w)
    e1 = (i1 - EXPERT_LANE0).astype(F32)
    e2 = (i2 - EXPERT_LANE0).astype(F32)
    rt_ref[...] = jnp.where(lane == ROUTE_E1, e1,
                            jnp.where(lane == ROUTE_E2, e2,
                                      jnp.where(lane == ROUTE_W1, w1,
                                                jnp.where(lane == ROUTE_W2, w2, 0.0))))


def outproj_router(x, mix_a, blk_a, mix_b, blk_b, w_out_bf16, ffn_gain, r_hi, r_lo, r_bias):
    n, d = x.shape
    half = w_out_bf16.shape[0] // 2
    row = lambda i: (i, 0)
    const = lambda i: (0, 0)
    return pl.pallas_call(
        _outproj_router_kernel,
        grid=(n // ROW_TILE,),
        in_specs=[pl.BlockSpec((ROW_TILE, d), row),
                  pl.BlockSpec((ROW_TILE, half), lambda i: (i, blk_a)),
                  pl.BlockSpec((ROW_TILE, half), lambda i: (i, blk_b)),
                  pl.BlockSpec((half, d), lambda i: (0, 0)),
                  pl.BlockSpec((half, d), lambda i: (1, 0)),
                  pl.BlockSpec((1, d), const),
                  pl.BlockSpec((d, LANES), const),
                  pl.BlockSpec((d, LANES), const),
                  pl.BlockSpec((1, LANES), const)],
        out_specs=[pl.BlockSpec((ROW_TILE, d), row),
                   pl.BlockSpec((ROW_TILE, d), row),
                   pl.BlockSpec((ROW_TILE, LANES), row)],
        out_shape=[jax.ShapeDtypeStruct((n, d), F32),
                   jax.ShapeDtypeStruct((n, d), F32),
                   jax.ShapeDtypeStruct((n, LANES), F32)],
        compiler_params=_params("parallel"),
        name="outproj_router",
    )(x, mix_a, mix_b, w_out_bf16, w_out_bf16, ffn_gain.reshape(1, d), r_hi, r_lo, r_bias)


def _rank_kernel(rt_ref, tri_ref, rk_ref, cnt_ref, carry_scr):
    @pl.when(pl.program_id(0) == 0)
    def _():
        carry_scr[...] = jnp.zeros(carry_scr.shape, F32)

    rt = rt_ref[...]
    lane = lax.broadcasted_iota(jnp.int32, rt.shape, 1)
    e1 = rt[:, ROUTE_E1:ROUTE_E1 + 1].astype(jnp.int32)
    e2 = rt[:, ROUTE_E2:ROUTE_E2 + 1].astype(jnp.int32)
    oh1 = lane == e1
    oh2 = lane == e2
    both = jnp.where(oh1 | oh2, 1.0, 0.0)
    before = _dot(tri_ref[...], both.astype(BF16)) + carry_scr[0:1, :]
    r1 = jnp.sum(jnp.where(oh1, before, 0.0), axis=1, keepdims=True)
    r2 = jnp.sum(jnp.where(oh2, before, 0.0), axis=1, keepdims=True)
    rk_ref[...] = jnp.where(lane == 0, r1, jnp.where(lane == 1, r2, 0.0))
    total = carry_scr[0:1, :] + jnp.sum(both, axis=0, keepdims=True)
    carry_scr[...] = jnp.broadcast_to(total, carry_scr.shape)
    cnt_ref[...] = jnp.broadcast_to(total, cnt_ref.shape)


def rank_slots(route):
    n = route.shape[0]
    tri = jnp.tril(jnp.ones((ROW_TILE, ROW_TILE), BF16), k=-1)
    return pl.pallas_call(
        _rank_kernel,
        grid=(n // ROW_TILE,),
        in_specs=[pl.BlockSpec((ROW_TILE, LANES), lambda i: (i, 0)),
                  pl.BlockSpec((ROW_TILE, ROW_TILE), lambda i: (0, 0))],
        out_specs=[pl.BlockSpec((ROW_TILE, LANES), lambda i: (i, 0)),
                   pl.BlockSpec((8, LANES), lambda i: (0, 0))],
        out_shape=[jax.ShapeDtypeStruct((n, LANES), F32),
                   jax.ShapeDtypeStruct((8, LANES), F32)],
        scratch_shapes=[pltpu.VMEM((8, LANES), F32)],
        compiler_params=_params("arbitrary"),
        name="moe_rank",
    )(route, tri)


DISPATCH_TOKENS = 512
DMA_UNROLL = 8


def _dispatch_kernel(dest_ref, zblk_ref, nz_ref, h_ref, xs_ref, zero_scr, sem, zsem):
    base = pl.program_id(0) * DISPATCH_TOKENS

    @pl.when(pl.program_id(0) == 0)
    def _():
        zero_scr[...] = jnp.zeros(zero_scr.shape, F32)

        def zero_copy(i):
            start = pl.multiple_of(zblk_ref[i] * MOE_TILE, MOE_TILE)
            return pltpu.make_async_copy(zero_scr, xs_ref.at[pl.ds(start, MOE_TILE)], zsem)

        def zissue(i, carry):
            zero_copy(i).start()
            return carry

        def zdrain(i, carry):
            zero_copy(i).wait()
            return carry

        lax.fori_loop(0, nz_ref[0], zissue, 0)
        lax.fori_loop(0, nz_ref[0], zdrain, 0)

    def row_copy(i, k):
        return pltpu.make_async_copy(h_ref.at[pl.ds(i, 1)],
                                     xs_ref.at[pl.ds(dest_ref[TOP_K * (base + i) + k], 1)], sem)

    def issue(i, carry):
        for k in range(TOP_K):
            row_copy(i, k).start()
        return carry

    lax.fori_loop(0, DISPATCH_TOKENS, issue, 0, unroll=DMA_UNROLL)

    def drain(i, carry):
        for k in range(TOP_K):
            row_copy(i, k).wait()
        return carry

    lax.fori_loop(0, DISPATCH_TOKENS, drain, 0, unroll=DMA_UNROLL)


def dispatch_rows(h, dest, zero_blocks, n_zero, n_rows):
    n, d = h.shape
    return pl.pallas_call(
        _dispatch_kernel,
        grid_spec=pltpu.PrefetchScalarGridSpec(
            num_scalar_prefetch=3,
            grid=(n // DISPATCH_TOKENS,),
            in_specs=[pl.BlockSpec((DISPATCH_TOKENS, d), lambda i, dst, zb, nz: (i, 0))],
            out_specs=pl.BlockSpec(memory_space=pl.ANY),
            scratch_shapes=[pltpu.VMEM((MOE_TILE, d), F32),
                            pltpu.SemaphoreType.DMA(()),
                            pltpu.SemaphoreType.DMA(())]),
        out_shape=jax.ShapeDtypeStruct((n_rows, d), F32),
        compiler_params=_params("arbitrary"),
        name="moe_dispatch",
    )(dest, zero_blocks, n_zero, h)


def _expert_kernel(be_ref, nb_ref, x_ref, wg_ref, wu_ref, wd_ref, y_ref):
    del be_ref

    @pl.when(pl.program_id(0) < nb_ref[0])
    def _():
        xb = x_ref[...].astype(BF16)
        gate = _dot(xb, wg_ref[...])
        up = _dot(xb, wu_ref[...])
        hidden = (gate * _sigmoid(gate) * up).astype(BF16)
        y_ref[...] = _dot(hidden, wd_ref[...])

    @pl.when(pl.program_id(0) >= nb_ref[0])
    def _():
        y_ref[...] = jnp.zeros(y_ref.shape, F32)


def expert_ffn(xs, block_expert, n_used, w_gate, w_up, w_down):
    p, d = xs.shape
    ff = w_gate.shape[2]
    return pl.pallas_call(
        _expert_kernel,
        grid_spec=pltpu.PrefetchScalarGridSpec(
            num_scalar_prefetch=2,
            grid=(p // MOE_TILE,),
            in_specs=[pl.BlockSpec((MOE_TILE, d), lambda i, be, nb: (i, 0)),
                      pl.BlockSpec((None, d, ff), lambda i, be, nb: (be[i], 0, 0)),
                      pl.BlockSpec((None, d, ff), lambda i, be, nb: (be[i], 0, 0)),
                      pl.BlockSpec((None, ff, d), lambda i, be, nb: (be[i], 0, 0))],
            out_specs=pl.BlockSpec((MOE_TILE, d), lambda i, be, nb: (i, 0))),
        out_shape=jax.ShapeDtypeStruct((p, d), F32),
        compiler_params=_params("arbitrary"),
        name="moe_experts",
    )(block_expert, n_used, xs, w_gate, w_up, w_down)


COMBINE_TOKENS = 512


def _combine_kernel(dest_ref, x_ref, rt_ref, y_ref, o_ref, buf0, buf1, sem):
    base = pl.program_id(0) * COMBINE_TOKENS
    bufs = (buf0, buf1)

    def row_copy(i, k):
        return pltpu.make_async_copy(y_ref.at[pl.ds(dest_ref[TOP_K * (base + i) + k], 1)],
                                     bufs[k].at[pl.ds(i, 1)], sem)

    def issue(i, carry):
        for k in range(TOP_K):
            row_copy(i, k).start()
        return carry

    lax.fori_loop(0, COMBINE_TOKENS, issue, 0, unroll=DMA_UNROLL)

    def drain(i, carry):
        for k in range(TOP_K):
            row_copy(i, k).wait()
        return carry

    lax.fori_loop(0, COMBINE_TOKENS, drain, 0, unroll=DMA_UNROLL)
    rt = rt_ref[...]
    w1 = rt[:, ROUTE_W1:ROUTE_W1 + 1]
    w2 = rt[:, ROUTE_W2:ROUTE_W2 + 1]
    o_ref[...] = x_ref[...] + (w1 * buf0[...] + w2 * buf1[...])


def combine_rows(x, route, y, dest):
    n, d = x.shape
    return pl.pallas_call(
        _combine_kernel,
        grid_spec=pltpu.PrefetchScalarGridSpec(
            num_scalar_prefetch=1,
            grid=(n // COMBINE_TOKENS,),
            in_specs=[pl.BlockSpec((COMBINE_TOKENS, d), lambda i, dst: (i, 0)),
                      pl.BlockSpec((COMBINE_TOKENS, LANES), lambda i, dst: (i, 0)),
                      pl.BlockSpec(memory_space=pl.ANY)],
            out_specs=pl.BlockSpec((COMBINE_TOKENS, d), lambda i, dst: (i, 0)),
            scratch_shapes=[pltpu.VMEM((COMBINE_TOKENS, d), F32),
                            pltpu.VMEM((COMBINE_TOKENS, d), F32),
                            pltpu.SemaphoreType.DMA(())]),
        out_shape=jax.ShapeDtypeStruct((n, d), F32),
        compiler_params=_params("arbitrary"),
        name="moe_combine",
    )(dest, x, route, y)


def moe_layer(x, h, route, w_gate, w_up, w_down):
    n = x.shape[0]
    ranks, counts = rank_slots(route)
    counts = counts[0, :N_EXPERTS].astype(jnp.int32)
    padded = (counts + MOE_TILE - 1) // MOE_TILE * MOE_TILE
    pad_end = jnp.cumsum(padded)
    pad_start = pad_end - padded
    n_blocks = -(-(n * TOP_K + N_EXPERTS * (MOE_TILE - 1)) // MOE_TILE)
    eids = route[:, ROUTE_E1:ROUTE_E2 + 1].astype(jnp.int32)
    slot0 = jnp.sum(jnp.where(eids[..., None] == jnp.arange(N_EXPERTS), pad_start, 0), axis=-1)
    dest = (slot0 + ranks[:, :TOP_K].astype(jnp.int32)).reshape(-1)
    block_start = jnp.arange(n_blocks, dtype=jnp.int32) * MOE_TILE
    block_expert = jnp.minimum(
        jnp.sum(block_start[:, None] >= pad_end[None, :], axis=1), N_EXPERTS - 1).astype(jnp.int32)
    n_used = (pad_end[-1:] // MOE_TILE).astype(jnp.int32)
    tail = n_used[0] + jnp.arange(N_EXPERTS + 1, dtype=jnp.int32)
    cand = jnp.concatenate([(pad_end // MOE_TILE - 1).astype(jnp.int32), tail])
    keep = jnp.concatenate([padded > 0, tail < n_blocks])
    zero_blocks = jnp.where(keep, cand, 0)[jnp.argsort(~keep, stable=True)].astype(jnp.int32)
    n_zero = jnp.sum(keep).astype(jnp.int32).reshape(1)
    xs = dispatch_rows(h, dest, zero_blocks, n_zero, n_blocks * MOE_TILE)
    y = expert_ffn(xs, block_expert, n_used, w_gate, w_up, w_down)
    return combine_rows(x, route, y, dest)


def router_tables(w_group, b_group, w_erouter, b_erouter):
    d = w_group.shape[0]
    w_exp = jnp.transpose(w_erouter, (1, 0, 2)).reshape(d, N_EXPERTS)
    w = jnp.zeros((d, LANES), F32)
    w = w.at[:, :N_GROUPS].set(w_group.astype(F32))
    w = w.at[:, EXPERT_LANE0:EXPERT_LANE0 + N_EXPERTS].set(w_exp.astype(F32))
    bias = jnp.zeros((1, LANES), F32)
    bias = bias.at[0, :N_GROUPS].set(b_group.astype(F32))
    bias = bias.at[0, EXPERT_LANE0:EXPERT_LANE0 + N_EXPERTS].set(b_erouter.reshape(-1).astype(F32))
    hi = w.astype(BF16)
    lo = (w - hi.astype(F32)).astype(BF16)
    return hi, lo, bias


def _gates_kernel(ab_ref, a_ref, dt_ref, tri_ref, o_ref):
    x = ab_ref[...].astype(F32)
    lane = lax.broadcasted_iota(jnp.int32, x.shape, 1)
    z = x + dt_ref[...]
    softplus = jnp.maximum(z, 0.0) + jnp.log(1.0 + jnp.exp(-jnp.abs(z)))
    g = -jnp.exp(a_ref[...]) * softplus
    gcum = _split3_dot(tri_ref[...], g)
    o_ref[...] = jnp.where(lane < C_HEADS, gcum, _sigmoid(x))


def gdn_gates(proj, a_log, dt_bias):
    n = proj.shape[0]
    pad = lambda v: jnp.zeros((1, LANES), F32).at[0, :C_HEADS].set(v.astype(F32))
    r = jnp.arange(ROW_TILE)
    tri = ((r[:, None] >= r[None, :]) & (r[:, None] // CHUNK == r[None, :] // CHUNK)).astype(BF16)
    return pl.pallas_call(
        _gates_kernel,
        grid=(n // ROW_TILE,),
        in_specs=[pl.BlockSpec((ROW_TILE, LANES), lambda i: (i, 4 * C_WIDTH // LANES)),
                  pl.BlockSpec((1, LANES), lambda i: (0, 0)),
                  pl.BlockSpec((1, LANES), lambda i: (0, 0)),
                  pl.BlockSpec((ROW_TILE, ROW_TILE), lambda i: (0, 0))],
        out_specs=pl.BlockSpec((ROW_TILE, LANES), lambda i: (i, 0)),
        out_shape=jax.ShapeDtypeStruct((n, LANES), F32),
        compiler_params=_params("parallel"),
        name="gdn_gates",
    )(proj, pad(a_log), pad(dt_bias), tri)


CONV_COLS = 256
CONV_ROWS = 512
CONV_HALO = 8


def _conv_kernel(x_ref, w_ref, o_ref, pad_scr, *, normalize, scale, scale_blocks):
    seq = x_ref.shape[0]
    out_scale = jnp.where(pl.program_id(1) < scale_blocks, scale, 1.0)
    pad_scr[0:CONV_HALO, :] = jnp.zeros((CONV_HALO, CONV_COLS), F32)
    pad_scr[CONV_HALO:, :] = x_ref[...].astype(F32)
    w = w_ref[...]
    for r0 in range(0, seq, CONV_ROWS):
        acc = None
        for j in range(CONV_W):
            start = CONV_HALO + r0 - (CONV_W - 1) + j
            term = pad_scr[start:start + CONV_ROWS, :] * w[j:j + 1, :]
            acc = term if acc is None else acc + term
        y = acc * _sigmoid(acc)
        if normalize:
            for s in range(CONV_COLS // LANES):
                ys = y[:, s * LANES:(s + 1) * LANES]
                ssq = jnp.sum(ys * ys, axis=1, keepdims=True)
                o_ref[r0:r0 + CONV_ROWS, s * LANES:(s + 1) * LANES] = (
                    ys * lax.rsqrt(ssq + EPS) * out_scale).astype(o_ref.dtype)
        else:
            o_ref[r0:r0 + CONV_ROWS, :] = y.astype(o_ref.dtype)


def conv_silu(proj3, conv_w, col0, width, normalize, scale=1.0, scale_width=0):
    batch, seq, _ = proj3.shape
    cb0 = col0 // CONV_COLS
    return pl.pallas_call(
        functools.partial(_conv_kernel, normalize=normalize, scale=scale,
                          scale_blocks=scale_width // CONV_COLS),
        grid=(batch, width // CONV_COLS),
        in_specs=[pl.BlockSpec((None, seq, CONV_COLS), lambda b, c: (b, 0, cb0 + c)),
                  pl.BlockSpec((CONV_W, CONV_COLS), lambda b, c: (0, cb0 + c))],
        out_specs=pl.BlockSpec((None, seq, CONV_COLS), lambda b, c: (b, 0, c)),
        out_shape=jax.ShapeDtypeStruct((batch, seq, width), BF16),
        scratch_shapes=[pltpu.VMEM((seq + CONV_HALO, CONV_COLS), F32)],
        compiler_params=_params("parallel", "parallel"),
        name="conv_silu",
    )(proj3, conv_w)


def _head_column(gb, lane_index):
    lane = lax.broadcasted_iota(jnp.int32, gb.shape, 1)
    return jnp.sum(jnp.where(lane == lane_index, gb, 0.0), axis=1, keepdims=True)


def _chunk_decay(gcum_col):
    r = lax.broadcasted_iota(jnp.int32, (CHUNK, CHUNK), 0)
    c = lax.broadcasted_iota(jnp.int32, (CHUNK, CHUNK), 1)
    gcum = jnp.broadcast_to(gcum_col, (CHUNK, LANES))
    gcum_rows = jnp.transpose(gcum)[:CHUNK, :]
    gdiff = jnp.where(c <= r, gcum[:, :CHUNK] - gcum_rows, 0.0)
    return gcum, gdiff, r, c


def _split3_dot(a_bf16, b_f32):
    b0 = b_f32.astype(BF16)
    r1 = b_f32 - b0.astype(F32)
    b1 = r1.astype(BF16)
    b2 = (r1 - b1.astype(F32)).astype(BF16)
    return _dot(a_bf16, b0) + _dot(a_bf16, b1) + _dot(a_bf16, b2)


def _lmat_kernel(k_ref, gb_ref, l_ref):
    hg = pl.program_id(1)
    gb = gb_ref[...]
    tiles = [(ci, hh) for ci in range(GDN_ROWS // CHUNK) for hh in range(DELTA_HEADS)]
    rows = lambda ci: slice(ci * CHUNK, (ci + 1) * CHUNK)
    kk = {}
    for ci, hh in tiles:
        k = k_ref[rows(ci), hh * LANES:(hh + 1) * LANES]
        kk[ci, hh] = _dot_nt(k, k)
    for ci, hh in tiles:
        head = hg * DELTA_HEADS + hh
        gb_c = gb[rows(ci)]
        _, gdiff, r, c = _chunk_decay(_head_column(gb_c, head))
        beta = _head_column(gb_c, C_HEADS + head)
        l_ref[hh, rows(ci), :] = jnp.where(r > c, beta * kk[ci, hh] * jnp.exp(gdiff), 0.0)


def gdn_lmat(k3, gb3):
    batch, seq, _ = k3.shape
    width = DELTA_HEADS * LANES
    groups = C_HEADS // DELTA_HEADS
    return pl.pallas_call(
        _lmat_kernel,
        grid=(batch, groups, seq // GDN_ROWS),
        in_specs=[pl.BlockSpec((None, GDN_ROWS, width), lambda b, h, i: (b, i, groups + h)),
                  pl.BlockSpec((None, GDN_ROWS, LANES), lambda b, h, i: (b, i, 0))],
        out_specs=pl.BlockSpec((None, DELTA_HEADS, GDN_ROWS, CHUNK), lambda b, h, i: (b, h, i, 0)),
        out_shape=jax.ShapeDtypeStruct((batch, C_HEADS, seq, CHUNK), F32),
        compiler_params=_params("parallel", "parallel", "parallel"),
        name="gdn_lmat",
    )(k3, gb3)


def _solve_kernel(l_ref, t_ref):
    col = lax.broadcasted_iota(jnp.int32, (CHUNK, SOLVE_LANES), 0)
    for i in range(CHUNK):
        acc = jnp.where(col == i, 1.0, 0.0)
        for j in range(i):
            acc = acc - l_ref[i, j:j + 1, :] * t_ref[j]
        t_ref[i] = acc


def gdn_solve(lmat_t):
    g = lmat_t.shape[2]
    return pl.pallas_call(
        _solve_kernel,
        grid=(g // SOLVE_LANES,),
        in_specs=[pl.BlockSpec((CHUNK, CHUNK, SOLVE_LANES), lambda i: (0, 0, i))],
        out_specs=pl.BlockSpec((CHUNK, CHUNK, SOLVE_LANES), lambda i: (0, 0, i)),
        out_shape=jax.ShapeDtypeStruct(lmat_t.shape, F32),
        compiler_params=_params("parallel"),
        name="gdn_solve",
    )(lmat_t)


DELTA_HEADS = 4
DELTA_ROWS = 256


def _delta_kernel(q_ref, k_ref, v_ref, z_ref, gb_ref, t_ref, gain_ref, o_ref, *state_scrs):
    hg = pl.program_id(1)

    @pl.when(pl.program_id(2) == 0)
    def _():
        for scr in state_scrs:
            scr[...] = jnp.zeros(scr.shape, F32)

    states = [scr[...] for scr in state_scrs]
    gb = gb_ref[...]
    gain = gain_ref[...]
    n_chunks = DELTA_ROWS // CHUNK
    tiles = [(ci, hh) for ci in range(n_chunks) for hh in range(DELTA_HEADS)]
    rows = lambda ci: slice(ci * CHUNK, (ci + 1) * CHUNK)
    cols = lambda hh: slice(hh * LANES, (hh + 1) * LANES)

    pre = {}
    for ci, hh in tiles:
        head = hg * DELTA_HEADS + hh
        gb_c = gb[rows(ci)]
        gcum, gdiff, r, c = _chunk_decay(_head_column(gb_c, head))
        beta = _head_column(gb_c, C_HEADS + head)
        q = q_ref[rows(ci), cols(hh)]
        k = k_ref[rows(ci), cols(hh)]
        kf = k.astype(F32)
        eg = jnp.exp(gcum)
        g_last = gcum[CHUNK - 1:CHUNK, :]
        pre[ci, hh] = dict(
            q=q, k=k, eg=eg, e_last=jnp.exp(g_last),
            decay=jnp.where(c <= r, jnp.exp(gdiff), 0.0),
            k_dec=(kf * jnp.exp(g_last - gcum)).astype(BF16),
            rhs=jnp.concatenate([(v_ref[rows(ci), cols(hh)].astype(F32) * beta).astype(BF16),
                                 (kf * beta * eg).astype(BF16)], axis=1))
    uw = {t: _dot(t_ref[t[1], rows(t[0]), :].astype(BF16), pre[t]["rhs"]).astype(BF16)
          for t in tiles}
    a_intra = {t: (_dot_nt(pre[t]["q"], pre[t]["k"]) * pre[t]["decay"]).astype(BF16) for t in tiles}
    a_uw = {t: _dot(a_intra[t], uw[t]) for t in tiles}
    kd_uw = {t: _dot_tn(pre[t]["k_dec"], uw[t]) for t in tiles}
    lhs = {t: jnp.concatenate(
        [(pre[t]["q"].astype(F32) * pre[t]["eg"] - a_uw[t][:, LANES:]).astype(BF16),
         kd_uw[t][:, LANES:].astype(BF16)], axis=0) for t in tiles}

    for ci in range(n_chunks):
        for hh in range(DELTA_HEADS):
            t = (ci, hh)
            state = states[hh]
            prod = _dot(lhs[t], state.astype(BF16))
            o = prod[:CHUNK] + a_uw[t][:, :LANES]
            states[hh] = state * pre[t]["e_last"] - prod[CHUNK:] + kd_uw[t][:, :LANES]
            z = z_ref[rows(ci), cols(hh)].astype(F32)
            o_ref[rows(ci), cols(hh)] = (_rms_rows(o, gain) * (z * _sigmoid(z))).astype(o_ref.dtype)
    for scr, state in zip(state_scrs, states):
        scr[...] = state


def gdn_delta(qk3, v3, proj3, gb3, tinv, out_gain):
    batch, seq, _ = v3.shape
    width = DELTA_HEADS * LANES
    groups = C_HEADS // DELTA_HEADS
    zcol = 3 * C_WIDTH // width
    blk = (None, DELTA_ROWS, width)
    return pl.pallas_call(
        _delta_kernel,
        grid=(batch, groups, seq // DELTA_ROWS),
        in_specs=[pl.BlockSpec(blk, lambda b, h, i: (b, i, h)),
                  pl.BlockSpec(blk, lambda b, h, i: (b, i, groups + h)),
                  pl.BlockSpec(blk, lambda b, h, i: (b, i, h)),
                  pl.BlockSpec(blk, lambda b, h, i: (b, i, zcol + h)),
                  pl.BlockSpec((None, DELTA_ROWS, LANES), lambda b, h, i: (b, i, 0)),
                  pl.BlockSpec((None, DELTA_HEADS, DELTA_ROWS, CHUNK), lambda b, h, i: (b, h, i, 0)),
                  pl.BlockSpec((1, LANES), lambda b, h, i: (0, 0))],
        out_specs=pl.BlockSpec(blk, lambda b, h, i: (b, i, h)),
        out_shape=jax.ShapeDtypeStruct((batch, seq, C_WIDTH), BF16),
        scratch_shapes=[pltpu.VMEM((C_DK, LANES), F32) for _ in range(DELTA_HEADS)],
        compiler_params=_params("parallel", "parallel", "arbitrary"),
        name="gdn_delta",
    )(qk3, qk3, v3, proj3, gb3, tinv, out_gain.reshape(1, LANES))


def rope_tables(seq):
    half = HEAD_DIM // 2
    inv_freq = 1.0 / (ROPE_THETA ** (jnp.arange(0, HEAD_DIM, 2, dtype=F32) / HEAD_DIM))
    ang = jnp.arange(seq, dtype=F32)[:, None] * inv_freq[None, :]
    cos, sin = jnp.cos(ang), jnp.sin(ang)
    cos_t = jnp.tile(cos, (1, LANES // half))
    sin_t = jnp.tile(jnp.concatenate([-sin, sin], axis=1), (1, LANES // HEAD_DIM))
    return cos_t, sin_t


def even_layer(x, batch, seq, layer, attn_norm, w_in, a_q_gain, a_k_gain, b_q_gain, b_k_gain,
               lq1, lk1, lq2, lk2, b_out_gain, w_out, ffn_gain, router):
    lam_init = 0.8 - 0.6 * math.exp(-0.3 * layer)
    scale = HEAD_DIM ** -0.5
    tile = lambda g, s: jnp.tile(g.astype(F32) * s, A_WIDTH // HEAD_DIM)
    gains = jnp.stack([tile(a_q_gain, scale), tile(a_k_gain, 1.0),
                       tile(b_q_gain, scale), tile(b_k_gain, 1.0)])
    cos_t, sin_t = rope_tables(seq)
    qk, vv = even_in_proj(x, attn_norm, w_in.astype(BF16), gains, cos_t, sin_t, seq)
    oa = dilated_attention(qk, vv, batch, seq)
    ob = diff_attention(qk, vv, lq1.astype(F32), lk1.astype(F32), lq2.astype(F32), lk2.astype(F32),
                        b_out_gain, lam_init, batch, seq)
    return outproj_router(x, oa, 0, ob, 0, w_out.astype(BF16), ffn_gain, *router)


def odd_layer(x, batch, seq, attn_norm, w_in, conv_w, a_log, dt_bias, out_gain, w_out, ffn_gain, router):
    n = batch * seq
    w_pad = jnp.pad(w_in, ((0, 0), (0, LANES - 2 * C_HEADS))).astype(BF16)
    proj = norm_matmul(x, attn_norm, w_pad, 1408)
    gb = gdn_gates(proj, a_log, dt_bias)
    proj3 = proj.reshape(batch, seq, proj.shape[1])
    conv_w = conv_w.astype(F32)
    qk3 = conv_silu(proj3, conv_w, 0, 2 * C_WIDTH, True, C_DK ** -0.5, C_WIDTH)
    v3 = conv_silu(proj3, conv_w, 2 * C_WIDTH, C_WIDTH, False)
    gb3 = gb.reshape(batch, seq, LANES)
    lmat = gdn_lmat(qk3, gb3)
    g = batch * C_HEADS * (seq // CHUNK)
    lmat_t = jnp.transpose(lmat.reshape(g, CHUNK, CHUNK), (1, 2, 0))
    tinv = jnp.transpose(gdn_solve(lmat_t), (2, 0, 1)).reshape(batch, C_HEADS, seq, CHUNK)
    mix = gdn_delta(qk3, v3, proj3, gb3, tinv, out_gain).reshape(n, C_WIDTH)
    return outproj_router(x, mix, 0, mix, 1, w_out.astype(BF16), ffn_gain, *router)


def kernel(x, even_attn_norm, even_w_in, a_q_norm, a_k_norm, b_q_norm, b_k_norm, b_lambda_q1, b_lambda_k1, b_lambda_q2, b_lambda_k2, b_out_norm, even_w_out, odd_attn_norm, odd_w_in, odd_conv_w, odd_a_log, odd_dt_bias, odd_out_norm, odd_w_out, ffn_norm, router_group_w, router_group_b, router_expert_w, router_expert_b, expert_w_gate, expert_w_up, expert_w_down):
    batch, seq, d = x.shape
    depth = ffn_norm.shape[0]
    xf = x.reshape(batch * seq, d).astype(F32)
    for layer in range(depth):
        i = layer // 2
        router = router_tables(router_group_w[layer], router_group_b[layer],
                               router_expert_w[layer], router_expert_b[layer])
        if layer % 2 == 0:
            xf, h, route = even_layer(
                xf, batch, seq, layer, even_attn_norm[i], even_w_in[i], a_q_norm[i], a_k_norm[i],
                b_q_norm[i], b_k_norm[i], b_lambda_q1[i], b_lambda_k1[i], b_lambda_q2[i],
                b_lambda_k2[i], b_out_norm[i], even_w_out[i], ffn_norm[layer], router)
        else:
            xf, h, route = odd_layer(
                xf, batch, seq, odd_attn_norm[i], odd_w_in[i], odd_conv_w[i], odd_a_log[i],
                odd_dt_bias[i], odd_out_norm[i], odd_w_out[i], ffn_norm[layer], router)
        xf = moe_layer(xf, h, route, expert_w_gate[layer].astype(BF16),
                       expert_w_up[layer].astype(BF16), expert_w_down[layer].astype(BF16))
    return xf.reshape(batch, seq, d).astype(x.dtype)
```

```python
import functools
import math

import jax
import jax.numpy as jnp
from jax import lax
from jax.experimental import pallas as pl
from jax.experimental.pallas import tpu as pltpu

F32 = jnp.float32
BF16 = jnp.bfloat16

D_MODEL = 1024
HEAD_DIM = 64
ROPE_THETA = 10000.0
A_HEADS = 8
A_BRANCHES = ((128, 1), (512, 4), (2048, 16))
A_BLOCK = 128
B_HEADS = 4
A_WIDTH = 512
B_WIDTH = 512
EVEN_IN = 3 * A_WIDTH + 3 * B_WIDTH
C_HEADS = 8
C_DK = 128
C_WIDTH = 1024
CONV_W = 4
CHUNK = 64
N_GROUPS = 4
EXPERTS_PER_GROUP = 8
N_EXPERTS = 32
TOP_K = 2
EXPERT_FF = 512
EPS = 1e-6
NEG_INF = -1e30

LANES = 128
VMEM_LIMIT = 48 * 1024 * 1024

ROW_TILE = 512
MOE_TILE = 512
DIFF_TQ = 512
GDN_ROWS = 512
SOLVE_LANES = 128


def _params(*sem):
    return pltpu.CompilerParams(dimension_semantics=sem, vmem_limit_bytes=VMEM_LIMIT)


def _dot(a, b):
    return jnp.dot(a, b, preferred_element_type=F32)


def _dot_nt(a, b):
    return lax.dot_general(a, b, (((1,), (1,)), ((), ())), preferred_element_type=F32)


def _dot_tn(a, b):
    return lax.dot_general(a, b, (((0,), (0,)), ((), ())), preferred_element_type=F32)


def _split_dot(a_f32, b_bf16):
    hi = a_f32.astype(BF16)
    lo = (a_f32 - hi.astype(F32)).astype(BF16)
    return _dot(hi, b_bf16) + _dot(lo, b_bf16)


def _sigmoid(x):
    return 1.0 / (1.0 + jnp.exp(-x))


def _rms_rows(xf, gain):
    ms = jnp.mean(xf * xf, axis=-1, keepdims=True)
    return xf * lax.rsqrt(ms + EPS) * gain


def _norm_matmul_kernel(x_ref, g_ref, w_ref, o_ref, *, col_chunk):
    h = _rms_rows(x_ref[...], g_ref[...]).astype(BF16)
    for c in range(0, o_ref.shape[1], col_chunk):
        o_ref[:, c:c + col_chunk] = _dot(h, w_ref[:, c:c + col_chunk]).astype(o_ref.dtype)


def norm_matmul(x, gain, w_bf16, col_chunk):
    n, d = x.shape
    m = w_bf16.shape[1]
    return pl.pallas_call(
        functools.partial(_norm_matmul_kernel, col_chunk=col_chunk),
        grid=(n // ROW_TILE,),
        in_specs=[pl.BlockSpec((ROW_TILE, d), lambda i: (i, 0)),
                  pl.BlockSpec((1, d), lambda i: (0, 0)),
                  pl.BlockSpec((d, m), lambda i: (0, 0))],
        out_specs=pl.BlockSpec((ROW_TILE, m), lambda i: (i, 0)),
        out_shape=jax.ShapeDtypeStruct((n, m), BF16),
        compiler_params=_params("parallel"),
        name="norm_matmul",
    )(x, gain.reshape(1, d), w_bf16)


def _even_in_kernel(x_ref, g_ref, w_ref, hg_ref, cos_ref, sin_ref, qk_ref, v_ref):
    h = _rms_rows(x_ref[...], g_ref[...]).astype(BF16)
    lane = lax.broadcasted_iota(jnp.int32, (1, LANES), 1)
    first_half = (lane % HEAD_DIM) < (HEAD_DIM // 2)
    r = lax.broadcasted_iota(jnp.int32, (LANES, LANES), 0) // HEAD_DIM
    c = lax.broadcasted_iota(jnp.int32, (LANES, LANES), 1) // HEAD_DIM
    same_head = jnp.where(r == c, 1.0, 0.0).astype(BF16)
    cos = cos_ref[...]
    sin = sin_ref[...]
    for chunk in range(EVEN_IN // A_WIDTH):
        y = _dot(h, w_ref[:, chunk * A_WIDTH:(chunk + 1) * A_WIDTH])
        if chunk % 3 == 2:
            v_ref[:, (chunk // 3) * A_WIDTH:(chunk // 3 + 1) * A_WIDTH] = y.astype(v_ref.dtype)
            continue
        group = chunk - chunk // 3
        for s in range(A_WIDTH // LANES):
            sl = slice(s * LANES, (s + 1) * LANES)
            x = y[:, sl]
            ssq = _split_dot(x * x, same_head)
            z = x * lax.rsqrt(ssq * (1.0 / HEAD_DIM) + EPS) * hg_ref[group:group + 1, sl]
            partner = jnp.where(first_half,
                                pltpu.roll(z, LANES - HEAD_DIM // 2, axis=1),
                                pltpu.roll(z, HEAD_DIM // 2, axis=1))
            qk_ref[:, group * A_WIDTH + s * LANES:group * A_WIDTH + (s + 1) * LANES] = (
                z * cos + partner * sin).astype(qk_ref.dtype)


def even_in_proj(x, gain, w_bf16, head_gains, cos_t, sin_t, seq):
    n, d = x.shape
    tiles_per_seq = seq // ROW_TILE
    return pl.pallas_call(
        _even_in_kernel,
        grid=(n // ROW_TILE,),
        in_specs=[pl.BlockSpec((ROW_TILE, d), lambda i: (i, 0)),
                  pl.BlockSpec((1, d), lambda i: (0, 0)),
                  pl.BlockSpec((d, EVEN_IN), lambda i: (0, 0)),
                  pl.BlockSpec((4, A_WIDTH), lambda i: (0, 0)),
                  pl.BlockSpec((ROW_TILE, LANES), lambda i: (i % tiles_per_seq, 0)),
                  pl.BlockSpec((ROW_TILE, LANES), lambda i: (i % tiles_per_seq, 0))],
        out_specs=[pl.BlockSpec((ROW_TILE, 4 * A_WIDTH), lambda i: (i, 0)),
                   pl.BlockSpec((ROW_TILE, 2 * A_WIDTH), lambda i: (i, 0))],
        out_shape=[jax.ShapeDtypeStruct((n, 4 * A_WIDTH), BF16),
                   jax.ShapeDtypeStruct((n, 2 * A_WIDTH), BF16)],
        compiler_params=_params("parallel"),
        name="even_in_proj",
    )(x, gain.reshape(1, d), w_bf16, head_gains, cos_t, sin_t)


DILATED_GROUP = 4


def _dilated_kernel(q_ref, k_ref, v_ref, o_ref, qf, kf, vf, of, lf, bias):
    seq = q_ref.shape[0]
    nb = A_BLOCK
    qf[...] = q_ref[...].astype(F32)
    kf[...] = k_ref[...].astype(F32)
    vf[...] = v_ref[...].astype(F32)
    lane = lax.broadcasted_iota(jnp.int32, (1, LANES), 1)
    low = lane < HEAD_DIM
    qi = lax.broadcasted_iota(jnp.int32, (2 * nb, 2 * nb), 0) % nb
    kj = lax.broadcasted_iota(jnp.int32, (2 * nb, 2 * nb), 1)
    band = (kj >= qi) & (kj <= qi + nb)
    bias[0] = jnp.where(band & (kj >= nb), 0.0, NEG_INF)
    bias[1] = jnp.where(band, 0.0, NEG_INF)
    zero = jnp.zeros((), BF16)
    for idx, (window, d) in enumerate(A_BRANCHES):
        assert window // d == nb
        nblk = seq // (d * nb)
        first, last = idx == 0, idx == len(A_BRANCHES) - 1

        group = min(DILATED_GROUP, nblk)
        nres = DILATED_GROUP // group
        tiles = [(a, j) for a in range(nres) for j in range(group)]

        def blocks(t, carry, d=d, nblk=nblk, first=first, last=last, group=group, nres=nres, tiles=tiles):
            r0 = (t // (nblk // group)) * nres
            c0 = (t % (nblk // group)) * group

            def rows(a, c):
                return pl.ds(r0 + a + c * (nb * d), nb, stride=d)

            kb = {(a, j): kf[rows(a, jnp.maximum(c0 - 1 + j, 0)), :].astype(BF16)
                  for a in range(nres) for j in range(group + 1)}
            vb = {(a, j): vf[rows(a, jnp.maximum(c0 - 1 + j, 0)), :].astype(BF16)
                  for a in range(nres) for j in range(group + 1)}
            qb = {(a, j): qf[rows(a, c0 + j), :].astype(BF16) for a, j in tiles}
            s = {(a, j): _dot_nt(
                jnp.concatenate([jnp.where(low, qb[a, j], zero), jnp.where(low, zero, qb[a, j])], axis=0),
                jnp.concatenate([kb[a, j], kb[a, j + 1]], axis=0))
                + bias[jnp.minimum(c0 + j, 1)] for a, j in tiles}
            m = {t_: jnp.max(s[t_], axis=1, keepdims=True) for t_ in tiles}
            p = {t_: jnp.exp(s[t_] - m[t_]) for t_ in tiles}
            den = {t_: jnp.sum(p[t_], axis=1, keepdims=True) for t_ in tiles}
            o2 = {(a, j): _dot(p[a, j].astype(BF16), jnp.concatenate([vb[a, j], vb[a, j + 1]], axis=0))
                  / den[a, j] for a, j in tiles}
            for a, j in tiles:
                cur = rows(a, c0 + j)
                lse2 = m[a, j] + jnp.log(den[a, j])
                o = jnp.where(low, o2[a, j][:nb], o2[a, j][nb:])
                lse = jnp.where(low, lse2[:nb], lse2[nb:])
                if not first:
                    o_prev = of[cur, :]
                    l_prev = lf[cur, :]
                    top = jnp.maximum(lse, l_prev)
                    wa = jnp.exp(lse - top)
                    wb = jnp.exp(l_prev - top)
                    tot = wa + wb
                    o = (wa * o + wb * o_prev) / tot
                    lse = top + jnp.log(tot)
                of[cur, :] = o
                if not last:
                    lf[cur, :] = lse
            return carry

        lax.fori_loop(0, seq // (nb * DILATED_GROUP), blocks, 0)
    o_ref[...] = of[...].astype(o_ref.dtype)


def dilated_attention(qk, vv, batch, seq):
    qk3 = qk.reshape(batch, seq, 2048)
    pj3 = vv.reshape(batch, seq, 2 * A_WIDTH)
    slabs = A_WIDTH // LANES
    blk = (None, seq, LANES)
    out = pl.pallas_call(
        _dilated_kernel,
        grid=(batch, slabs),
        in_specs=[pl.BlockSpec(blk, lambda b, s: (b, 0, s)),
                  pl.BlockSpec(blk, lambda b, s: (b, 0, slabs + s)),
                  pl.BlockSpec(blk, lambda b, s: (b, 0, s))],
        out_specs=pl.BlockSpec(blk, lambda b, s: (b, 0, s)),
        out_shape=jax.ShapeDtypeStruct((batch, seq, A_WIDTH), BF16),
        scratch_shapes=[pltpu.VMEM((seq, LANES), F32) for _ in range(5)]
        + [pltpu.VMEM((2, 2 * A_BLOCK, 2 * A_BLOCK), F32)],
        compiler_params=_params("parallel", "parallel"),
        name="dilated_attention",
    )(qk3, qk3, pj3)
    return out.reshape(batch * seq, A_WIDTH)


def _diff_attn_kernel(q_ref, k_ref, v_ref, lq1_ref, lk1_ref, lq2_ref, lk2_ref, g_ref, o_ref,
                      m_scr, l_scr, acc_scr, *, lam_init):
    h = pl.program_id(1)
    i = pl.program_id(2)
    tq = DIFF_TQ
    lane = lax.broadcasted_iota(jnp.int32, (1, LANES), 1)
    low = lane < HEAD_DIM
    zero = jnp.zeros((), BF16)
    q = q_ref[...]
    q2 = jnp.concatenate([jnp.where(low, q, zero), jnp.where(low, zero, q)], axis=0)
    m_scr[...] = jnp.full(m_scr.shape, NEG_INF, F32)
    l_scr[...] = jnp.zeros(l_scr.shape, F32)
    acc_scr[...] = jnp.zeros(acc_scr.shape, F32)

    def step(j, masked):
        start = pl.multiple_of(j * tq, tq)
        k = k_ref[pl.ds(start, tq), :]
        v = v_ref[pl.ds(start, tq), :]
        s = _dot_nt(q2, k)
        if masked:
            qi = lax.broadcasted_iota(jnp.int32, (2 * tq, tq), 0) % tq
            kj = lax.broadcasted_iota(jnp.int32, (2 * tq, tq), 1)
            s = jnp.where(kj <= qi, s, NEG_INF)
        m_old = m_scr[...]
        m_new = jnp.maximum(m_old, jnp.max(s, axis=1, keepdims=True))
        alpha = jnp.exp(m_old - m_new)
        p = jnp.exp(s - jnp.concatenate([m_new] * (tq // LANES), axis=1))
        l_scr[...] = alpha * l_scr[...] + jnp.sum(p, axis=1, keepdims=True)
        acc_scr[...] = alpha * acc_scr[...] + _dot(p.astype(BF16), v)
        m_scr[...] = m_new

    def body(j, carry):
        step(j, False)
        return carry

    lax.fori_loop(0, i, body, 0)
    step(i, True)

    acc = acc_scr[...] / l_scr[...]
    d1 = jnp.sum(lq1_ref[pl.ds(h, 1), :] * lk1_ref[pl.ds(h, 1), :], axis=1, keepdims=True)
    d2 = jnp.sum(lq2_ref[pl.ds(h, 1), :] * lk2_ref[pl.ds(h, 1), :], axis=1, keepdims=True)
    lam = jnp.exp(d1) - jnp.exp(d2) + lam_init
    o = acc[:tq] - lam * acc[tq:]
    o_ref[...] = (_rms_rows(o, g_ref[...]) * (1.0 - lam_init)).astype(o_ref.dtype)


def diff_attention(qk, vv, lq1, lk1, lq2, lk2, out_gain, lam_init, batch, seq):
    n = batch * seq
    qk3 = qk.reshape(batch, seq, 2048)
    pj3 = vv.reshape(batch, seq, 2 * A_WIDTH)
    qcol = 2 * A_WIDTH // LANES
    kcol = (2 * A_WIDTH + B_WIDTH) // LANES
    vcol = A_WIDTH // LANES
    small = pl.BlockSpec((B_HEADS, HEAD_DIM), lambda b, h, i: (0, 0))
    out = pl.pallas_call(
        functools.partial(_diff_attn_kernel, lam_init=lam_init),
        grid=(batch, B_HEADS, seq // DIFF_TQ),
        in_specs=[pl.BlockSpec((None, DIFF_TQ, LANES), lambda b, h, i: (b, i, qcol + h)),
                  pl.BlockSpec((None, seq, LANES), lambda b, h, i: (b, 0, kcol + h)),
                  pl.BlockSpec((None, seq, LANES), lambda b, h, i: (b, 0, vcol + h)),
                  small, small, small, small,
                  pl.BlockSpec((1, LANES), lambda b, h, i: (0, 0))],
        out_specs=pl.BlockSpec((None, DIFF_TQ, LANES), lambda b, h, i: (b, i, h)),
        out_shape=jax.ShapeDtypeStruct((batch, seq, B_WIDTH), BF16),
        scratch_shapes=[pltpu.VMEM((2 * DIFF_TQ, LANES), F32),
                        pltpu.VMEM((2 * DIFF_TQ, LANES), F32),
                        pltpu.VMEM((2 * DIFF_TQ, LANES), F32)],
        compiler_params=_params("parallel", "parallel", "parallel"),
        name="diff_attention",
    )(qk3, qk3, pj3, lq1, lk1, lq2, lk2, out_gain.reshape(1, LANES))
    return out.reshape(n, B_WIDTH)


ROUTE_E1, ROUTE_E2, ROUTE_W1, ROUTE_W2 = 0, 1, 2, 3
EXPERT_LANE0 = 8


def _outproj_router_kernel(x_ref, a_ref, b_ref, wa_ref, wb_ref, g_ref, rhi_ref, rlo_ref, rb_ref,
                           xo_ref, h_ref, rt_ref):
    y = x_ref[...] + _dot(a_ref[...], wa_ref[...]) + _dot(b_ref[...], wb_ref[...])
    xo_ref[...] = y
    hn = _rms_rows(y, g_ref[...])
    h_ref[...] = hn
    hi = hn.astype(BF16)
    lo = (hn - hi.astype(F32)).astype(BF16)
    logits = (_dot(hi, rhi_ref[...]) + _dot(lo, rhi_ref[...]) + _dot(hi, rlo_ref[...])) + rb_ref[...]

    lane = lax.broadcasted_iota(jnp.int32, logits.shape, 1)
    big = jnp.int32(LANES)
    is_group = lane < N_GROUPS
    gl = jnp.where(is_group, logits, NEG_INF)
    gmax = jnp.max(gl, axis=1, keepdims=True)
    grp = jnp.min(jnp.where(is_group & (gl == gmax), lane, big), axis=1, keepdims=True)
    g_w = 1.0 / jnp.sum(jnp.where(is_group, jnp.exp(gl - gmax), 0.0), axis=1, keepdims=True)
    first = EXPERT_LANE0 + grp * EXPERTS_PER_GROUP
    in_group = (lane >= first) & (lane < first + EXPERTS_PER_GROUP)
    el = jnp.where(in_group, logits, NEG_INF)
    v1 = jnp.max(el, axis=1, keepdims=True)
    i1 = jnp.min(jnp.where(in_group & (el == v1), lane, big), axis=1, keepdims=True)
    rest = in_group & (lane != i1)
    el2 = jnp.where(rest, logits, NEG_INF)
    v2 = jnp.max(el2, axis=1, keepdims=True)
    i2 = jnp.min(jnp.where(rest & (el2 == v2), lane, big), axis=1, keepdims=True)
    e2w = jnp.exp(v2 - v1)
    w1 = g_w / (1.0 + e2w)
    w2 = g_w * e2w / (1.0 + e2w)
    e1 = (i1 - EXPERT_LANE0).astype(F32)
    e2 = (i2 - EXPERT_LANE0).astype(F32)
    rt_ref[...] = jnp.where(lane == ROUTE_E1, e1,
                            jnp.where(lane == ROUTE_E2, e2,
                                      jnp.where(lane == ROUTE_W1, w1,
                                                jnp.where(lane == ROUTE_W2, w2, 0.0))))


def outproj_router(x, mix_a, blk_a, mix_b, blk_b, w_out_bf16, ffn_gain, r_hi, r_lo, r_bias):
    n, d = x.shape
    half = w_out_bf16.shape[0] // 2
    row = lambda i: (i, 0)
    const = lambda i: (0, 0)
    return pl.pallas_call(
        _outproj_router_kernel,
        grid=(n // ROW_TILE,),
        in_specs=[pl.BlockSpec((ROW_TILE, d), row),
                  pl.BlockSpec((ROW_TILE, half), lambda i: (i, blk_a)),
                  pl.BlockSpec((ROW_TILE, half), lambda i: (i, blk_b)),
                  pl.BlockSpec((half, d), lambda i: (0, 0)),
                  pl.BlockSpec((half, d), lambda i: (1, 0)),
                  pl.BlockSpec((1, d), const),
                  pl.BlockSpec((d, LANES), const),
                  pl.BlockSpec((d, LANES), const),
                  pl.BlockSpec((1, LANES), const)],
        out_specs=[pl.BlockSpec((ROW_TILE, d), row),
                   pl.BlockSpec((ROW_TILE, d), row),
                   pl.BlockSpec((ROW_TILE, LANES), row)],
        out_shape=[jax.ShapeDtypeStruct((n, d), F32),
                   jax.ShapeDtypeStruct((n, d), F32),
                   jax.ShapeDtypeStruct((n, LANES), F32)],
        compiler_params=_params("parallel"),
        name="outproj_router",
    )(x, mix_a, mix_b, w_out_bf16, w_out_bf16, ffn_gain.reshape(1, d), r_hi, r_lo, r_bias)


def _rank_kernel(rt_ref, tri_ref, rk_ref, cnt_ref, carry_scr):
    @pl.when(pl.program_id(0) == 0)
    def _():
        carry_scr[...] = jnp.zeros(carry_scr.shape, F32)

    rt = rt_ref[...]
    lane = lax.broadcasted_iota(jnp.int32, rt.shape, 1)
    e1 = rt[:, ROUTE_E1:ROUTE_E1 + 1].astype(jnp.int32)
    e2 = rt[:, ROUTE_E2:ROUTE_E2 + 1].astype(jnp.int32)
    oh1 = lane == e1
    oh2 = lane == e2
    both = jnp.where(oh1 | oh2, 1.0, 0.0)
    before = _dot(tri_ref[...], both.astype(BF16)) + carry_scr[0:1, :]
    r1 = jnp.sum(jnp.where(oh1, before, 0.0), axis=1, keepdims=True)
    r2 = jnp.sum(jnp.where(oh2, before, 0.0), axis=1, keepdims=True)
    rk_ref[...] = jnp.where(lane == 0, r1, jnp.where(lane == 1, r2, 0.0))
    total = carry_scr[0:1, :] + jnp.sum(both, axis=0, keepdims=True)
    carry_scr[...] = jnp.broadcast_to(total, carry_scr.shape)
    cnt_ref[...] = jnp.broadcast_to(total, cnt_ref.shape)


def rank_slots(route):
    n = route.shape[0]
    tri = jnp.tril(jnp.ones((ROW_TILE, ROW_TILE), BF16), k=-1)
    return pl.pallas_call(
        _rank_kernel,
        grid=(n // ROW_TILE,),
        in_specs=[pl.BlockSpec((ROW_TILE, LANES), lambda i: (i, 0)),
                  pl.BlockSpec((ROW_TILE, ROW_TILE), lambda i: (0, 0))],
        out_specs=[pl.BlockSpec((ROW_TILE, LANES), lambda i: (i, 0)),
                   pl.BlockSpec((8, LANES), lambda i: (0, 0))],
        out_shape=[jax.ShapeDtypeStruct((n, LANES), F32),
                   jax.ShapeDtypeStruct((8, LANES), F32)],
        scratch_shapes=[pltpu.VMEM((8, LANES), F32)],
        compiler_params=_params("arbitrary"),
        name="moe_rank",
    )(route, tri)


DISPATCH_TOKENS = 512
DMA_UNROLL = 8


def _dispatch_kernel(dest_ref, zblk_ref, nz_ref, h_ref, xs_ref, zero_scr, sem, zsem):
    base = pl.program_id(0) * DISPATCH_TOKENS

    @pl.when(pl.program_id(0) == 0)
    def _():
        zero_scr[...] = jnp.zeros(zero_scr.shape, F32)

        def zero_copy(i):
            start = pl.multiple_of(zblk_ref[i] * MOE_TILE, MOE_TILE)
            return pltpu.make_async_copy(zero_scr, xs_ref.at[pl.ds(start, MOE_TILE)], zsem)

        def zissue(i, carry):
            zero_copy(i).start()
            return carry

        def zdrain(i, carry):
            zero_copy(i).wait()
            return carry

        lax.fori_loop(0, nz_ref[0], zissue, 0)
        lax.fori_loop(0, nz_ref[0], zdrain, 0)

    def row_copy(i, k):
        return pltpu.make_async_copy(h_ref.at[pl.ds(i, 1)],
                                     xs_ref.at[pl.ds(dest_ref[TOP_K * (base + i) + k], 1)], sem)

    def issue(i, carry):
        for k in range(TOP_K):
            row_copy(i, k).start()
        return carry

    lax.fori_loop(0, DISPATCH_TOKENS, issue, 0, unroll=DMA_UNROLL)

    for k in range(TOP_K):
        pltpu.make_async_copy(h_ref, xs_ref.at[pl.ds(0, DISPATCH_TOKENS)], sem).wait()


def dispatch_rows(h, dest, zero_blocks, n_zero, n_rows):
    n, d = h.shape
    return pl.pallas_call(
        _dispatch_kernel,
        grid_spec=pltpu.PrefetchScalarGridSpec(
            num_scalar_prefetch=3,
            grid=(n // DISPATCH_TOKENS,),
            in_specs=[pl.BlockSpec((DISPATCH_TOKENS, d), lambda i, dst, zb, nz: (i, 0))],
            out_specs=pl.BlockSpec(memory_space=pl.ANY),
            scratch_shapes=[pltpu.VMEM((MOE_TILE, d), F32),
                            pltpu.SemaphoreType.DMA(()),
                            pltpu.SemaphoreType.DMA(())]),
        out_shape=jax.ShapeDtypeStruct((n_rows, d), F32),
        compiler_params=_params("arbitrary"),
        name="moe_dispatch",
    )(dest, zero_blocks, n_zero, h)


def _expert_kernel(be_ref, nb_ref, x_ref, wg_ref, wu_ref, wd_ref, y_ref, wg_b, wu_b, wd_b):
    i = pl.program_id(0)

    @pl.when((i == 0) | (be_ref[i] != be_ref[jnp.maximum(i - 1, 0)]))
    def _():
        wg_b[...] = wg_ref[...].astype(BF16)
        wu_b[...] = wu_ref[...].astype(BF16)
        wd_b[...] = wd_ref[...].astype(BF16)

    @pl.when(i < nb_ref[0])
    def _():
        xb = x_ref[...].astype(BF16)
        gate = _dot(xb, wg_b[...])
        up = _dot(xb, wu_b[...])
        hidden = (gate * _sigmoid(gate) * up).astype(BF16)
        y_ref[...] = _dot(hidden, wd_b[...])

    @pl.when(i >= nb_ref[0])
    def _():
        y_ref[...] = jnp.zeros(y_ref.shape, F32)


def expert_ffn(xs, block_expert, n_used, layer, w_gate, w_up, w_down):
    p, d = xs.shape
    ff = w_gate.shape[3]
    return pl.pallas_call(
        _expert_kernel,
        grid_spec=pltpu.PrefetchScalarGridSpec(
            num_scalar_prefetch=2,
            grid=(p // MOE_TILE,),
            in_specs=[pl.BlockSpec((MOE_TILE, d), lambda i, be, nb: (i, 0)),
                      pl.BlockSpec((None, None, d, ff), lambda i, be, nb: (layer, be[i], 0, 0)),
                      pl.BlockSpec((None, None, d, ff), lambda i, be, nb: (layer, be[i], 0, 0)),
                      pl.BlockSpec((None, None, ff, d), lambda i, be, nb: (layer, be[i], 0, 0))],
            out_specs=pl.BlockSpec((MOE_TILE, d), lambda i, be, nb: (i, 0)),
            scratch_shapes=[pltpu.VMEM((d, ff), BF16), pltpu.VMEM((d, ff), BF16),
                            pltpu.VMEM((ff, d), BF16)]),
        out_shape=jax.ShapeDtypeStruct((p, d), F32),
        compiler_params=_params("arbitrary"),
        name="moe_experts",
    )(block_expert, n_used, xs, w_gate, w_up, w_down)


COMBINE_TOKENS = 512


def _combine_kernel(dest_ref, x_ref, rt_ref, y_ref, o_ref, buf0, buf1, sem):
    base = pl.program_id(0) * COMBINE_TOKENS
    bufs = (buf0, buf1)

    def row_copy(i, k):
        return pltpu.make_async_copy(y_ref.at[pl.ds(dest_ref[TOP_K * (base + i) + k], 1)],
                                     bufs[k].at[pl.ds(i, 1)], sem)

    def issue(i, carry):
        for k in range(TOP_K):
            row_copy(i, k).start()
        return carry

    lax.fori_loop(0, COMBINE_TOKENS, issue, 0, unroll=DMA_UNROLL)
    for k in range(TOP_K):
        pltpu.make_async_copy(y_ref.at[pl.ds(0, COMBINE_TOKENS)], bufs[k], sem).wait()
    rt = rt_ref[...]
    w1 = rt[:, ROUTE_W1:ROUTE_W1 + 1]
    w2 = rt[:, ROUTE_W2:ROUTE_W2 + 1]
    o_ref[...] = x_ref[...] + (w1 * buf0[...] + w2 * buf1[...])


def combine_rows(x, route, y, dest):
    n, d = x.shape
    return pl.pallas_call(
        _combine_kernel,
        grid_spec=pltpu.PrefetchScalarGridSpec(
            num_scalar_prefetch=1,
            grid=(n // COMBINE_TOKENS,),
            in_specs=[pl.BlockSpec((COMBINE_TOKENS, d), lambda i, dst: (i, 0)),
                      pl.BlockSpec((COMBINE_TOKENS, LANES), lambda i, dst: (i, 0)),
                      pl.BlockSpec(memory_space=pl.ANY)],
            out_specs=pl.BlockSpec((COMBINE_TOKENS, d), lambda i, dst: (i, 0)),
            scratch_shapes=[pltpu.VMEM((COMBINE_TOKENS, d), F32),
                            pltpu.VMEM((COMBINE_TOKENS, d), F32),
                            pltpu.SemaphoreType.DMA(())]),
        out_shape=jax.ShapeDtypeStruct((n, d), F32),
        compiler_params=_params("arbitrary"),
        name="moe_combine",
    )(dest, x, route, y)


def moe_layer(x, h, route, layer, w_gate, w_up, w_down):
    n = x.shape[0]
    ranks, counts = rank_slots(route)
    counts = counts[0, :N_EXPERTS].astype(jnp.int32)
    padded = (counts + MOE_TILE - 1) // MOE_TILE * MOE_TILE
    pad_end = jnp.cumsum(padded)
    pad_start = pad_end - padded
    n_blocks = -(-(n * TOP_K + N_EXPERTS * (MOE_TILE - 1)) // MOE_TILE)
    eids = route[:, ROUTE_E1:ROUTE_E2 + 1].astype(jnp.int32)
    slot0 = jnp.sum(jnp.where(eids[..., None] == jnp.arange(N_EXPERTS), pad_start, 0), axis=-1)
    dest = (slot0 + ranks[:, :TOP_K].astype(jnp.int32)).reshape(-1)
    block_start = jnp.arange(n_blocks, dtype=jnp.int32) * MOE_TILE
    block_expert = jnp.minimum(
        jnp.sum(block_start[:, None] >= pad_end[None, :], axis=1), N_EXPERTS - 1).astype(jnp.int32)
    n_used = (pad_end[-1:] // MOE_TILE).astype(jnp.int32)
    tail = n_used[0] + jnp.arange(N_EXPERTS + 1, dtype=jnp.int32)
    cand = jnp.concatenate([(pad_end // MOE_TILE - 1).astype(jnp.int32), tail])
    keep = jnp.concatenate([padded > 0, tail < n_blocks])
    zero_blocks = jnp.where(keep, cand, 0)[jnp.argsort(~keep, stable=True)].astype(jnp.int32)
    n_zero = jnp.sum(keep).astype(jnp.int32).reshape(1)
    xs = dispatch_rows(h, dest, zero_blocks, n_zero, n_blocks * MOE_TILE)
    y = expert_ffn(xs, block_expert, n_used, layer, w_gate, w_up, w_down)
    return combine_rows(x, route, y, dest)


def router_tables(w_group, b_group, w_erouter, b_erouter):
    d = w_group.shape[0]
    w_exp = jnp.transpose(w_erouter, (1, 0, 2)).reshape(d, N_EXPERTS)
    w = jnp.zeros((d, LANES), F32)
    w = w.at[:, :N_GROUPS].set(w_group.astype(F32))
    w = w.at[:, EXPERT_LANE0:EXPERT_LANE0 + N_EXPERTS].set(w_exp.astype(F32))
    bias = jnp.zeros((1, LANES), F32)
    bias = bias.at[0, :N_GROUPS].set(b_group.astype(F32))
    bias = bias.at[0, EXPERT_LANE0:EXPERT_LANE0 + N_EXPERTS].set(b_erouter.reshape(-1).astype(F32))
    hi = w.astype(BF16)
    lo = (w - hi.astype(F32)).astype(BF16)
    return hi, lo, bias


def _gates_kernel(ab_ref, a_ref, dt_ref, tri_ref, o_ref):
    x = ab_ref[...].astype(F32)
    lane = lax.broadcasted_iota(jnp.int32, x.shape, 1)
    z = x + dt_ref[...]
    softplus = jnp.maximum(z, 0.0) + jnp.log(1.0 + jnp.exp(-jnp.abs(z)))
    g = -jnp.exp(a_ref[...]) * softplus
    gcum = _split3_dot(tri_ref[...], g)
    o_ref[...] = jnp.where(lane < C_HEADS, gcum, _sigmoid(x))


def gdn_gates(proj, a_log, dt_bias):
    n = proj.shape[0]
    pad = lambda v: jnp.zeros((1, LANES), F32).at[0, :C_HEADS].set(v.astype(F32))
    r = jnp.arange(ROW_TILE)
    tri = ((r[:, None] >= r[None, :]) & (r[:, None] // CHUNK == r[None, :] // CHUNK)).astype(BF16)
    return pl.pallas_call(
        _gates_kernel,
        grid=(n // ROW_TILE,),
        in_specs=[pl.BlockSpec((ROW_TILE, LANES), lambda i: (i, 4 * C_WIDTH // LANES)),
                  pl.BlockSpec((1, LANES), lambda i: (0, 0)),
                  pl.BlockSpec((1, LANES), lambda i: (0, 0)),
                  pl.BlockSpec((ROW_TILE, ROW_TILE), lambda i: (0, 0))],
        out_specs=pl.BlockSpec((ROW_TILE, LANES), lambda i: (i, 0)),
        out_shape=jax.ShapeDtypeStruct((n, LANES), F32),
        compiler_params=_params("parallel"),
        name="gdn_gates",
    )(proj, pad(a_log), pad(dt_bias), tri)


CONV_COLS = 256
CONV_ROWS = 512
CONV_HALO = 8


def _conv_kernel(x_ref, w_ref, o_ref, pad_scr, *, normalize, scale, scale_blocks):
    seq = x_ref.shape[0]
    out_scale = jnp.where(pl.program_id(1) < scale_blocks, scale, 1.0)
    pad_scr[0:CONV_HALO, :] = jnp.zeros((CONV_HALO, CONV_COLS), F32)
    pad_scr[CONV_HALO:, :] = x_ref[...].astype(F32)
    w = w_ref[...]
    for r0 in range(0, seq, CONV_ROWS):
        acc = None
        for j in range(CONV_W):
            start = CONV_HALO + r0 - (CONV_W - 1) + j
            term = pad_scr[start:start + CONV_ROWS, :] * w[j:j + 1, :]
            acc = term if acc is None else acc + term
        y = acc * _sigmoid(acc)
        if normalize:
            for s in range(CONV_COLS // LANES):
                ys = y[:, s * LANES:(s + 1) * LANES]
                ssq = jnp.sum(ys * ys, axis=1, keepdims=True)
                o_ref[r0:r0 + CONV_ROWS, s * LANES:(s + 1) * LANES] = (
                    ys * lax.rsqrt(ssq + EPS) * out_scale).astype(o_ref.dtype)
        else:
            o_ref[r0:r0 + CONV_ROWS, :] = y.astype(o_ref.dtype)


def conv_silu(proj3, conv_w, col0, width, normalize, scale=1.0, scale_width=0):
    batch, seq, _ = proj3.shape
    cb0 = col0 // CONV_COLS
    return pl.pallas_call(
        functools.partial(_conv_kernel, normalize=normalize, scale=scale,
                          scale_blocks=scale_width // CONV_COLS),
        grid=(batch, width // CONV_COLS),
        in_specs=[pl.BlockSpec((None, seq, CONV_COLS), lambda b, c: (b, 0, cb0 + c)),
                  pl.BlockSpec((CONV_W, CONV_COLS), lambda b, c: (0, cb0 + c))],
        out_specs=pl.BlockSpec((None, seq, CONV_COLS), lambda b, c: (b, 0, c)),
        out_shape=jax.ShapeDtypeStruct((batch, seq, width), BF16),
        scratch_shapes=[pltpu.VMEM((seq + CONV_HALO, CONV_COLS), F32)],
        compiler_params=_params("parallel", "parallel"),
        name="conv_silu",
    )(proj3, conv_w)


def _head_column(gb, lane_index):
    lane = lax.broadcasted_iota(jnp.int32, gb.shape, 1)
    return jnp.sum(jnp.where(lane == lane_index, gb, 0.0), axis=1, keepdims=True)


def _chunk_decay(gcum_col):
    r = lax.broadcasted_iota(jnp.int32, (CHUNK, CHUNK), 0)
    c = lax.broadcasted_iota(jnp.int32, (CHUNK, CHUNK), 1)
    gcum = jnp.broadcast_to(gcum_col, (CHUNK, LANES))
    gcum_rows = jnp.transpose(gcum)[:CHUNK, :]
    gdiff = jnp.where(c <= r, gcum[:, :CHUNK] - gcum_rows, 0.0)
    return gcum, gdiff, r, c


def _split3_dot(a_bf16, b_f32):
    b0 = b_f32.astype(BF16)
    r1 = b_f32 - b0.astype(F32)
    b1 = r1.astype(BF16)
    b2 = (r1 - b1.astype(F32)).astype(BF16)
    return _dot(a_bf16, b0) + _dot(a_bf16, b1) + _dot(a_bf16, b2)


def _lmat_kernel(k_ref, gb_ref, l_ref):
    hg = pl.program_id(1)
    gb = gb_ref[...]
    tiles = [(ci, hh) for ci in range(GDN_ROWS // CHUNK) for hh in range(DELTA_HEADS)]
    rows = lambda ci: slice(ci * CHUNK, (ci + 1) * CHUNK)
    kk = {}
    for ci, hh in tiles:
        k = k_ref[rows(ci), hh * LANES:(hh + 1) * LANES]
        kk[ci, hh] = _dot_nt(k, k)
    for ci, hh in tiles:
        head = hg * DELTA_HEADS + hh
        gb_c = gb[rows(ci)]
        _, gdiff, r, c = _chunk_decay(_head_column(gb_c, head))
        beta = _head_column(gb_c, C_HEADS + head)
        l_ref[hh, rows(ci), :] = jnp.where(r > c, beta * kk[ci, hh] * jnp.exp(gdiff), 0.0)


def gdn_lmat(k3, gb3):
    batch, seq, _ = k3.shape
    width = DELTA_HEADS * LANES
    groups = C_HEADS // DELTA_HEADS
    return pl.pallas_call(
        _lmat_kernel,
        grid=(batch, groups, seq // GDN_ROWS),
        in_specs=[pl.BlockSpec((None, GDN_ROWS, width), lambda b, h, i: (b, i, groups + h)),
                  pl.BlockSpec((None, GDN_ROWS, LANES), lambda b, h, i: (b, i, 0))],
        out_specs=pl.BlockSpec((None, DELTA_HEADS, GDN_ROWS, CHUNK), lambda b, h, i: (b, h, i, 0)),
        out_shape=jax.ShapeDtypeStruct((batch, C_HEADS, seq, CHUNK), F32),
        compiler_params=_params("parallel", "parallel", "parallel"),
        name="gdn_lmat",
    )(k3, gb3)


def _solve_kernel(l_ref, t_ref):
    sub = 8
    col = lax.broadcasted_iota(jnp.int32, (sub, SOLVE_LANES), 0)
    for i in range(CHUNK):
        acc = [jnp.where(col + g * sub == i, 1.0, 0.0) for g in range(CHUNK // sub)]
        for j in range(i):
            lij = l_ref[i, j:j + 1, :]
            for g in range(j // sub + 1):
                acc[g] = acc[g] - lij * t_ref[j, g * sub:(g + 1) * sub, :]
        t_ref[i] = jnp.concatenate(acc, axis=0)


def gdn_solve(lmat_t):
    g = lmat_t.shape[2]
    return pl.pallas_call(
        _solve_kernel,
        grid=(g // SOLVE_LANES,),
        in_specs=[pl.BlockSpec((CHUNK, CHUNK, SOLVE_LANES), lambda i: (0, 0, i))],
        out_specs=pl.BlockSpec((CHUNK, CHUNK, SOLVE_LANES), lambda i: (0, 0, i)),
        out_shape=jax.ShapeDtypeStruct(lmat_t.shape, F32),
        compiler_params=_params("parallel"),
        name="gdn_solve",
    )(lmat_t)


DELTA_HEADS = 4
DELTA_ROWS = 256


def _delta_kernel(q_ref, k_ref, v_ref, z_ref, gb_ref, t_ref, gain_ref, o_ref, *state_scrs):
    hg = pl.program_id(1)

    @pl.when(pl.program_id(2) == 0)
    def _():
        for scr in state_scrs:
            scr[...] = jnp.zeros(scr.shape, F32)

    states = [scr[...] for scr in state_scrs]
    gb = gb_ref[...]
    gain = gain_ref[...]
    n_chunks = DELTA_ROWS // CHUNK
    tiles = [(ci, hh) for ci in range(n_chunks) for hh in range(DELTA_HEADS)]
    rows = lambda ci: slice(ci * CHUNK, (ci + 1) * CHUNK)
    cols = lambda hh: slice(hh * LANES, (hh + 1) * LANES)

    pre = {}
    for ci, hh in tiles:
        head = hg * DELTA_HEADS + hh
        gb_c = gb[rows(ci)]
        gcum, gdiff, r, c = _chunk_decay(_head_column(gb_c, head))
        beta = _head_column(gb_c, C_HEADS + head)
        q = q_ref[rows(ci), cols(hh)]
        k = k_ref[rows(ci), cols(hh)]
        kf = k.astype(F32)
        eg = jnp.exp(gcum)
        g_last = gcum[CHUNK - 1:CHUNK, :]
        pre[ci, hh] = dict(
            q=q, k=k, eg=eg, e_last=jnp.exp(g_last),
            decay=jnp.where(c <= r, jnp.exp(gdiff), 0.0),
            k_dec=(kf * jnp.exp(g_last - gcum)).astype(BF16),
            rhs=jnp.concatenate([(v_ref[rows(ci), cols(hh)].astype(F32) * beta).astype(BF16),
                                 (kf * beta * eg).astype(BF16)], axis=1))
    uw = {t: _dot(t_ref[t[1], rows(t[0]), :].astype(BF16), pre[t]["rhs"]).astype(BF16)
          for t in tiles}
    a_intra = {t: (_dot_nt(pre[t]["q"], pre[t]["k"]) * pre[t]["decay"]).astype(BF16) for t in tiles}
    a_uw = {t: _dot(a_intra[t], uw[t]) for t in tiles}
    kd_uw = {t: _dot_tn(pre[t]["k_dec"], uw[t]) for t in tiles}
    lhs = {t: jnp.concatenate(
        [(pre[t]["q"].astype(F32) * pre[t]["eg"] - a_uw[t][:, LANES:]).astype(BF16),
         kd_uw[t][:, LANES:].astype(BF16)], axis=0) for t in tiles}

    for ci in range(n_chunks):
        for hh in range(DELTA_HEADS):
            t = (ci, hh)
            state = states[hh]
            prod = _dot(lhs[t], state.astype(BF16))
            o = prod[:CHUNK] + a_uw[t][:, :LANES]
            states[hh] = state * pre[t]["e_last"] - prod[CHUNK:] + kd_uw[t][:, :LANES]
            z = z_ref[rows(ci), cols(hh)].astype(F32)
            o_ref[rows(ci), cols(hh)] = (_rms_rows(o, gain) * (z * _sigmoid(z))).astype(o_ref.dtype)
    for scr, state in zip(state_scrs, states):
        scr[...] = state


def gdn_delta(qk3, v3, proj3, gb3, tinv, out_gain):
    batch, seq, _ = v3.shape
    width = DELTA_HEADS * LANES
    groups = C_HEADS // DELTA_HEADS
    zcol = 3 * C_WIDTH // width
    blk = (None, DELTA_ROWS, width)
    return pl.pallas_call(
        _delta_kernel,
        grid=(batch, groups, seq // DELTA_ROWS),
        in_specs=[pl.BlockSpec(blk, lambda b, h, i: (b, i, h)),
                  pl.BlockSpec(blk, lambda b, h, i: (b, i, groups + h)),
                  pl.BlockSpec(blk, lambda b, h, i: (b, i, h)),
                  pl.BlockSpec(blk, lambda b, h, i: (b, i, zcol + h)),
                  pl.BlockSpec((None, DELTA_ROWS, LANES), lambda b, h, i: (b, i, 0)),
                  pl.BlockSpec((None, DELTA_HEADS, DELTA_ROWS, CHUNK), lambda b, h, i: (b, h, i, 0)),
                  pl.BlockSpec((1, LANES), lambda b, h, i: (0, 0))],
        out_specs=pl.BlockSpec(blk, lambda b, h, i: (b, i, h)),
        out_shape=jax.ShapeDtypeStruct((batch, seq, C_WIDTH), BF16),
        scratch_shapes=[pltpu.VMEM((C_DK, LANES), F32) for _ in range(DELTA_HEADS)],
        compiler_params=_params("parallel", "parallel", "arbitrary"),
        name="gdn_delta",
    )(qk3, qk3, v3, proj3, gb3, tinv, out_gain.reshape(1, LANES))


def rope_tables(seq):
    half = HEAD_DIM // 2
    inv_freq = 1.0 / (ROPE_THETA ** (jnp.arange(0, HEAD_DIM, 2, dtype=F32) / HEAD_DIM))
    ang = jnp.arange(seq, dtype=F32)[:, None] * inv_freq[None, :]
    cos, sin = jnp.cos(ang), jnp.sin(ang)
    cos_t = jnp.tile(cos, (1, LANES // half))
    sin_t = jnp.tile(jnp.concatenate([-sin, sin], axis=1), (1, LANES // HEAD_DIM))
    return cos_t, sin_t


def even_layer(x, batch, seq, layer, attn_norm, w_in, a_q_gain, a_k_gain, b_q_gain, b_k_gain,
               lq1, lk1, lq2, lk2, b_out_gain, w_out, ffn_gain, router):
    lam_init = 0.8 - 0.6 * math.exp(-0.3 * layer)
    scale = HEAD_DIM ** -0.5
    tile = lambda g, s: jnp.tile(g.astype(F32) * s, A_WIDTH // HEAD_DIM)
    gains = jnp.stack([tile(a_q_gain, scale), tile(a_k_gain, 1.0),
                       tile(b_q_gain, scale), tile(b_k_gain, 1.0)])
    cos_t, sin_t = rope_tables(seq)
    qk, vv = even_in_proj(x, attn_norm, w_in.astype(BF16), gains, cos_t, sin_t, seq)
    oa = dilated_attention(qk, vv, batch, seq)
    ob = diff_attention(qk, vv, lq1.astype(F32), lk1.astype(F32), lq2.astype(F32), lk2.astype(F32),
                        b_out_gain, lam_init, batch, seq)
    return outproj_router(x, oa, 0, ob, 0, w_out.astype(BF16), ffn_gain, *router)


def odd_layer(x, batch, seq, attn_norm, w_in, conv_w, a_log, dt_bias, out_gain, w_out, ffn_gain, router):
    n = batch * seq
    w_pad = jnp.pad(w_in, ((0, 0), (0, LANES - 2 * C_HEADS))).astype(BF16)
    proj = norm_matmul(x, attn_norm, w_pad, 1408)
    gb = gdn_gates(proj, a_log, dt_bias)
    proj3 = proj.reshape(batch, seq, proj.shape[1])
    conv_w = conv_w.astype(F32)
    qk3 = conv_silu(proj3, conv_w, 0, 2 * C_WIDTH, True, C_DK ** -0.5, C_WIDTH)
    v3 = conv_silu(proj3, conv_w, 2 * C_WIDTH, C_WIDTH, False)
    gb3 = gb.reshape(batch, seq, LANES)
    lmat = gdn_lmat(qk3, gb3)
    g = batch * C_HEADS * (seq // CHUNK)
    lmat_t = jnp.transpose(lmat.reshape(g, CHUNK, CHUNK), (1, 2, 0))
    tinv = jnp.transpose(gdn_solve(lmat_t), (2, 0, 1)).reshape(batch, C_HEADS, seq, CHUNK)
    mix = gdn_delta(qk3, v3, proj3, gb3, tinv, out_gain).reshape(n, C_WIDTH)
    return outproj_router(x, mix, 0, mix, 1, w_out.astype(BF16), ffn_gain, *router)


def kernel(x, even_attn_norm, even_w_in, a_q_norm, a_k_norm, b_q_norm, b_k_norm, b_lambda_q1, b_lambda_k1, b_lambda_q2, b_lambda_k2, b_out_norm, even_w_out, odd_attn_norm, odd_w_in, odd_conv_w, odd_a_log, odd_dt_bias, odd_out_norm, odd_w_out, ffn_norm, router_group_w, router_group_b, router_expert_w, router_expert_b, expert_w_gate, expert_w_up, expert_w_down):
    batch, seq, d = x.shape
    depth = ffn_norm.shape[0]
    xf = x.reshape(batch * seq, d).astype(F32)
    for layer in range(depth):
        i = layer // 2
        router = router_tables(router_group_w[layer], router_group_b[layer],
                               router_expert_w[layer], router_expert_b[layer])
        if layer % 2 == 0:
            xf, h, route = even_layer(
                xf, batch, seq, layer, even_attn_norm[i], even_w_in[i], a_q_norm[i], a_k_norm[i],
                b_q_norm[i], b_k_norm[i], b_lambda_q1[i], b_lambda_k1[i], b_lambda_q2[i],
                b_lambda_k2[i], b_out_norm[i], even_w_out[i], ffn_norm[layer], router)
        else:
            xf, h, route = odd_layer(
                xf, batch, seq, odd_attn_norm[i], odd_w_in[i], odd_conv_w[i], odd_a_log[i],
                odd_dt_bias[i], odd_out_norm[i], odd_w_out[i], ffn_norm[layer], router)
        xf = moe_layer(xf, h, route, layer, expert_w_gate, expert_w_up, expert_w_down)
    return xf.reshape(batch, seq, d).astype(x.dtype)
```

```python
import functools
import math

import jax
import jax.numpy as jnp
from jax import lax
from jax.experimental import pallas as pl
from jax.experimental.pallas import tpu as pltpu

F32 = jnp.float32
BF16 = jnp.bfloat16

D_MODEL = 1024
HEAD_DIM = 64
ROPE_THETA = 10000.0
A_HEADS = 8
A_BRANCHES = ((128, 1), (512, 4), (2048, 16))
A_BLOCK = 128
B_HEADS = 4
A_WIDTH = 512
B_WIDTH = 512
EVEN_IN = 3 * A_WIDTH + 3 * B_WIDTH
C_HEADS = 8
C_DK = 128
C_WIDTH = 1024
CONV_W = 4
CHUNK = 64
N_GROUPS = 4
EXPERTS_PER_GROUP = 8
N_EXPERTS = 32
TOP_K = 2
EXPERT_FF = 512
EPS = 1e-6
NEG_INF = -1e30

LANES = 128
VMEM_LIMIT = 48 * 1024 * 1024

ROW_TILE = 512
MOE_TILE = 512
DIFF_TQ = 512
GDN_ROWS = 512
SOLVE_LANES = 128


def _params(*sem):
    return pltpu.CompilerParams(dimension_semantics=sem, vmem_limit_bytes=VMEM_LIMIT)


def _dot(a, b):
    return jnp.dot(a, b, preferred_element_type=F32)


def _dot_nt(a, b):
    return lax.dot_general(a, b, (((1,), (1,)), ((), ())), preferred_element_type=F32)


def _dot_tn(a, b):
    return lax.dot_general(a, b, (((0,), (0,)), ((), ())), preferred_element_type=F32)


def _split_dot(a_f32, b_bf16):
    hi = a_f32.astype(BF16)
    lo = (a_f32 - hi.astype(F32)).astype(BF16)
    return _dot(hi, b_bf16) + _dot(lo, b_bf16)


def _sigmoid(x):
    return 1.0 / (1.0 + jnp.exp(-x))


def _rms_rows(xf, gain):
    ms = jnp.mean(xf * xf, axis=-1, keepdims=True)
    return xf * lax.rsqrt(ms + EPS) * gain


def _norm_matmul_kernel(x_ref, g_ref, w_ref, o_ref, *, col_chunk):
    h = _rms_rows(x_ref[...], g_ref[...]).astype(BF16)
    for c in range(0, o_ref.shape[1], col_chunk):
        o_ref[:, c:c + col_chunk] = _dot(h, w_ref[:, c:c + col_chunk]).astype(o_ref.dtype)


def norm_matmul(x, gain, w_bf16, col_chunk):
    n, d = x.shape
    m = w_bf16.shape[1]
    return pl.pallas_call(
        functools.partial(_norm_matmul_kernel, col_chunk=col_chunk),
        grid=(n // ROW_TILE,),
        in_specs=[pl.BlockSpec((ROW_TILE, d), lambda i: (i, 0)),
                  pl.BlockSpec((1, d), lambda i: (0, 0)),
                  pl.BlockSpec((d, m), lambda i: (0, 0))],
        out_specs=pl.BlockSpec((ROW_TILE, m), lambda i: (i, 0)),
        out_shape=jax.ShapeDtypeStruct((n, m), BF16),
        compiler_params=_params("parallel"),
        name="norm_matmul",
    )(x, gain.reshape(1, d), w_bf16)


def _even_in_kernel(x_ref, g_ref, w_ref, hg_ref, cos_ref, sin_ref, qk_ref, v_ref):
    h = _rms_rows(x_ref[...], g_ref[...]).astype(BF16)
    lane = lax.broadcasted_iota(jnp.int32, (1, LANES), 1)
    first_half = (lane % HEAD_DIM) < (HEAD_DIM // 2)
    r = lax.broadcasted_iota(jnp.int32, (LANES, LANES), 0) // HEAD_DIM
    c = lax.broadcasted_iota(jnp.int32, (LANES, LANES), 1) // HEAD_DIM
    same_head = jnp.where(r == c, 1.0, 0.0).astype(BF16)
    cos = cos_ref[...]
    sin = sin_ref[...]
    for chunk in range(EVEN_IN // A_WIDTH):
        y = _dot(h, w_ref[:, chunk * A_WIDTH:(chunk + 1) * A_WIDTH])
        if chunk % 3 == 2:
            v_ref[:, (chunk // 3) * A_WIDTH:(chunk // 3 + 1) * A_WIDTH] = y.astype(v_ref.dtype)
            continue
        group = chunk - chunk // 3
        for s in range(A_WIDTH // LANES):
            sl = slice(s * LANES, (s + 1) * LANES)
            x = y[:, sl]
            ssq = _split_dot(x * x, same_head)
            z = x * lax.rsqrt(ssq * (1.0 / HEAD_DIM) + EPS) * hg_ref[group:group + 1, sl]
            partner = jnp.where(first_half,
                                pltpu.roll(z, LANES - HEAD_DIM // 2, axis=1),
                                pltpu.roll(z, HEAD_DIM // 2, axis=1))
            qk_ref[:, group * A_WIDTH + s * LANES:group * A_WIDTH + (s + 1) * LANES] = (
                z * cos + partner * sin).astype(qk_ref.dtype)


def even_in_proj(x, gain, w_bf16, head_gains, cos_t, sin_t, seq):
    n, d = x.shape
    tiles_per_seq = seq // ROW_TILE
    return pl.pallas_call(
        _even_in_kernel,
        grid=(n // ROW_TILE,),
        in_specs=[pl.BlockSpec((ROW_TILE, d), lambda i: (i, 0)),
                  pl.BlockSpec((1, d), lambda i: (0, 0)),
                  pl.BlockSpec((d, EVEN_IN), lambda i: (0, 0)),
                  pl.BlockSpec((4, A_WIDTH), lambda i: (0, 0)),
                  pl.BlockSpec((ROW_TILE, LANES), lambda i: (i % tiles_per_seq, 0)),
                  pl.BlockSpec((ROW_TILE, LANES), lambda i: (i % tiles_per_seq, 0))],
        out_specs=[pl.BlockSpec((ROW_TILE, 4 * A_WIDTH), lambda i: (i, 0)),
                   pl.BlockSpec((ROW_TILE, 2 * A_WIDTH), lambda i: (i, 0))],
        out_shape=[jax.ShapeDtypeStruct((n, 4 * A_WIDTH), BF16),
                   jax.ShapeDtypeStruct((n, 2 * A_WIDTH), BF16)],
        compiler_params=_params("parallel"),
        name="even_in_proj",
    )(x, gain.reshape(1, d), w_bf16, head_gains, cos_t, sin_t)


DILATED_GROUP = 4


def _dilated_kernel(q_ref, k_ref, v_ref, o_ref, qf, kf, vf, of, lf, bias):
    seq = q_ref.shape[0]
    nb = A_BLOCK
    qf[...] = q_ref[...].astype(F32)
    kf[...] = k_ref[...].astype(F32)
    vf[...] = v_ref[...].astype(F32)
    lane = lax.broadcasted_iota(jnp.int32, (1, LANES), 1)
    low = lane < HEAD_DIM
    qi = lax.broadcasted_iota(jnp.int32, (2 * nb, 2 * nb), 0) % nb
    kj = lax.broadcasted_iota(jnp.int32, (2 * nb, 2 * nb), 1)
    band = (kj >= qi) & (kj <= qi + nb)
    bias[0] = jnp.where(band & (kj >= nb), 0.0, NEG_INF)
    bias[1] = jnp.where(band, 0.0, NEG_INF)
    zero = jnp.zeros((), BF16)
    for idx, (window, d) in enumerate(A_BRANCHES):
        assert window // d == nb
        nblk = seq // (d * nb)
        first, last = idx == 0, idx == len(A_BRANCHES) - 1

        group = min(DILATED_GROUP, nblk)
        nres = DILATED_GROUP // group
        tiles = [(a, j) for a in range(nres) for j in range(group)]

        def blocks(t, carry, d=d, nblk=nblk, first=first, last=last, group=group, nres=nres, tiles=tiles):
            r0 = (t // (nblk // group)) * nres
            c0 = (t % (nblk // group)) * group

            def rows(a, c):
                return pl.ds(r0 + a + c * (nb * d), nb, stride=d)

            kb = {(a, j): kf[rows(a, jnp.maximum(c0 - 1 + j, 0)), :].astype(BF16)
                  for a in range(nres) for j in range(group + 1)}
            vb = {(a, j): vf[rows(a, jnp.maximum(c0 - 1 + j, 0)), :].astype(BF16)
                  for a in range(nres) for j in range(group + 1)}
            qb = {(a, j): qf[rows(a, c0 + j), :].astype(BF16) for a, j in tiles}
            s = {(a, j): _dot_nt(
                jnp.concatenate([jnp.where(low, qb[a, j], zero), jnp.where(low, zero, qb[a, j])], axis=0),
                jnp.concatenate([kb[a, j], kb[a, j + 1]], axis=0))
                + bias[jnp.minimum(c0 + j, 1)] for a, j in tiles}
            m = {t_: jnp.max(s[t_], axis=1, keepdims=True) for t_ in tiles}
            p = {t_: jnp.exp(s[t_] - m[t_]) for t_ in tiles}
            den = {t_: jnp.sum(p[t_], axis=1, keepdims=True) for t_ in tiles}
            o2 = {(a, j): _dot(p[a, j].astype(BF16), jnp.concatenate([vb[a, j], vb[a, j + 1]], axis=0))
                  / den[a, j] for a, j in tiles}
            for a, j in tiles:
                cur = rows(a, c0 + j)
                lse2 = m[a, j] + jnp.log(den[a, j])
                o = jnp.where(low, o2[a, j][:nb], o2[a, j][nb:])
                lse = jnp.where(low, lse2[:nb], lse2[nb:])
                if not first:
                    o_prev = of[cur, :]
                    l_prev = lf[cur, :]
                    top = jnp.maximum(lse, l_prev)
                    wa = jnp.exp(lse - top)
                    wb = jnp.exp(l_prev - top)
                    tot = wa + wb
                    o = (wa * o + wb * o_prev) / tot
                    lse = top + jnp.log(tot)
                of[cur, :] = o
                if not last:
                    lf[cur, :] = lse
            return carry

        lax.fori_loop(0, seq // (nb * DILATED_GROUP), blocks, 0)
    o_ref[...] = of[...].astype(o_ref.dtype)


def dilated_attention(qk, vv, batch, seq):
    qk3 = qk.reshape(batch, seq, 2048)
    pj3 = vv.reshape(batch, seq, 2 * A_WIDTH)
    slabs = A_WIDTH // LANES
    blk = (None, seq, LANES)
    out = pl.pallas_call(
        _dilated_kernel,
        grid=(batch, slabs),
        in_specs=[pl.BlockSpec(blk, lambda b, s: (b, 0, s)),
                  pl.BlockSpec(blk, lambda b, s: (b, 0, slabs + s)),
                  pl.BlockSpec(blk, lambda b, s: (b, 0, s))],
        out_specs=pl.BlockSpec(blk, lambda b, s: (b, 0, s)),
        out_shape=jax.ShapeDtypeStruct((batch, seq, A_WIDTH), BF16),
        scratch_shapes=[pltpu.VMEM((seq, LANES), F32) for _ in range(5)]
        + [pltpu.VMEM((2, 2 * A_BLOCK, 2 * A_BLOCK), F32)],
        compiler_params=_params("parallel", "parallel"),
        name="dilated_attention",
    )(qk3, qk3, pj3)
    return out.reshape(batch * seq, A_WIDTH)


def _diff_attn_kernel(q_ref, k_ref, v_ref, lq1_ref, lk1_ref, lq2_ref, lk2_ref, g_ref, o_ref,
                      m_scr, l_scr, acc_scr, *, lam_init):
    h = pl.program_id(1)
    i = pl.program_id(2)
    tq = DIFF_TQ
    lane = lax.broadcasted_iota(jnp.int32, (1, LANES), 1)
    low = lane < HEAD_DIM
    zero = jnp.zeros((), BF16)
    q = q_ref[...]
    q2 = jnp.concatenate([jnp.where(low, q, zero), jnp.where(low, zero, q)], axis=0)
    m_scr[...] = jnp.full(m_scr.shape, NEG_INF, F32)
    l_scr[...] = jnp.zeros(l_scr.shape, F32)
    acc_scr[...] = jnp.zeros(acc_scr.shape, F32)

    def step(j, masked):
        start = pl.multiple_of(j * tq, tq)
        k = k_ref[pl.ds(start, tq), :]
        v = v_ref[pl.ds(start, tq), :]
        s = _dot_nt(q2, k)
        if masked:
            qi = lax.broadcasted_iota(jnp.int32, (2 * tq, tq), 0) % tq
            kj = lax.broadcasted_iota(jnp.int32, (2 * tq, tq), 1)
            s = jnp.where(kj <= qi, s, NEG_INF)
        m_old = m_scr[...]
        m_new = jnp.maximum(m_old, jnp.max(s, axis=1, keepdims=True))
        alpha = jnp.exp(m_old - m_new)
        p = jnp.exp(s - jnp.concatenate([m_new] * (tq // LANES), axis=1))
        l_scr[...] = alpha * l_scr[...] + jnp.sum(p, axis=1, keepdims=True)
        acc_scr[...] = alpha * acc_scr[...] + _dot(p.astype(BF16), v)
        m_scr[...] = m_new

    def body(j, carry):
        step(j, False)
        return carry

    lax.fori_loop(0, i, body, 0)
    step(i, True)

    acc = acc_scr[...] / l_scr[...]
    d1 = jnp.sum(lq1_ref[pl.ds(h, 1), :] * lk1_ref[pl.ds(h, 1), :], axis=1, keepdims=True)
    d2 = jnp.sum(lq2_ref[pl.ds(h, 1), :] * lk2_ref[pl.ds(h, 1), :], axis=1, keepdims=True)
    lam = jnp.exp(d1) - jnp.exp(d2) + lam_init
    o = acc[:tq] - lam * acc[tq:]
    o_ref[...] = (_rms_rows(o, g_ref[...]) * (1.0 - lam_init)).astype(o_ref.dtype)


def diff_attention(qk, vv, lq1, lk1, lq2, lk2, out_gain, lam_init, batch, seq):
    n = batch * seq
    qk3 = qk.reshape(batch, seq, 2048)
    pj3 = vv.reshape(batch, seq, 2 * A_WIDTH)
    qcol = 2 * A_WIDTH // LANES
    kcol = (2 * A_WIDTH + B_WIDTH) // LANES
    vcol = A_WIDTH // LANES
    small = pl.BlockSpec((B_HEADS, HEAD_DIM), lambda b, h, i: (0, 0))
    out = pl.pallas_call(
        functools.partial(_diff_attn_kernel, lam_init=lam_init),
        grid=(batch, B_HEADS, seq // DIFF_TQ),
        in_specs=[pl.BlockSpec((None, DIFF_TQ, LANES), lambda b, h, i: (b, i, qcol + h)),
                  pl.BlockSpec((None, seq, LANES), lambda b, h, i: (b, 0, kcol + h)),
                  pl.BlockSpec((None, seq, LANES), lambda b, h, i: (b, 0, vcol + h)),
                  small, small, small, small,
                  pl.BlockSpec((1, LANES), lambda b, h, i: (0, 0))],
        out_specs=pl.BlockSpec((None, DIFF_TQ, LANES), lambda b, h, i: (b, i, h)),
        out_shape=jax.ShapeDtypeStruct((batch, seq, B_WIDTH), BF16),
        scratch_shapes=[pltpu.VMEM((2 * DIFF_TQ, LANES), F32),
                        pltpu.VMEM((2 * DIFF_TQ, LANES), F32),
                        pltpu.VMEM((2 * DIFF_TQ, LANES), F32)],
        compiler_params=_params("parallel", "parallel", "parallel"),
        name="diff_attention",
    )(qk3, qk3, pj3, lq1, lk1, lq2, lk2, out_gain.reshape(1, LANES))
    return out.reshape(n, B_WIDTH)


ROUTE_E1, ROUTE_E2, ROUTE_W1, ROUTE_W2 = 0, 1, 2, 3
EXPERT_LANE0 = 8


def _outproj_router_kernel(x_ref, a_ref, b_ref, wa_ref, wb_ref, g_ref, rhi_ref, rlo_ref, rb_ref,
                           xo_ref, h_ref, rt_ref):
    y = x_ref[...] + _dot(a_ref[...], wa_ref[...]) + _dot(b_ref[...], wb_ref[...])
    xo_ref[...] = y
    hn = _rms_rows(y, g_ref[...])
    h_ref[...] = hn
    hi = hn.astype(BF16)
    lo = (hn - hi.astype(F32)).astype(BF16)
    both = _dot(hi, jnp.concatenate([rhi_ref[...], rlo_ref[...]], axis=1))
    logits = (both[:, :LANES] + _dot(lo, rhi_ref[...]) + both[:, LANES:]) + rb_ref[...]

    lane = lax.broadcasted_iota(jnp.int32, logits.shape, 1)
    big = jnp.int32(LANES)
    is_group = lane < N_GROUPS
    gl = jnp.where(is_group, logits, NEG_INF)
    gmax = jnp.max(gl, axis=1, keepdims=True)
    grp = jnp.min(jnp.where(is_group & (gl == gmax), lane, big), axis=1, keepdims=True)
    g_w = 1.0 / jnp.sum(jnp.where(is_group, jnp.exp(gl - gmax), 0.0), axis=1, keepdims=True)
    first = EXPERT_LANE0 + grp * EXPERTS_PER_GROUP
    in_group = (lane >= first) & (lane < first + EXPERTS_PER_GROUP)
    el = jnp.where(in_group, logits, NEG_INF)
    v1 = jnp.max(el, axis=1, keepdims=True)
    i1 = jnp.min(jnp.where(in_group & (el == v1), lane, big), axis=1, keepdims=True)
    rest = in_group & (lane != i1)
    el2 = jnp.where(rest, logits, NEG_INF)
    v2 = jnp.max(el2, axis=1, keepdims=True)
    i2 = jnp.min(jnp.where(rest & (el2 == v2), lane, big), axis=1, keepdims=True)
    e2w = jnp.exp(v2 - v1)
    w1 = g_w / (1.0 + e2w)
    w2 = g_w * e2w / (1.0 + e2w)
    e1 = (i1 - EXPERT_LANE0).astype(F32)
    e2 = (i2 - EXPERT_LANE0).astype(F32)
    rt_ref[...] = jnp.where(lane == ROUTE_E1, e1,
                            jnp.where(lane == ROUTE_E2, e2,
                                      jnp.where(lane == ROUTE_W1, w1,
                                                jnp.where(lane == ROUTE_W2, w2, 0.0))))


def outproj_router(x, mix_a, blk_a, mix_b, blk_b, w_out_bf16, ffn_gain, r_hi, r_lo, r_bias):
    n, d = x.shape
    half = w_out_bf16.shape[0] // 2
    row = lambda i: (i, 0)
    const = lambda i: (0, 0)
    return pl.pallas_call(
        _outproj_router_kernel,
        grid=(n // ROW_TILE,),
        in_specs=[pl.BlockSpec((ROW_TILE, d), row),
                  pl.BlockSpec((ROW_TILE, half), lambda i: (i, blk_a)),
                  pl.BlockSpec((ROW_TILE, half), lambda i: (i, blk_b)),
                  pl.BlockSpec((half, d), lambda i: (0, 0)),
                  pl.BlockSpec((half, d), lambda i: (1, 0)),
                  pl.BlockSpec((1, d), const),
                  pl.BlockSpec((d, LANES), const),
                  pl.BlockSpec((d, LANES), const),
                  pl.BlockSpec((1, LANES), const)],
        out_specs=[pl.BlockSpec((ROW_TILE, d), row),
                   pl.BlockSpec((ROW_TILE, d), row),
                   pl.BlockSpec((ROW_TILE, LANES), row)],
        out_shape=[jax.ShapeDtypeStruct((n, d), F32),
                   jax.ShapeDtypeStruct((n, d), F32),
                   jax.ShapeDtypeStruct((n, LANES), F32)],
        compiler_params=_params("parallel"),
        name="outproj_router",
    )(x, mix_a, mix_b, w_out_bf16, w_out_bf16, ffn_gain.reshape(1, d), r_hi, r_lo, r_bias)


def _rank_kernel(rt_ref, tri_ref, rk_ref, cnt_ref, carry_scr):
    @pl.when(pl.program_id(0) == 0)
    def _():
        carry_scr[...] = jnp.zeros(carry_scr.shape, F32)

    rt = rt_ref[...]
    lane = lax.broadcasted_iota(jnp.int32, rt.shape, 1)
    e1 = rt[:, ROUTE_E1:ROUTE_E1 + 1].astype(jnp.int32)
    e2 = rt[:, ROUTE_E2:ROUTE_E2 + 1].astype(jnp.int32)
    oh1 = lane == e1
    oh2 = lane == e2
    both = jnp.where(oh1 | oh2, 1.0, 0.0)
    before = _dot(tri_ref[...], both.astype(BF16)) + carry_scr[0:1, :]
    r1 = jnp.sum(jnp.where(oh1, before, 0.0), axis=1, keepdims=True)
    r2 = jnp.sum(jnp.where(oh2, before, 0.0), axis=1, keepdims=True)
    rk_ref[...] = jnp.where(lane == 0, r1, jnp.where(lane == 1, r2, 0.0))
    total = carry_scr[0:1, :] + jnp.sum(both, axis=0, keepdims=True)
    carry_scr[...] = jnp.broadcast_to(total, carry_scr.shape)
    cnt_ref[...] = jnp.broadcast_to(total, cnt_ref.shape)


def rank_slots(route):
    n = route.shape[0]
    tri = jnp.tril(jnp.ones((ROW_TILE, ROW_TILE), BF16), k=-1)
    return pl.pallas_call(
        _rank_kernel,
        grid=(n // ROW_TILE,),
        in_specs=[pl.BlockSpec((ROW_TILE, LANES), lambda i: (i, 0)),
                  pl.BlockSpec((ROW_TILE, ROW_TILE), lambda i: (0, 0))],
        out_specs=[pl.BlockSpec((ROW_TILE, LANES), lambda i: (i, 0)),
                   pl.BlockSpec((8, LANES), lambda i: (0, 0))],
        out_shape=[jax.ShapeDtypeStruct((n, LANES), F32),
                   jax.ShapeDtypeStruct((8, LANES), F32)],
        scratch_shapes=[pltpu.VMEM((8, LANES), F32)],
        compiler_params=_params("arbitrary"),
        name="moe_rank",
    )(route, tri)


DISPATCH_TOKENS = 512
DMA_UNROLL = 8


def _dispatch_kernel(dest_ref, zblk_ref, nz_ref, h_ref, xs_ref, zero_scr, sem, zsem):
    base = pl.program_id(0) * DISPATCH_TOKENS

    @pl.when(pl.program_id(0) == 0)
    def _():
        zero_scr[...] = jnp.zeros(zero_scr.shape, F32)

        def zero_copy(i):
            start = pl.multiple_of(zblk_ref[i] * MOE_TILE, MOE_TILE)
            return pltpu.make_async_copy(zero_scr, xs_ref.at[pl.ds(start, MOE_TILE)], zsem)

        def zissue(i, carry):
            zero_copy(i).start()
            return carry

        def zdrain(i, carry):
            zero_copy(i).wait()
            return carry

        lax.fori_loop(0, nz_ref[0], zissue, 0)
        lax.fori_loop(0, nz_ref[0], zdrain, 0)

    def row_copy(i, k):
        return pltpu.make_async_copy(h_ref.at[pl.ds(i, 1)],
                                     xs_ref.at[pl.ds(dest_ref[TOP_K * (base + i) + k], 1)], sem)

    def issue(i, carry):
        for k in range(TOP_K):
            row_copy(i, k).start(priority=k % 2)
        return carry

    lax.fori_loop(0, DISPATCH_TOKENS, issue, 0, unroll=DMA_UNROLL)

    for k in range(TOP_K):
        pltpu.make_async_copy(h_ref, xs_ref.at[pl.ds(0, DISPATCH_TOKENS)], sem).wait()


def dispatch_rows(h, dest, zero_blocks, n_zero, n_rows):
    n, d = h.shape
    return pl.pallas_call(
        _dispatch_kernel,
        grid_spec=pltpu.PrefetchScalarGridSpec(
            num_scalar_prefetch=3,
            grid=(n // DISPATCH_TOKENS,),
            in_specs=[pl.BlockSpec((DISPATCH_TOKENS, d), lambda i, dst, zb, nz: (i, 0))],
            out_specs=pl.BlockSpec(memory_space=pl.ANY),
            scratch_shapes=[pltpu.VMEM((MOE_TILE, d), F32),
                            pltpu.SemaphoreType.DMA(()),
                            pltpu.SemaphoreType.DMA(())]),
        out_shape=jax.ShapeDtypeStruct((n_rows, d), F32),
        compiler_params=_params("arbitrary"),
        name="moe_dispatch",
    )(dest, zero_blocks, n_zero, h)


def _expert_kernel(be_ref, nb_ref, x_ref, wg_ref, wu_ref, wd_ref, y_ref, wg_b, wu_b, wd_b):
    i = pl.program_id(0)

    @pl.when((i == 0) | (be_ref[i] != be_ref[jnp.maximum(i - 1, 0)]))
    def _():
        wg_b[...] = wg_ref[...].astype(BF16)
        wu_b[...] = wu_ref[...].astype(BF16)
        wd_b[...] = wd_ref[...].astype(BF16)

    @pl.when(i < nb_ref[0])
    def _():
        xb = x_ref[...].astype(BF16)
        gate = _dot(xb, wg_b[...])
        up = _dot(xb, wu_b[...])
        hidden = (gate * _sigmoid(gate) * up).astype(BF16)
        y_ref[...] = _dot(hidden, wd_b[...])

    @pl.when(i >= nb_ref[0])
    def _():
        y_ref[...] = jnp.zeros(y_ref.shape, F32)


def expert_ffn(xs, block_expert, n_used, layer, w_gate, w_up, w_down):
    p, d = xs.shape
    ff = w_gate.shape[3]
    return pl.pallas_call(
        _expert_kernel,
        grid_spec=pltpu.PrefetchScalarGridSpec(
            num_scalar_prefetch=2,
            grid=(p // MOE_TILE,),
            in_specs=[pl.BlockSpec((MOE_TILE, d), lambda i, be, nb: (i, 0)),
                      pl.BlockSpec((None, None, d, ff), lambda i, be, nb: (layer, be[i], 0, 0)),
                      pl.BlockSpec((None, None, d, ff), lambda i, be, nb: (layer, be[i], 0, 0)),
                      pl.BlockSpec((None, None, ff, d), lambda i, be, nb: (layer, be[i], 0, 0))],
            out_specs=pl.BlockSpec((MOE_TILE, d), lambda i, be, nb: (i, 0)),
            scratch_shapes=[pltpu.VMEM((d, ff), BF16), pltpu.VMEM((d, ff), BF16),
                            pltpu.VMEM((ff, d), BF16)]),
        out_shape=jax.ShapeDtypeStruct((p, d), F32),
        compiler_params=_params("arbitrary"),
        name="moe_experts",
    )(block_expert, n_used, xs, w_gate, w_up, w_down)


COMBINE_TOKENS = 512


def _combine_kernel(dest_ref, x_ref, rt_ref, y_ref, o_ref, buf0, buf1, sem):
    base = pl.program_id(0) * COMBINE_TOKENS
    bufs = (buf0, buf1)

    def row_copy(i, k):
        return pltpu.make_async_copy(y_ref.at[pl.ds(dest_ref[TOP_K * (base + i) + k], 1)],
                                     bufs[k].at[pl.ds(i, 1)], sem)

    def issue(i, carry):
        for k in range(TOP_K):
            row_copy(i, k).start(priority=k % 2)
        return carry

    lax.fori_loop(0, COMBINE_TOKENS, issue, 0, unroll=DMA_UNROLL)
    for k in range(TOP_K):
        pltpu.make_async_copy(y_ref.at[pl.ds(0, COMBINE_TOKENS)], bufs[k], sem).wait()
    rt = rt_ref[...]
    w1 = rt[:, ROUTE_W1:ROUTE_W1 + 1]
    w2 = rt[:, ROUTE_W2:ROUTE_W2 + 1]
    o_ref[...] = x_ref[...] + (w1 * buf0[...] + w2 * buf1[...])


def combine_rows(x, route, y, dest):
    n, d = x.shape
    return pl.pallas_call(
        _combine_kernel,
        grid_spec=pltpu.PrefetchScalarGridSpec(
            num_scalar_prefetch=1,
            grid=(n // COMBINE_TOKENS,),
            in_specs=[pl.BlockSpec((COMBINE_TOKENS, d), lambda i, dst: (i, 0)),
                      pl.BlockSpec((COMBINE_TOKENS, LANES), lambda i, dst: (i, 0)),
                      pl.BlockSpec(memory_space=pl.ANY)],
            out_specs=pl.BlockSpec((COMBINE_TOKENS, d), lambda i, dst: (i, 0)),
            scratch_shapes=[pltpu.VMEM((COMBINE_TOKENS, d), F32),
                            pltpu.VMEM((COMBINE_TOKENS, d), F32),
                            pltpu.SemaphoreType.DMA(())]),
        out_shape=jax.ShapeDtypeStruct((n, d), F32),
        compiler_params=_params("arbitrary"),
        name="moe_combine",
    )(dest, x, route, y)


def moe_layer(x, h, route, layer, w_gate, w_up, w_down):
    n = x.shape[0]
    ranks, counts = rank_slots(route)
    counts = counts[0, :N_EXPERTS].astype(jnp.int32)
    padded = (counts + MOE_TILE - 1) // MOE_TILE * MOE_TILE
    pad_end = jnp.cumsum(padded)
    pad_start = pad_end - padded
    n_blocks = -(-(n * TOP_K + N_EXPERTS * (MOE_TILE - 1)) // MOE_TILE)
    eids = route[:, ROUTE_E1:ROUTE_E2 + 1].astype(jnp.int32)
    slot0 = jnp.sum(jnp.where(eids[..., None] == jnp.arange(N_EXPERTS), pad_start, 0), axis=-1)
    dest = (slot0 + ranks[:, :TOP_K].astype(jnp.int32)).reshape(-1)
    block_start = jnp.arange(n_blocks, dtype=jnp.int32) * MOE_TILE
    block_expert = jnp.minimum(
        jnp.sum(block_start[:, None] >= pad_end[None, :], axis=1), N_EXPERTS - 1).astype(jnp.int32)
    n_used = (pad_end[-1:] // MOE_TILE).astype(jnp.int32)
    tail = n_used[0] + jnp.arange(N_EXPERTS + 1, dtype=jnp.int32)
    cand = jnp.concatenate([(pad_end // MOE_TILE - 1).astype(jnp.int32), tail])
    keep = jnp.concatenate([padded > 0, tail < n_blocks])
    zero_blocks = jnp.where(keep, cand, 0)[jnp.argsort(~keep, stable=True)].astype(jnp.int32)
    n_zero = jnp.sum(keep).astype(jnp.int32).reshape(1)
    xs = dispatch_rows(h, dest, zero_blocks, n_zero, n_blocks * MOE_TILE)
    y = expert_ffn(xs, block_expert, n_used, layer, w_gate, w_up, w_down)
    return combine_rows(x, route, y, dest)


def router_tables(w_group, b_group, w_erouter, b_erouter):
    d = w_group.shape[0]
    w_exp = jnp.transpose(w_erouter, (1, 0, 2)).reshape(d, N_EXPERTS)
    w = jnp.zeros((d, LANES), F32)
    w = w.at[:, :N_GROUPS].set(w_group.astype(F32))
    w = w.at[:, EXPERT_LANE0:EXPERT_LANE0 + N_EXPERTS].set(w_exp.astype(F32))
    bias = jnp.zeros((1, LANES), F32)
    bias = bias.at[0, :N_GROUPS].set(b_group.astype(F32))
    bias = bias.at[0, EXPERT_LANE0:EXPERT_LANE0 + N_EXPERTS].set(b_erouter.reshape(-1).astype(F32))
    hi = w.astype(BF16)
    lo = (w - hi.astype(F32)).astype(BF16)
    return hi, lo, bias


def _gates_kernel(ab_ref, a_ref, dt_ref, tri_ref, o_ref):
    x = ab_ref[...].astype(F32)
    lane = lax.broadcasted_iota(jnp.int32, x.shape, 1)
    z = x + dt_ref[...]
    softplus = jnp.maximum(z, 0.0) + jnp.log(1.0 + jnp.exp(-jnp.abs(z)))
    g = -jnp.exp(a_ref[...]) * softplus
    gcum = _split3_dot(tri_ref[...], g)
    o_ref[...] = jnp.where(lane < C_HEADS, gcum, _sigmoid(x))


def gdn_gates(proj, a_log, dt_bias):
    n = proj.shape[0]
    pad = lambda v: jnp.zeros((1, LANES), F32).at[0, :C_HEADS].set(v.astype(F32))
    r = jnp.arange(ROW_TILE)
    tri = ((r[:, None] >= r[None, :]) & (r[:, None] // CHUNK == r[None, :] // CHUNK)).astype(BF16)
    return pl.pallas_call(
        _gates_kernel,
        grid=(n // ROW_TILE,),
        in_specs=[pl.BlockSpec((ROW_TILE, LANES), lambda i: (i, 4 * C_WIDTH // LANES)),
                  pl.BlockSpec((1, LANES), lambda i: (0, 0)),
                  pl.BlockSpec((1, LANES), lambda i: (0, 0)),
                  pl.BlockSpec((ROW_TILE, ROW_TILE), lambda i: (0, 0))],
        out_specs=pl.BlockSpec((ROW_TILE, LANES), lambda i: (i, 0)),
        out_shape=jax.ShapeDtypeStruct((n, LANES), F32),
        compiler_params=_params("parallel"),
        name="gdn_gates",
    )(proj, pad(a_log), pad(dt_bias), tri)


CONV_COLS = 256
CONV_ROWS = 512
CONV_HALO = 8


def _conv_kernel(x_ref, w_ref, o_ref, pad_scr, *, normalize, scale, scale_blocks):
    seq = x_ref.shape[0]
    out_scale = jnp.where(pl.program_id(1) < scale_blocks, scale, 1.0)
    pad_scr[0:CONV_HALO, :] = jnp.zeros((CONV_HALO, CONV_COLS), F32)
    pad_scr[CONV_HALO:, :] = x_ref[...].astype(F32)
    w = w_ref[...]
    for r0 in range(0, seq, CONV_ROWS):
        acc = None
        for j in range(CONV_W):
            start = CONV_HALO + r0 - (CONV_W - 1) + j
            term = pad_scr[start:start + CONV_ROWS, :] * w[j:j + 1, :]
            acc = term if acc is None else acc + term
        y = acc * _sigmoid(acc)
        if normalize:
            for s in range(CONV_COLS // LANES):
                ys = y[:, s * LANES:(s + 1) * LANES]
                ssq = jnp.sum(ys * ys, axis=1, keepdims=True)
                o_ref[r0:r0 + CONV_ROWS, s * LANES:(s + 1) * LANES] = (
                    ys * lax.rsqrt(ssq + EPS) * out_scale).astype(o_ref.dtype)
        else:
            o_ref[r0:r0 + CONV_ROWS, :] = y.astype(o_ref.dtype)


def conv_silu(proj3, conv_w, col0, width, normalize, scale=1.0, scale_width=0):
    batch, seq, _ = proj3.shape
    cb0 = col0 // CONV_COLS
    return pl.pallas_call(
        functools.partial(_conv_kernel, normalize=normalize, scale=scale,
                          scale_blocks=scale_width // CONV_COLS),
        grid=(batch, width // CONV_COLS),
        in_specs=[pl.BlockSpec((None, seq, CONV_COLS), lambda b, c: (b, 0, cb0 + c)),
                  pl.BlockSpec((CONV_W, CONV_COLS), lambda b, c: (0, cb0 + c))],
        out_specs=pl.BlockSpec((None, seq, CONV_COLS), lambda b, c: (b, 0, c)),
        out_shape=jax.ShapeDtypeStruct((batch, seq, width), BF16),
        scratch_shapes=[pltpu.VMEM((seq + CONV_HALO, CONV_COLS), F32)],
        compiler_params=_params("parallel", "parallel"),
        name="conv_silu",
    )(proj3, conv_w)


def _head_column(gb, lane_index):
    lane = lax.broadcasted_iota(jnp.int32, gb.shape, 1)
    return jnp.sum(jnp.where(lane == lane_index, gb, 0.0), axis=1, keepdims=True)


def _chunk_decay(gcum_col):
    r = lax.broadcasted_iota(jnp.int32, (CHUNK, CHUNK), 0)
    c = lax.broadcasted_iota(jnp.int32, (CHUNK, CHUNK), 1)
    gcum = jnp.broadcast_to(gcum_col, (CHUNK, LANES))
    gcum_rows = jnp.transpose(gcum)[:CHUNK, :]
    gdiff = jnp.where(c <= r, gcum[:, :CHUNK] - gcum_rows, 0.0)
    return gcum, gdiff, r, c


def _split3_dot(a_bf16, b_f32):
    b0 = b_f32.astype(BF16)
    r1 = b_f32 - b0.astype(F32)
    b1 = r1.astype(BF16)
    b2 = (r1 - b1.astype(F32)).astype(BF16)
    return _dot(a_bf16, b0) + _dot(a_bf16, b1) + _dot(a_bf16, b2)


def _lmat_kernel(k_ref, gb_ref, l_ref):
    hg = pl.program_id(1)
    gb = gb_ref[...]
    tiles = [(ci, hh) for ci in range(GDN_ROWS // CHUNK) for hh in range(DELTA_HEADS)]
    rows = lambda ci: slice(ci * CHUNK, (ci + 1) * CHUNK)
    kk = {}
    for ci, hh in tiles:
        k = k_ref[rows(ci), hh * LANES:(hh + 1) * LANES]
        kk[ci, hh] = _dot_nt(k, k)
    for ci, hh in tiles:
        head = hg * DELTA_HEADS + hh
        gb_c = gb[rows(ci)]
        _, gdiff, r, c = _chunk_decay(_head_column(gb_c, head))
        beta = _head_column(gb_c, C_HEADS + head)
        l_ref[hh, rows(ci), :] = jnp.where(r > c, beta * kk[ci, hh] * jnp.exp(gdiff), 0.0)


def gdn_lmat(k3, gb3):
    batch, seq, _ = k3.shape
    width = DELTA_HEADS * LANES
    groups = C_HEADS // DELTA_HEADS
    return pl.pallas_call(
        _lmat_kernel,
        grid=(batch, groups, seq // GDN_ROWS),
        in_specs=[pl.BlockSpec((None, GDN_ROWS, width), lambda b, h, i: (b, i, groups + h)),
                  pl.BlockSpec((None, GDN_ROWS, LANES), lambda b, h, i: (b, i, 0))],
        out_specs=pl.BlockSpec((None, DELTA_HEADS, GDN_ROWS, CHUNK), lambda b, h, i: (b, h, i, 0)),
        out_shape=jax.ShapeDtypeStruct((batch, C_HEADS, seq, CHUNK), F32),
        compiler_params=_params("parallel", "parallel", "parallel"),
        name="gdn_lmat",
    )(k3, gb3)


def _solve_kernel(l_ref, t_ref):
    sub = 8
    col = lax.broadcasted_iota(jnp.int32, (sub, SOLVE_LANES), 0)
    for i in range(CHUNK):
        acc = [jnp.where(col + g * sub == i, 1.0, 0.0) for g in range(CHUNK // sub)]
        for j in range(i):
            lij = l_ref[i, j:j + 1, :]
            for g in range(j // sub + 1):
                acc[g] = acc[g] - lij * t_ref[j, g * sub:(g + 1) * sub, :]
        t_ref[i] = jnp.concatenate(acc, axis=0)


def gdn_solve(lmat_t):
    g = lmat_t.shape[2]
    return pl.pallas_call(
        _solve_kernel,
        grid=(g // SOLVE_LANES,),
        in_specs=[pl.BlockSpec((CHUNK, CHUNK, SOLVE_LANES), lambda i: (0, 0, i))],
        out_specs=pl.BlockSpec((CHUNK, CHUNK, SOLVE_LANES), lambda i: (0, 0, i)),
        out_shape=jax.ShapeDtypeStruct(lmat_t.shape, F32),
        compiler_params=_params("parallel"),
        name="gdn_solve",
    )(lmat_t)


DELTA_HEADS = 4
DELTA_ROWS = 256


def _delta_kernel(q_ref, k_ref, v_ref, z_ref, gb_ref, t_ref, gain_ref, o_ref, *state_scrs):
    hg = pl.program_id(1)

    @pl.when(pl.program_id(2) == 0)
    def _():
        for scr in state_scrs:
            scr[...] = jnp.zeros(scr.shape, F32)

    states = [scr[...] for scr in state_scrs]
    gb = gb_ref[...]
    gain = gain_ref[...]
    n_chunks = DELTA_ROWS // CHUNK
    tiles = [(ci, hh) for ci in range(n_chunks) for hh in range(DELTA_HEADS)]
    rows = lambda ci: slice(ci * CHUNK, (ci + 1) * CHUNK)
    cols = lambda hh: slice(hh * LANES, (hh + 1) * LANES)

    pre = {}
    for ci, hh in tiles:
        head = hg * DELTA_HEADS + hh
        gb_c = gb[rows(ci)]
        gcum, gdiff, r, c = _chunk_decay(_head_column(gb_c, head))
        beta = _head_column(gb_c, C_HEADS + head)
        q = q_ref[rows(ci), cols(hh)]
        k = k_ref[rows(ci), cols(hh)]
        kf = k.astype(F32)
        eg = jnp.exp(gcum)
        g_last = gcum[CHUNK - 1:CHUNK, :]
        pre[ci, hh] = dict(
            q=q, k=k, eg=eg, e_last=jnp.exp(g_last),
            decay=jnp.where(c <= r, jnp.exp(gdiff), 0.0),
            k_dec=(kf * jnp.exp(g_last - gcum)).astype(BF16),
            rhs=jnp.concatenate([(v_ref[rows(ci), cols(hh)].astype(F32) * beta).astype(BF16),
                                 (kf * beta * eg).astype(BF16)], axis=1))
    uw = {t: _dot(t_ref[t[1], rows(t[0]), :].astype(BF16), pre[t]["rhs"]).astype(BF16)
          for t in tiles}
    a_intra = {t: (_dot_nt(pre[t]["q"], pre[t]["k"]) * pre[t]["decay"]).astype(BF16) for t in tiles}
    a_uw = {t: _dot(a_intra[t], uw[t]) for t in tiles}
    kd_uw = {t: _dot_tn(pre[t]["k_dec"], uw[t]) for t in tiles}
    lhs = {t: jnp.concatenate(
        [(pre[t]["q"].astype(F32) * pre[t]["eg"] - a_uw[t][:, LANES:]).astype(BF16),
         kd_uw[t][:, LANES:].astype(BF16)], axis=0) for t in tiles}

    for ci in range(n_chunks):
        for hh in range(DELTA_HEADS):
            t = (ci, hh)
            state = states[hh]
            prod = _dot(lhs[t], state.astype(BF16))
            o = prod[:CHUNK] + a_uw[t][:, :LANES]
            states[hh] = state * pre[t]["e_last"] - prod[CHUNK:] + kd_uw[t][:, :LANES]
            z = z_ref[rows(ci), cols(hh)].astype(F32)
            o_ref[rows(ci), cols(hh)] = (_rms_rows(o, gain) * (z * _sigmoid(z))).astype(o_ref.dtype)
    for scr, state in zip(state_scrs, states):
        scr[...] = state


def gdn_delta(qk3, v3, proj3, gb3, tinv, out_gain):
    batch, seq, _ = v3.shape
    width = DELTA_HEADS * LANES
    groups = C_HEADS // DELTA_HEADS
    zcol = 3 * C_WIDTH // width
    blk = (None, DELTA_ROWS, width)
    return pl.pallas_call(
        _delta_kernel,
        grid=(batch, groups, seq // DELTA_ROWS),
        in_specs=[pl.BlockSpec(blk, lambda b, h, i: (b, i, h)),
                  pl.BlockSpec(blk, lambda b, h, i: (b, i, groups + h)),
                  pl.BlockSpec(blk, lambda b, h, i: (b, i, h)),
                  pl.BlockSpec(blk, lambda b, h, i: (b, i, zcol + h)),
                  pl.BlockSpec((None, DELTA_ROWS, LANES), lambda b, h, i: (b, i, 0)),
                  pl.BlockSpec((None, DELTA_HEADS, DELTA_ROWS, CHUNK), lambda b, h, i: (b, h, i, 0)),
                  pl.BlockSpec((1, LANES), lambda b, h, i: (0, 0))],
        out_specs=pl.BlockSpec(blk, lambda b, h, i: (b, i, h)),
        out_shape=jax.ShapeDtypeStruct((batch, seq, C_WIDTH), BF16),
        scratch_shapes=[pltpu.VMEM((C_DK, LANES), F32) for _ in range(DELTA_HEADS)],
        compiler_params=_params("parallel", "parallel", "arbitrary"),
        name="gdn_delta",
    )(qk3, qk3, v3, proj3, gb3, tinv, out_gain.reshape(1, LANES))


def rope_tables(seq):
    half = HEAD_DIM // 2
    inv_freq = 1.0 / (ROPE_THETA ** (jnp.arange(0, HEAD_DIM, 2, dtype=F32) / HEAD_DIM))
    ang = jnp.arange(seq, dtype=F32)[:, None] * inv_freq[None, :]
    cos, sin = jnp.cos(ang), jnp.sin(ang)
    cos_t = jnp.tile(cos, (1, LANES // half))
    sin_t = jnp.tile(jnp.concatenate([-sin, sin], axis=1), (1, LANES // HEAD_DIM))
    return cos_t, sin_t


def even_layer(x, batch, seq, layer, attn_norm, w_in, a_q_gain, a_k_gain, b_q_gain, b_k_gain,
               lq1, lk1, lq2, lk2, b_out_gain, w_out, ffn_gain, router):
    lam_init = 0.8 - 0.6 * math.exp(-0.3 * layer)
    scale = HEAD_DIM ** -0.5
    tile = lambda g, s: jnp.tile(g.astype(F32) * s, A_WIDTH // HEAD_DIM)
    gains = jnp.stack([tile(a_q_gain, scale), tile(a_k_gain, 1.0),
                       tile(b_q_gain, scale), tile(b_k_gain, 1.0)])
    cos_t, sin_t = rope_tables(seq)
    qk, vv = even_in_proj(x, attn_norm, w_in.astype(BF16), gains, cos_t, sin_t, seq)
    oa = dilated_attention(qk, vv, batch, seq)
    ob = diff_attention(qk, vv, lq1.astype(F32), lk1.astype(F32), lq2.astype(F32), lk2.astype(F32),
                        b_out_gain, lam_init, batch, seq)
    return outproj_router(x, oa, 0, ob, 0, w_out.astype(BF16), ffn_gain, *router)


def odd_layer(x, batch, seq, attn_norm, w_in, conv_w, a_log, dt_bias, out_gain, w_out, ffn_gain, router):
    n = batch * seq
    w_pad = jnp.pad(w_in, ((0, 0), (0, LANES - 2 * C_HEADS))).astype(BF16)
    proj = norm_matmul(x, attn_norm, w_pad, 1408)
    gb = gdn_gates(proj, a_log, dt_bias)
    proj3 = proj.reshape(batch, seq, proj.shape[1])
    conv_w = conv_w.astype(F32)
    qk3 = conv_silu(proj3, conv_w, 0, 2 * C_WIDTH, True, C_DK ** -0.5, C_WIDTH)
    v3 = conv_silu(proj3, conv_w, 2 * C_WIDTH, C_WIDTH, False)
    gb3 = gb.reshape(batch, seq, LANES)
    lmat = gdn_lmat(qk3, gb3)
    g = batch * C_HEADS * (seq // CHUNK)
    lmat_t = jnp.transpose(lmat.reshape(g, CHUNK, CHUNK), (1, 2, 0))
    tinv = jnp.transpose(gdn_solve(lmat_t), (2, 0, 1)).reshape(batch, C_HEADS, seq, CHUNK)
    mix = gdn_delta(qk3, v3, proj3, gb3, tinv, out_gain).reshape(n, C_WIDTH)
    return outproj_router(x, mix, 0, mix, 1, w_out.astype(BF16), ffn_gain, *router)


def kernel(x, even_attn_norm, even_w_in, a_q_norm, a_k_norm, b_q_norm, b_k_norm, b_lambda_q1, b_lambda_k1, b_lambda_q2, b_lambda_k2, b_out_norm, even_w_out, odd_attn_norm, odd_w_in, odd_conv_w, odd_a_log, odd_dt_bias, odd_out_norm, odd_w_out, ffn_norm, router_group_w, router_group_b, router_expert_w, router_expert_b, expert_w_gate, expert_w_up, expert_w_down):
    batch, seq, d = x.shape
    depth = ffn_norm.shape[0]
    xf = x.reshape(batch * seq, d).astype(F32)
    for layer in range(depth):
        i = layer // 2
        router = router_tables(router_group_w[layer], router_group_b[layer],
                               router_expert_w[layer], router_expert_b[layer])
        if layer % 2 == 0:
            xf, h, route = even_layer(
                xf, batch, seq, layer, even_attn_norm[i], even_w_in[i], a_q_norm[i], a_k_norm[i],
                b_q_norm[i], b_k_norm[i], b_lambda_q1[i], b_lambda_k1[i], b_lambda_q2[i],
                b_lambda_k2[i], b_out_norm[i], even_w_out[i], ffn_norm[layer], router)
        else:
            xf, h, route = odd_layer(
                xf, batch, seq, odd_attn_norm[i], odd_w_in[i], odd_conv_w[i], odd_a_log[i],
                odd_dt_bias[i], odd_out_norm[i], odd_w_out[i], ffn_norm[layer], router)
        xf = moe_layer(xf, h, route, layer, expert_w_gate, expert_w_up, expert_w_down)
    return xf.reshape(batch, seq, d).astype(x.dtype)
```

```python
import functools
import math

import jax
import jax.numpy as jnp
from jax import lax
from jax.experimental import pallas as pl
from jax.experimental.pallas import tpu as pltpu

F32 = jnp.float32
BF16 = jnp.bfloat16

D_MODEL = 1024
HEAD_DIM = 64
ROPE_THETA = 10000.0
A_HEADS = 8
A_BRANCHES = ((128, 1), (512, 4), (2048, 16))
A_BLOCK = 128
B_HEADS = 4
A_WIDTH = 512
B_WIDTH = 512
EVEN_IN = 3 * A_WIDTH + 3 * B_WIDTH
C_HEADS = 8
C_DK = 128
C_WIDTH = 1024
CONV_W = 4
CHUNK = 64
N_GROUPS = 4
EXPERTS_PER_GROUP = 8
N_EXPERTS = 32
TOP_K = 2
EXPERT_FF = 512
EPS = 1e-6
NEG_INF = -1e30

LANES = 128
VMEM_LIMIT = 48 * 1024 * 1024

ROW_TILE = 512
MOE_TILE = 512
DIFF_TQ = 512
GDN_ROWS = 512
SOLVE_LANES = 128


def _params(*sem):
    return pltpu.CompilerParams(dimension_semantics=sem, vmem_limit_bytes=VMEM_LIMIT)


def _dot(a, b):
    return jnp.dot(a, b, preferred_element_type=F32)


def _dot_nt(a, b):
    return lax.dot_general(a, b, (((1,), (1,)), ((), ())), preferred_element_type=F32)


def _dot_tn(a, b):
    return lax.dot_general(a, b, (((0,), (0,)), ((), ())), preferred_element_type=F32)


def _split_dot(a_f32, b_bf16):
    hi = a_f32.astype(BF16)
    lo = (a_f32 - hi.astype(F32)).astype(BF16)
    return _dot(hi, b_bf16) + _dot(lo, b_bf16)


def _sigmoid(x):
    return 1.0 / (1.0 + jnp.exp(-x))


def _rms_rows(xf, gain):
    ms = jnp.mean(xf * xf, axis=-1, keepdims=True)
    return xf * lax.rsqrt(ms + EPS) * gain


def _norm_matmul_kernel(x_ref, g_ref, w_ref, o_ref, *, col_chunk):
    h = _rms_rows(x_ref[...], g_ref[...]).astype(BF16)
    for c in range(0, o_ref.shape[1], col_chunk):
        o_ref[:, c:c + col_chunk] = _dot(h, w_ref[:, c:c + col_chunk]).astype(o_ref.dtype)


def norm_matmul(x, gain, w_bf16, col_chunk):
    n, d = x.shape
    m = w_bf16.shape[1]
    return pl.pallas_call(
        functools.partial(_norm_matmul_kernel, col_chunk=col_chunk),
        grid=(n // ROW_TILE,),
        in_specs=[pl.BlockSpec((ROW_TILE, d), lambda i: (i, 0)),
                  pl.BlockSpec((1, d), lambda i: (0, 0)),
                  pl.BlockSpec((d, m), lambda i: (0, 0))],
        out_specs=pl.BlockSpec((ROW_TILE, m), lambda i: (i, 0)),
        out_shape=jax.ShapeDtypeStruct((n, m), BF16),
        compiler_params=_params("parallel"),
        name="norm_matmul",
    )(x, gain.reshape(1, d), w_bf16)


def _even_in_kernel(x_ref, g_ref, w_ref, hg_ref, cos_ref, sin_ref, qk_ref, v_ref):
    h = _rms_rows(x_ref[...], g_ref[...]).astype(BF16)
    lane = lax.broadcasted_iota(jnp.int32, (1, LANES), 1)
    first_half = (lane % HEAD_DIM) < (HEAD_DIM // 2)
    r = lax.broadcasted_iota(jnp.int32, (LANES, LANES), 0) // HEAD_DIM
    c = lax.broadcasted_iota(jnp.int32, (LANES, LANES), 1) // HEAD_DIM
    same_head = jnp.where(r == c, 1.0, 0.0).astype(BF16)
    cos = cos_ref[...]
    sin = sin_ref[...]
    for chunk in range(EVEN_IN // A_WIDTH):
        y = _dot(h, w_ref[:, chunk * A_WIDTH:(chunk + 1) * A_WIDTH])
        if chunk % 3 == 2:
            v_ref[:, (chunk // 3) * A_WIDTH:(chunk // 3 + 1) * A_WIDTH] = y.astype(v_ref.dtype)
            continue
        group = chunk - chunk // 3
        for s in range(A_WIDTH // LANES):
            sl = slice(s * LANES, (s + 1) * LANES)
            x = y[:, sl]
            ssq = _split_dot(x * x, same_head)
            z = x * lax.rsqrt(ssq * (1.0 / HEAD_DIM) + EPS) * hg_ref[group:group + 1, sl]
            partner = jnp.where(first_half,
                                pltpu.roll(z, LANES - HEAD_DIM // 2, axis=1),
                                pltpu.roll(z, HEAD_DIM // 2, axis=1))
            qk_ref[:, group * A_WIDTH + s * LANES:group * A_WIDTH + (s + 1) * LANES] = (
                z * cos + partner * sin).astype(qk_ref.dtype)


def even_in_proj(x, gain, w_bf16, head_gains, cos_t, sin_t, seq):
    n, d = x.shape
    tiles_per_seq = seq // ROW_TILE
    return pl.pallas_call(
        _even_in_kernel,
        grid=(n // ROW_TILE,),
        in_specs=[pl.BlockSpec((ROW_TILE, d), lambda i: (i, 0)),
                  pl.BlockSpec((1, d), lambda i: (0, 0)),
                  pl.BlockSpec((d, EVEN_IN), lambda i: (0, 0)),
                  pl.BlockSpec((4, A_WIDTH), lambda i: (0, 0)),
                  pl.BlockSpec((ROW_TILE, LANES), lambda i: (i % tiles_per_seq, 0)),
                  pl.BlockSpec((ROW_TILE, LANES), lambda i: (i % tiles_per_seq, 0))],
        out_specs=[pl.BlockSpec((ROW_TILE, 4 * A_WIDTH), lambda i: (i, 0)),
                   pl.BlockSpec((ROW_TILE, 2 * A_WIDTH), lambda i: (i, 0))],
        out_shape=[jax.ShapeDtypeStruct((n, 4 * A_WIDTH), BF16),
                   jax.ShapeDtypeStruct((n, 2 * A_WIDTH), BF16)],
        compiler_params=_params("parallel"),
        name="even_in_proj",
    )(x, gain.reshape(1, d), w_bf16, head_gains, cos_t, sin_t)


DILATED_GROUP = 4


def _dilated_kernel(q_ref, k_ref, v_ref, o_ref, qf, kf, vf, of, lf, bias):
    seq = q_ref.shape[0]
    nb = A_BLOCK
    qf[...] = q_ref[...].astype(F32)
    kf[...] = k_ref[...].astype(F32)
    vf[...] = v_ref[...].astype(F32)
    lane = lax.broadcasted_iota(jnp.int32, (1, LANES), 1)
    low = lane < HEAD_DIM
    qi = lax.broadcasted_iota(jnp.int32, (2 * nb, 2 * nb), 0) % nb
    kj = lax.broadcasted_iota(jnp.int32, (2 * nb, 2 * nb), 1)
    band = (kj >= qi) & (kj <= qi + nb)
    bias[0] = jnp.where(band & (kj >= nb), 0.0, NEG_INF)
    bias[1] = jnp.where(band, 0.0, NEG_INF)
    zero = jnp.zeros((), BF16)
    for idx, (window, d) in enumerate(A_BRANCHES):
        assert window // d == nb
        nblk = seq // (d * nb)
        first, last = idx == 0, idx == len(A_BRANCHES) - 1

        group = min(DILATED_GROUP, nblk)
        nres = DILATED_GROUP // group
        tiles = [(a, j) for a in range(nres) for j in range(group)]

        def blocks(t, carry, d=d, nblk=nblk, first=first, last=last, group=group, nres=nres, tiles=tiles):
            r0 = (t // (nblk // group)) * nres
            c0 = (t % (nblk // group)) * group

            def rows(a, c):
                return pl.ds(r0 + a + c * (nb * d), nb, stride=d)

            kb = {(a, j): kf[rows(a, jnp.maximum(c0 - 1 + j, 0)), :].astype(BF16)
                  for a in range(nres) for j in range(group + 1)}
            vb = {(a, j): vf[rows(a, jnp.maximum(c0 - 1 + j, 0)), :].astype(BF16)
                  for a in range(nres) for j in range(group + 1)}
            qb = {(a, j): qf[rows(a, c0 + j), :].astype(BF16) for a, j in tiles}
            s = {(a, j): _dot_nt(
                jnp.concatenate([jnp.where(low, qb[a, j], zero), jnp.where(low, zero, qb[a, j])], axis=0),
                jnp.concatenate([kb[a, j], kb[a, j + 1]], axis=0))
                + bias[jnp.minimum(c0 + j, 1)] for a, j in tiles}
            m = {t_: jnp.max(s[t_], axis=1, keepdims=True) for t_ in tiles}
            p = {t_: jnp.exp(s[t_] - m[t_]) for t_ in tiles}
            den = {t_: jnp.sum(p[t_], axis=1, keepdims=True) for t_ in tiles}
            o2 = {(a, j): _dot(p[a, j].astype(BF16), jnp.concatenate([vb[a, j], vb[a, j + 1]], axis=0))
                  / den[a, j] for a, j in tiles}
            for a, j in tiles:
                cur = rows(a, c0 + j)
                lse2 = m[a, j] + jnp.log(den[a, j])
                o = jnp.where(low, o2[a, j][:nb], o2[a, j][nb:])
                lse = jnp.where(low, lse2[:nb], lse2[nb:])
                if not first:
                    o_prev = of[cur, :]
                    l_prev = lf[cur, :]
                    top = jnp.maximum(lse, l_prev)
                    wa = jnp.exp(lse - top)
                    wb = jnp.exp(l_prev - top)
                    tot = wa + wb
                    o = (wa * o + wb * o_prev) / tot
                    lse = top + jnp.log(tot)
                of[cur, :] = o
                if not last:
                    lf[cur, :] = lse
            return carry

        lax.fori_loop(0, seq // (nb * DILATED_GROUP), blocks, 0)
    o_ref[...] = of[...].astype(o_ref.dtype)


def dilated_attention(qk, vv, batch, seq):
    qk3 = qk.reshape(batch, seq, 2048)
    pj3 = vv.reshape(batch, seq, 2 * A_WIDTH)
    slabs = A_WIDTH // LANES
    blk = (None, seq, LANES)
    out = pl.pallas_call(
        _dilated_kernel,
        grid=(batch, slabs),
        in_specs=[pl.BlockSpec(blk, lambda b, s: (b, 0, s)),
                  pl.BlockSpec(blk, lambda b, s: (b, 0, slabs + s)),
                  pl.BlockSpec(blk, lambda b, s: (b, 0, s))],
        out_specs=pl.BlockSpec(blk, lambda b, s: (b, 0, s)),
        out_shape=jax.ShapeDtypeStruct((batch, seq, A_WIDTH), BF16),
        scratch_shapes=[pltpu.VMEM((seq, LANES), F32) for _ in range(5)]
        + [pltpu.VMEM((2, 2 * A_BLOCK, 2 * A_BLOCK), F32)],
        compiler_params=_params("parallel", "parallel"),
        name="dilated_attention",
    )(qk3, qk3, pj3)
    return out.reshape(batch * seq, A_WIDTH)


def _diff_attn_kernel(q_ref, k_ref, v_ref, lq1_ref, lk1_ref, lq2_ref, lk2_ref, g_ref, o_ref,
                      m_scr, l_scr, acc_scr, *, lam_init):
    h = pl.program_id(1)
    i = pl.program_id(2)
    tq = DIFF_TQ
    lane = lax.broadcasted_iota(jnp.int32, (1, LANES), 1)
    low = lane < HEAD_DIM
    zero = jnp.zeros((), BF16)
    q = q_ref[...]
    q2 = jnp.concatenate([jnp.where(low, q, zero), jnp.where(low, zero, q)], axis=0)
    m_scr[...] = jnp.full(m_scr.shape, NEG_INF, F32)
    l_scr[...] = jnp.zeros(l_scr.shape, F32)
    acc_scr[...] = jnp.zeros(acc_scr.shape, F32)

    def step(j, masked):
        start = pl.multiple_of(j * tq, tq)
        k = k_ref[pl.ds(start, tq), :]
        v = v_ref[pl.ds(start, tq), :]
        s = _dot_nt(q2, k)
        if masked:
            qi = lax.broadcasted_iota(jnp.int32, (2 * tq, tq), 0) % tq
            kj = lax.broadcasted_iota(jnp.int32, (2 * tq, tq), 1)
            s = jnp.where(kj <= qi, s, NEG_INF)
        m_old = m_scr[...]
        m_new = jnp.maximum(m_old, jnp.max(s, axis=1, keepdims=True))
        alpha = jnp.exp(m_old - m_new)
        p = jnp.exp(s - jnp.concatenate([m_new] * (tq // LANES), axis=1))
        l_scr[...] = alpha * l_scr[...] + jnp.sum(p, axis=1, keepdims=True)
        acc_scr[...] = alpha * acc_scr[...] + _dot(p.astype(BF16), v)
        m_scr[...] = m_new

    def body(j, carry):
        step(j, False)
        return carry

    lax.fori_loop(0, i, body, 0)
    step(i, True)

    acc = acc_scr[...] / l_scr[...]
    d1 = jnp.sum(lq1_ref[pl.ds(h, 1), :] * lk1_ref[pl.ds(h, 1), :], axis=1, keepdims=True)
    d2 = jnp.sum(lq2_ref[pl.ds(h, 1), :] * lk2_ref[pl.ds(h, 1), :], axis=1, keepdims=True)
    lam = jnp.exp(d1) - jnp.exp(d2) + lam_init
    o = acc[:tq] - lam * acc[tq:]
    o_ref[...] = (_rms_rows(o, g_ref[...]) * (1.0 - lam_init)).astype(o_ref.dtype)


def diff_attention(qk, vv, lq1, lk1, lq2, lk2, out_gain, lam_init, batch, seq):
    n = batch * seq
    qk3 = qk.reshape(batch, seq, 2048)
    pj3 = vv.reshape(batch, seq, 2 * A_WIDTH)
    qcol = 2 * A_WIDTH // LANES
    kcol = (2 * A_WIDTH + B_WIDTH) // LANES
    vcol = A_WIDTH // LANES
    small = pl.BlockSpec((B_HEADS, HEAD_DIM), lambda b, h, i: (0, 0))
    out = pl.pallas_call(
        functools.partial(_diff_attn_kernel, lam_init=lam_init),
        grid=(batch, B_HEADS, seq // DIFF_TQ),
        in_specs=[pl.BlockSpec((None, DIFF_TQ, LANES), lambda b, h, i: (b, i, qcol + h)),
                  pl.BlockSpec((None, seq, LANES), lambda b, h, i: (b, 0, kcol + h)),
                  pl.BlockSpec((None, seq, LANES), lambda b, h, i: (b, 0, vcol + h)),
                  small, small, small, small,
                  pl.BlockSpec((1, LANES), lambda b, h, i: (0, 0))],
        out_specs=pl.BlockSpec((None, DIFF_TQ, LANES), lambda b, h, i: (b, i, h)),
        out_shape=jax.ShapeDtypeStruct((batch, seq, B_WIDTH), BF16),
        scratch_shapes=[pltpu.VMEM((2 * DIFF_TQ, LANES), F32),
                        pltpu.VMEM((2 * DIFF_TQ, LANES), F32),
                        pltpu.VMEM((2 * DIFF_TQ, LANES), F32)],
        compiler_params=_params("parallel", "parallel", "parallel"),
        name="diff_attention",
    )(qk3, qk3, pj3, lq1, lk1, lq2, lk2, out_gain.reshape(1, LANES))
    return out.reshape(n, B_WIDTH)


ROUTE_E1, ROUTE_E2, ROUTE_W1, ROUTE_W2 = 0, 1, 2, 3
EXPERT_LANE0 = 8


TOKEN_ROWS = D_MODEL // LANES


def _store_token_tiles(ref, rows):
    t = rows.shape[0]
    for s in range(TOKEN_ROWS):
        ref[pl.ds(s, t, stride=TOKEN_ROWS), :] = rows[:, s * LANES:(s + 1) * LANES]


def _load_token_tiles(ref, t):
    return jnp.concatenate([ref[pl.ds(s, t, stride=TOKEN_ROWS), :] for s in range(TOKEN_ROWS)], axis=1)


def _outproj_router_kernel(x_ref, a_ref, b_ref, wa_ref, wb_ref, g_ref, rhi_ref, rlo_ref, rb_ref,
                           xo_ref, h_ref, rt_ref):
    y = x_ref[...] + _dot(a_ref[...], wa_ref[...]) + _dot(b_ref[...], wb_ref[...])
    xo_ref[...] = y
    hn = _rms_rows(y, g_ref[...])
    _store_token_tiles(h_ref, hn)
    hi = hn.astype(BF16)
    lo = (hn - hi.astype(F32)).astype(BF16)
    both = _dot(hi, jnp.concatenate([rhi_ref[...], rlo_ref[...]], axis=1))
    logits = (both[:, :LANES] + _dot(lo, rhi_ref[...]) + both[:, LANES:]) + rb_ref[...]

    lane = lax.broadcasted_iota(jnp.int32, logits.shape, 1)
    big = jnp.int32(LANES)
    is_group = lane < N_GROUPS
    gl = jnp.where(is_group, logits, NEG_INF)
    gmax = jnp.max(gl, axis=1, keepdims=True)
    grp = jnp.min(jnp.where(is_group & (gl == gmax), lane, big), axis=1, keepdims=True)
    g_w = 1.0 / jnp.sum(jnp.where(is_group, jnp.exp(gl - gmax), 0.0), axis=1, keepdims=True)
    first = EXPERT_LANE0 + grp * EXPERTS_PER_GROUP
    in_group = (lane >= first) & (lane < first + EXPERTS_PER_GROUP)
    el = jnp.where(in_group, logits, NEG_INF)
    v1 = jnp.max(el, axis=1, keepdims=True)
    i1 = jnp.min(jnp.where(in_group & (el == v1), lane, big), axis=1, keepdims=True)
    rest = in_group & (lane != i1)
    el2 = jnp.where(rest, logits, NEG_INF)
    v2 = jnp.max(el2, axis=1, keepdims=True)
    i2 = jnp.min(jnp.where(rest & (el2 == v2), lane, big), axis=1, keepdims=True)
    e2w = jnp.exp(v2 - v1)
    w1 = g_w / (1.0 + e2w)
    w2 = g_w * e2w / (1.0 + e2w)
    e1 = (i1 - EXPERT_LANE0).astype(F32)
    e2 = (i2 - EXPERT_LANE0).astype(F32)
    rt_ref[...] = jnp.where(lane == ROUTE_E1, e1,
                            jnp.where(lane == ROUTE_E2, e2,
                                      jnp.where(lane == ROUTE_W1, w1,
                                                jnp.where(lane == ROUTE_W2, w2, 0.0))))


def outproj_router(x, mix_a, blk_a, mix_b, blk_b, w_out_bf16, ffn_gain, r_hi, r_lo, r_bias):
    n, d = x.shape
    half = w_out_bf16.shape[0] // 2
    row = lambda i: (i, 0)
    const = lambda i: (0, 0)
    return pl.pallas_call(
        _outproj_router_kernel,
        grid=(n // ROW_TILE,),
        in_specs=[pl.BlockSpec((ROW_TILE, d), row),
                  pl.BlockSpec((ROW_TILE, half), lambda i: (i, blk_a)),
                  pl.BlockSpec((ROW_TILE, half), lambda i: (i, blk_b)),
                  pl.BlockSpec((half, d), lambda i: (0, 0)),
                  pl.BlockSpec((half, d), lambda i: (1, 0)),
                  pl.BlockSpec((1, d), const),
                  pl.BlockSpec((d, LANES), const),
                  pl.BlockSpec((d, LANES), const),
                  pl.BlockSpec((1, LANES), const)],
        out_specs=[pl.BlockSpec((ROW_TILE, d), row),
                   pl.BlockSpec((ROW_TILE * TOKEN_ROWS, LANES), row),
                   pl.BlockSpec((ROW_TILE, LANES), row)],
        out_shape=[jax.ShapeDtypeStruct((n, d), F32),
                   jax.ShapeDtypeStruct((n * TOKEN_ROWS, LANES), F32),
                   jax.ShapeDtypeStruct((n, LANES), F32)],
        compiler_params=_params("parallel"),
        name="outproj_router",
    )(x, mix_a, mix_b, w_out_bf16, w_out_bf16, ffn_gain.reshape(1, d), r_hi, r_lo, r_bias)


def _rank_kernel(rt_ref, tri_ref, rk_ref, cnt_ref, carry_scr):
    @pl.when(pl.program_id(0) == 0)
    def _():
        carry_scr[...] = jnp.zeros(carry_scr.shape, F32)

    rt = rt_ref[...]
    lane = lax.broadcasted_iota(jnp.int32, rt.shape, 1)
    e1 = rt[:, ROUTE_E1:ROUTE_E1 + 1].astype(jnp.int32)
    e2 = rt[:, ROUTE_E2:ROUTE_E2 + 1].astype(jnp.int32)
    oh1 = lane == e1
    oh2 = lane == e2
    both = jnp.where(oh1 | oh2, 1.0, 0.0)
    before = _dot(tri_ref[...], both.astype(BF16)) + carry_scr[0:1, :]
    r1 = jnp.sum(jnp.where(oh1, before, 0.0), axis=1, keepdims=True)
    r2 = jnp.sum(jnp.where(oh2, before, 0.0), axis=1, keepdims=True)
    rk_ref[...] = jnp.where(lane == 0, r1, jnp.where(lane == 1, r2, 0.0))
    total = carry_scr[0:1, :] + jnp.sum(both, axis=0, keepdims=True)
    carry_scr[...] = jnp.broadcast_to(total, carry_scr.shape)
    cnt_ref[...] = jnp.broadcast_to(total, cnt_ref.shape)


def rank_slots(route):
    n = route.shape[0]
    tri = jnp.tril(jnp.ones((ROW_TILE, ROW_TILE), BF16), k=-1)
    return pl.pallas_call(
        _rank_kernel,
        grid=(n // ROW_TILE,),
        in_specs=[pl.BlockSpec((ROW_TILE, LANES), lambda i: (i, 0)),
                  pl.BlockSpec((ROW_TILE, ROW_TILE), lambda i: (0, 0))],
        out_specs=[pl.BlockSpec((ROW_TILE, LANES), lambda i: (i, 0)),
                   pl.BlockSpec((8, LANES), lambda i: (0, 0))],
        out_shape=[jax.ShapeDtypeStruct((n, LANES), F32),
                   jax.ShapeDtypeStruct((8, LANES), F32)],
        scratch_shapes=[pltpu.VMEM((8, LANES), F32)],
        compiler_params=_params("arbitrary"),
        name="moe_rank",
    )(route, tri)


DISPATCH_TOKENS = 512
DMA_UNROLL = 8


def _dispatch_kernel(dest_ref, zblk_ref, nz_ref, h_ref, xs_ref, zero_scr, sem, zsem):
    base = pl.program_id(0) * DISPATCH_TOKENS

    @pl.when(pl.program_id(0) == 0)
    def _():
        zero_scr[...] = jnp.zeros(zero_scr.shape, F32)

        def zero_copy(i):
            rows = MOE_TILE * TOKEN_ROWS
            start = pl.multiple_of(zblk_ref[i] * rows, rows)
            return pltpu.make_async_copy(zero_scr, xs_ref.at[pl.ds(start, rows)], zsem)

        def zissue(i, carry):
            zero_copy(i).start()
            return carry

        def zdrain(i, carry):
            zero_copy(i).wait()
            return carry

        lax.fori_loop(0, nz_ref[0], zissue, 0)
        lax.fori_loop(0, nz_ref[0], zdrain, 0)

    def tile_copy(i, k):
        src = pl.ds(pl.multiple_of(i * TOKEN_ROWS, TOKEN_ROWS), TOKEN_ROWS)
        dst = pl.ds(pl.multiple_of(dest_ref[TOP_K * (base + i) + k], TOKEN_ROWS), TOKEN_ROWS)
        return pltpu.make_async_copy(h_ref.at[src], xs_ref.at[dst], sem)

    def issue(i, carry):
        for k in range(TOP_K):
            tile_copy(i, k).start(priority=k % 2)
        return carry

    lax.fori_loop(0, DISPATCH_TOKENS, issue, 0, unroll=DMA_UNROLL)

    for k in range(TOP_K):
        pltpu.make_async_copy(h_ref, xs_ref.at[pl.ds(0, DISPATCH_TOKENS * TOKEN_ROWS)], sem).wait()


def dispatch_rows(h, dest, zero_blocks, n_zero, n_slots):
    n = h.shape[0] // TOKEN_ROWS
    return pl.pallas_call(
        _dispatch_kernel,
        grid_spec=pltpu.PrefetchScalarGridSpec(
            num_scalar_prefetch=3,
            grid=(n // DISPATCH_TOKENS,),
            in_specs=[pl.BlockSpec((DISPATCH_TOKENS * TOKEN_ROWS, LANES), lambda i, dst, zb, nz: (i, 0))],
            out_specs=pl.BlockSpec(memory_space=pl.ANY),
            scratch_shapes=[pltpu.VMEM((MOE_TILE * TOKEN_ROWS, LANES), F32),
                            pltpu.SemaphoreType.DMA(()),
                            pltpu.SemaphoreType.DMA(())]),
        out_shape=jax.ShapeDtypeStruct((n_slots * TOKEN_ROWS, LANES), F32),
        compiler_params=_params("arbitrary"),
        name="moe_dispatch",
    )(dest, zero_blocks, n_zero, h)


def _expert_kernel(be_ref, nb_ref, x_ref, wg_ref, wu_ref, wd_ref, y_ref, wg_b, wu_b, wd_b):
    i = pl.program_id(0)

    @pl.when((i == 0) | (be_ref[i] != be_ref[jnp.maximum(i - 1, 0)]))
    def _():
        wg_b[...] = wg_ref[...].astype(BF16)
        wu_b[...] = wu_ref[...].astype(BF16)
        wd_b[...] = wd_ref[...].astype(BF16)

    @pl.when(i < nb_ref[0])
    def _():
        xb = _load_token_tiles(x_ref, MOE_TILE).astype(BF16)
        gate = _dot(xb, wg_b[...])
        up = _dot(xb, wu_b[...])
        hidden = (gate * _sigmoid(gate) * up).astype(BF16)
        _store_token_tiles(y_ref, _dot(hidden, wd_b[...]))

    @pl.when(i >= nb_ref[0])
    def _():
        y_ref[...] = jnp.zeros(y_ref.shape, F32)


def expert_ffn(xs, block_expert, n_used, layer, w_gate, w_up, w_down):
    p = xs.shape[0] // TOKEN_ROWS
    d, ff = w_gate.shape[2], w_gate.shape[3]
    slot_block = pl.BlockSpec((MOE_TILE * TOKEN_ROWS, LANES), lambda i, be, nb: (i, 0))
    return pl.pallas_call(
        _expert_kernel,
        grid_spec=pltpu.PrefetchScalarGridSpec(
            num_scalar_prefetch=2,
            grid=(p // MOE_TILE,),
            in_specs=[slot_block,
                      pl.BlockSpec((None, None, d, ff), lambda i, be, nb: (layer, be[i], 0, 0)),
                      pl.BlockSpec((None, None, d, ff), lambda i, be, nb: (layer, be[i], 0, 0)),
                      pl.BlockSpec((None, None, ff, d), lambda i, be, nb: (layer, be[i], 0, 0))],
            out_specs=slot_block,
            scratch_shapes=[pltpu.VMEM((d, ff), BF16), pltpu.VMEM((d, ff), BF16),
                            pltpu.VMEM((ff, d), BF16)]),
        out_shape=jax.ShapeDtypeStruct(xs.shape, F32),
        compiler_params=_params("arbitrary"),
        name="moe_experts",
    )(block_expert, n_used, xs, w_gate, w_up, w_down)


COMBINE_TOKENS = 512


def _combine_kernel(dest_ref, x_ref, rt_ref, y_ref, o_ref, buf0, buf1, sem):
    base = pl.program_id(0) * COMBINE_TOKENS
    bufs = (buf0, buf1)

    def tile_copy(i, k):
        src = pl.ds(pl.multiple_of(dest_ref[TOP_K * (base + i) + k], TOKEN_ROWS), TOKEN_ROWS)
        dst = pl.ds(pl.multiple_of(i * TOKEN_ROWS, TOKEN_ROWS), TOKEN_ROWS)
        return pltpu.make_async_copy(y_ref.at[src], bufs[k].at[dst], sem)

    def issue(i, carry):
        for k in range(TOP_K):
            tile_copy(i, k).start(priority=k % 2)
        return carry

    lax.fori_loop(0, COMBINE_TOKENS, issue, 0, unroll=DMA_UNROLL)
    for k in range(TOP_K):
        pltpu.make_async_copy(y_ref.at[pl.ds(0, COMBINE_TOKENS * TOKEN_ROWS)], bufs[k], sem).wait()
    rt = rt_ref[...]
    w1 = rt[:, ROUTE_W1:ROUTE_W1 + 1]
    w2 = rt[:, ROUTE_W2:ROUTE_W2 + 1]
    o_ref[...] = x_ref[...] + (w1 * _load_token_tiles(buf0, COMBINE_TOKENS)
                               + w2 * _load_token_tiles(buf1, COMBINE_TOKENS))


def combine_rows(x, route, y, dest):
    n, d = x.shape
    return pl.pallas_call(
        _combine_kernel,
        grid_spec=pltpu.PrefetchScalarGridSpec(
            num_scalar_prefetch=1,
            grid=(n // COMBINE_TOKENS,),
            in_specs=[pl.BlockSpec((COMBINE_TOKENS, d), lambda i, dst: (i, 0)),
                      pl.BlockSpec((COMBINE_TOKENS, LANES), lambda i, dst: (i, 0)),
                      pl.BlockSpec(memory_space=pl.ANY)],
            out_specs=pl.BlockSpec((COMBINE_TOKENS, d), lambda i, dst: (i, 0)),
            scratch_shapes=[pltpu.VMEM((COMBINE_TOKENS * TOKEN_ROWS, LANES), F32),
                            pltpu.VMEM((COMBINE_TOKENS * TOKEN_ROWS, LANES), F32),
                            pltpu.SemaphoreType.DMA(())]),
        out_shape=jax.ShapeDtypeStruct((n, d), F32),
        compiler_params=_params("arbitrary"),
        name="moe_combine",
    )(dest, x, route, y)


def moe_layer(x, h, route, layer, w_gate, w_up, w_down):
    n = x.shape[0]
    ranks, counts = rank_slots(route)
    counts = counts[0, :N_EXPERTS].astype(jnp.int32)
    padded = (counts + MOE_TILE - 1) // MOE_TILE * MOE_TILE
    pad_end = jnp.cumsum(padded)
    pad_start = pad_end - padded
    n_blocks = -(-(n * TOP_K + N_EXPERTS * (MOE_TILE - 1)) // MOE_TILE)
    eids = route[:, ROUTE_E1:ROUTE_E2 + 1].astype(jnp.int32)
    slot0 = jnp.sum(jnp.where(eids[..., None] == jnp.arange(N_EXPERTS), pad_start, 0), axis=-1)
    dest = ((slot0 + ranks[:, :TOP_K].astype(jnp.int32)) * TOKEN_ROWS).reshape(-1)
    block_start = jnp.arange(n_blocks, dtype=jnp.int32) * MOE_TILE
    block_expert = jnp.minimum(
        jnp.sum(block_start[:, None] >= pad_end[None, :], axis=1), N_EXPERTS - 1).astype(jnp.int32)
    n_used = (pad_end[-1:] // MOE_TILE).astype(jnp.int32)
    tail = n_used[0] + jnp.arange(N_EXPERTS + 1, dtype=jnp.int32)
    cand = jnp.concatenate([(pad_end // MOE_TILE - 1).astype(jnp.int32), tail])
    keep = jnp.concatenate([padded > 0, tail < n_blocks])
    zero_blocks = jnp.where(keep, cand, 0)[jnp.argsort(~keep, stable=True)].astype(jnp.int32)
    n_zero = jnp.sum(keep).astype(jnp.int32).reshape(1)
    xs = dispatch_rows(h, dest, zero_blocks, n_zero, n_blocks * MOE_TILE)
    y = expert_ffn(xs, block_expert, n_used, layer, w_gate, w_up, w_down)
    return combine_rows(x, route, y, dest)


def router_tables(w_group, b_group, w_erouter, b_erouter):
    d = w_group.shape[0]
    w_exp = jnp.transpose(w_erouter, (1, 0, 2)).reshape(d, N_EXPERTS)
    w = jnp.zeros((d, LANES), F32)
    w = w.at[:, :N_GROUPS].set(w_group.astype(F32))
    w = w.at[:, EXPERT_LANE0:EXPERT_LANE0 + N_EXPERTS].set(w_exp.astype(F32))
    bias = jnp.zeros((1, LANES), F32)
    bias = bias.at[0, :N_GROUPS].set(b_group.astype(F32))
    bias = bias.at[0, EXPERT_LANE0:EXPERT_LANE0 + N_EXPERTS].set(b_erouter.reshape(-1).astype(F32))
    hi = w.astype(BF16)
    lo = (w - hi.astype(F32)).astype(BF16)
    return hi, lo, bias


def _gates_kernel(ab_ref, a_ref, dt_ref, tri_ref, o_ref):
    x = ab_ref[...].astype(F32)
    lane = lax.broadcasted_iota(jnp.int32, x.shape, 1)
    z = x + dt_ref[...]
    softplus = jnp.maximum(z, 0.0) + jnp.log(1.0 + jnp.exp(-jnp.abs(z)))
    g = -jnp.exp(a_ref[...]) * softplus
    gcum = _split3_dot(tri_ref[...], g)
    o_ref[...] = jnp.where(lane < C_HEADS, gcum, _sigmoid(x))


def gdn_gates(proj, a_log, dt_bias):
    n = proj.shape[0]
    pad = lambda v: jnp.zeros((1, LANES), F32).at[0, :C_HEADS].set(v.astype(F32))
    r = jnp.arange(ROW_TILE)
    tri = ((r[:, None] >= r[None, :]) & (r[:, None] // CHUNK == r[None, :] // CHUNK)).astype(BF16)
    return pl.pallas_call(
        _gates_kernel,
        grid=(n // ROW_TILE,),
        in_specs=[pl.BlockSpec((ROW_TILE, LANES), lambda i: (i, 4 * C_WIDTH // LANES)),
                  pl.BlockSpec((1, LANES), lambda i: (0, 0)),
                  pl.BlockSpec((1, LANES), lambda i: (0, 0)),
                  pl.BlockSpec((ROW_TILE, ROW_TILE), lambda i: (0, 0))],
        out_specs=pl.BlockSpec((ROW_TILE, LANES), lambda i: (i, 0)),
        out_shape=jax.ShapeDtypeStruct((n, LANES), F32),
        compiler_params=_params("parallel"),
        name="gdn_gates",
    )(proj, pad(a_log), pad(dt_bias), tri)


CONV_COLS = 256
CONV_ROWS = 512
CONV_HALO = 8


def _conv_kernel(x_ref, w_ref, o_ref, pad_scr, *, normalize, scale, scale_blocks):
    seq = x_ref.shape[0]
    out_scale = jnp.where(pl.program_id(1) < scale_blocks, scale, 1.0)
    pad_scr[0:CONV_HALO, :] = jnp.zeros((CONV_HALO, CONV_COLS), F32)
    pad_scr[CONV_HALO:, :] = x_ref[...].astype(F32)
    w = w_ref[...]
    for r0 in range(0, seq, CONV_ROWS):
        acc = None
        for j in range(CONV_W):
            start = CONV_HALO + r0 - (CONV_W - 1) + j
            term = pad_scr[start:start + CONV_ROWS, :] * w[j:j + 1, :]
            acc = term if acc is None else acc + term
        y = acc * _sigmoid(acc)
        if normalize:
            for s in range(CONV_COLS // LANES):
                ys = y[:, s * LANES:(s + 1) * LANES]
                ssq = jnp.sum(ys * ys, axis=1, keepdims=True)
                o_ref[r0:r0 + CONV_ROWS, s * LANES:(s + 1) * LANES] = (
                    ys * lax.rsqrt(ssq + EPS) * out_scale).astype(o_ref.dtype)
        else:
            o_ref[r0:r0 + CONV_ROWS, :] = y.astype(o_ref.dtype)


def conv_silu(proj3, conv_w, col0, width, normalize, scale=1.0, scale_width=0):
    batch, seq, _ = proj3.shape
    cb0 = col0 // CONV_COLS
    return pl.pallas_call(
        functools.partial(_conv_kernel, normalize=normalize, scale=scale,
                          scale_blocks=scale_width // CONV_COLS),
        grid=(batch, width // CONV_COLS),
        in_specs=[pl.BlockSpec((None, seq, CONV_COLS), lambda b, c: (b, 0, cb0 + c)),
                  pl.BlockSpec((CONV_W, CONV_COLS), lambda b, c: (0, cb0 + c))],
        out_specs=pl.BlockSpec((None, seq, CONV_COLS), lambda b, c: (b, 0, c)),
        out_shape=jax.ShapeDtypeStruct((batch, seq, width), BF16),
        scratch_shapes=[pltpu.VMEM((seq + CONV_HALO, CONV_COLS), F32)],
        compiler_params=_params("parallel", "parallel"),
        name="conv_silu",
    )(proj3, conv_w)


def _head_column(gb, lane_index):
    lane = lax.broadcasted_iota(jnp.int32, gb.shape, 1)
    return jnp.sum(jnp.where(lane == lane_index, gb, 0.0), axis=1, keepdims=True)


def _chunk_decay(gcum_col):
    r = lax.broadcasted_iota(jnp.int32, (CHUNK, CHUNK), 0)
    c = lax.broadcasted_iota(jnp.int32, (CHUNK, CHUNK), 1)
    gcum = jnp.broadcast_to(gcum_col, (CHUNK, LANES))
    gcum_rows = jnp.transpose(gcum)[:CHUNK, :]
    gdiff = jnp.where(c <= r, gcum[:, :CHUNK] - gcum_rows, 0.0)
    return gcum, gdiff, r, c


def _split3_dot(a_bf16, b_f32):
    b0 = b_f32.astype(BF16)
    r1 = b_f32 - b0.astype(F32)
    b1 = r1.astype(BF16)
    b2 = (r1 - b1.astype(F32)).astype(BF16)
    return _dot(a_bf16, b0) + _dot(a_bf16, b1) + _dot(a_bf16, b2)


def _lmat_kernel(k_ref, gb_ref, l_ref):
    hg = pl.program_id(1)
    gb = gb_ref[...]
    tiles = [(ci, hh) for ci in range(GDN_ROWS // CHUNK) for hh in range(DELTA_HEADS)]
    rows = lambda ci: slice(ci * CHUNK, (ci + 1) * CHUNK)
    kk = {}
    for ci, hh in tiles:
        k = k_ref[rows(ci), hh * LANES:(hh + 1) * LANES]
        kk[ci, hh] = _dot_nt(k, k)
    for ci, hh in tiles:
        head = hg * DELTA_HEADS + hh
        gb_c = gb[rows(ci)]
        _, gdiff, r, c = _chunk_decay(_head_column(gb_c, head))
        beta = _head_column(gb_c, C_HEADS + head)
        l_ref[hh, rows(ci), :] = jnp.where(r > c, beta * kk[ci, hh] * jnp.exp(gdiff), 0.0)


def gdn_lmat(k3, gb3):
    batch, seq, _ = k3.shape
    width = DELTA_HEADS * LANES
    groups = C_HEADS // DELTA_HEADS
    return pl.pallas_call(
        _lmat_kernel,
        grid=(batch, groups, seq // GDN_ROWS),
        in_specs=[pl.BlockSpec((None, GDN_ROWS, width), lambda b, h, i: (b, i, groups + h)),
                  pl.BlockSpec((None, GDN_ROWS, LANES), lambda b, h, i: (b, i, 0))],
        out_specs=pl.BlockSpec((None, DELTA_HEADS, GDN_ROWS, CHUNK), lambda b, h, i: (b, h, i, 0)),
        out_shape=jax.ShapeDtypeStruct((batch, C_HEADS, seq, CHUNK), F32),
        compiler_params=_params("parallel", "parallel", "parallel"),
        name="gdn_lmat",
    )(k3, gb3)


def _solve_kernel(l_ref, t_ref):
    sub = 8
    col = lax.broadcasted_iota(jnp.int32, (sub, SOLVE_LANES), 0)
    for i in range(CHUNK):
        acc = [jnp.where(col + g * sub == i, 1.0, 0.0) for g in range(CHUNK // sub)]
        for j in range(i):
            lij = l_ref[i, j:j + 1, :]
            for g in range(j // sub + 1):
                acc[g] = acc[g] - lij * t_ref[j, g * sub:(g + 1) * sub, :]
        t_ref[i] = jnp.concatenate(acc, axis=0)


def gdn_solve(lmat_t):
    g = lmat_t.shape[2]
    return pl.pallas_call(
        _solve_kernel,
        grid=(g // SOLVE_LANES,),
        in_specs=[pl.BlockSpec((CHUNK, CHUNK, SOLVE_LANES), lambda i: (0, 0, i))],
        out_specs=pl.BlockSpec((CHUNK, CHUNK, SOLVE_LANES), lambda i: (0, 0, i)),
        out_shape=jax.ShapeDtypeStruct(lmat_t.shape, F32),
        compiler_params=_params("parallel"),
        name="gdn_solve",
    )(lmat_t)


DELTA_HEADS = 4
DELTA_ROWS = 256


def _delta_kernel(q_ref, k_ref, v_ref, z_ref, gb_ref, t_ref, gain_ref, o_ref, *state_scrs):
    hg = pl.program_id(1)

    @pl.when(pl.program_id(2) == 0)
    def _():
        for scr in state_scrs:
            scr[...] = jnp.zeros(scr.shape, F32)

    states = [scr[...] for scr in state_scrs]
    gb = gb_ref[...]
    gain = gain_ref[...]
    n_chunks = DELTA_ROWS // CHUNK
    tiles = [(ci, hh) for ci in range(n_chunks) for hh in range(DELTA_HEADS)]
    rows = lambda ci: slice(ci * CHUNK, (ci + 1) * CHUNK)
    cols = lambda hh: slice(hh * LANES, (hh + 1) * LANES)

    pre = {}
    for ci, hh in tiles:
        head = hg * DELTA_HEADS + hh
        gb_c = gb[rows(ci)]
        gcum, gdiff, r, c = _chunk_decay(_head_column(gb_c, head))
        beta = _head_column(gb_c, C_HEADS + head)
        q = q_ref[rows(ci), cols(hh)]
        k = k_ref[rows(ci), cols(hh)]
        kf = k.astype(F32)
        eg = jnp.exp(gcum)
        g_last = gcum[CHUNK - 1:CHUNK, :]
        pre[ci, hh] = dict(
            q=q, k=k, eg=eg, e_last=jnp.exp(g_last),
            decay=jnp.where(c <= r, jnp.exp(gdiff), 0.0),
            k_dec=(kf * jnp.exp(g_last - gcum)).astype(BF16),
            rhs=jnp.concatenate([(v_ref[rows(ci), cols(hh)].astype(F32) * beta).astype(BF16),
                                 (kf * beta * eg).astype(BF16)], axis=1))
    uw = {t: _dot(t_ref[t[1], rows(t[0]), :].astype(BF16), pre[t]["rhs"]).astype(BF16)
          for t in tiles}
    a_intra = {t: (_dot_nt(pre[t]["q"], pre[t]["k"]) * pre[t]["decay"]).astype(BF16) for t in tiles}
    a_uw = {t: _dot(a_intra[t], uw[t]) for t in tiles}
    kd_uw = {t: _dot_tn(pre[t]["k_dec"], uw[t]) for t in tiles}
    lhs = {t: jnp.concatenate(
        [(pre[t]["q"].astype(F32) * pre[t]["eg"] - a_uw[t][:, LANES:]).astype(BF16),
         kd_uw[t][:, LANES:].astype(BF16)], axis=0) for t in tiles}

    for ci in range(n_chunks):
        for hh in range(DELTA_HEADS):
            t = (ci, hh)
            state = states[hh]
            prod = _dot(lhs[t], state.astype(BF16))
            o = prod[:CHUNK] + a_uw[t][:, :LANES]
            states[hh] = state * pre[t]["e_last"] - prod[CHUNK:] + kd_uw[t][:, :LANES]
            z = z_ref[rows(ci), cols(hh)].astype(F32)
            o_ref[rows(ci), cols(hh)] = (_rms_rows(o, gain) * (z * _sigmoid(z))).astype(o_ref.dtype)
    for scr, state in zip(state_scrs, states):
        scr[...] = state


def gdn_delta(qk3, v3, proj3, gb3, tinv, out_gain):
    batch, seq, _ = v3.shape
    width = DELTA_HEADS * LANES
    groups = C_HEADS // DELTA_HEADS
    zcol = 3 * C_WIDTH // width
    blk = (None, DELTA_ROWS, width)
    return pl.pallas_call(
        _delta_kernel,
        grid=(batch, groups, seq // DELTA_ROWS),
        in_specs=[pl.BlockSpec(blk, lambda b, h, i: (b, i, h)),
                  pl.BlockSpec(blk, lambda b, h, i: (b, i, groups + h)),
                  pl.BlockSpec(blk, lambda b, h, i: (b, i, h)),
                  pl.BlockSpec(blk, lambda b, h, i: (b, i, zcol + h)),
                  pl.BlockSpec((None, DELTA_ROWS, LANES), lambda b, h, i: (b, i, 0)),
                  pl.BlockSpec((None, DELTA_HEADS, DELTA_ROWS, CHUNK), lambda b, h, i: (b, h, i, 0)),
                  pl.BlockSpec((1, LANES), lambda b, h, i: (0, 0))],
        out_specs=pl.BlockSpec(blk, lambda b, h, i: (b, i, h)),
        out_shape=jax.ShapeDtypeStruct((batch, seq, C_WIDTH), BF16),
        scratch_shapes=[pltpu.VMEM((C_DK, LANES), F32) for _ in range(DELTA_HEADS)],
        compiler_params=_params("parallel", "parallel", "arbitrary"),
        name="gdn_delta",
    )(qk3, qk3, v3, proj3, gb3, tinv, out_gain.reshape(1, LANES))


def rope_tables(seq):
    half = HEAD_DIM // 2
    inv_freq = 1.0 / (ROPE_THETA ** (jnp.arange(0, HEAD_DIM, 2, dtype=F32) / HEAD_DIM))
    ang = jnp.arange(seq, dtype=F32)[:, None] * inv_freq[None, :]
    cos, sin = jnp.cos(ang), jnp.sin(ang)
    cos_t = jnp.tile(cos, (1, LANES // half))
    sin_t = jnp.tile(jnp.concatenate([-sin, sin], axis=1), (1, LANES // HEAD_DIM))
    return cos_t, sin_t


def even_layer(x, batch, seq, layer, attn_norm, w_in, a_q_gain, a_k_gain, b_q_gain, b_k_gain,
               lq1, lk1, lq2, lk2, b_out_gain, w_out, ffn_gain, router):
    lam_init = 0.8 - 0.6 * math.exp(-0.3 * layer)
    scale = HEAD_DIM ** -0.5
    tile = lambda g, s: jnp.tile(g.astype(F32) * s, A_WIDTH // HEAD_DIM)
    gains = jnp.stack([tile(a_q_gain, scale), tile(a_k_gain, 1.0),
                       tile(b_q_gain, scale), tile(b_k_gain, 1.0)])
    cos_t, sin_t = rope_tables(seq)
    qk, vv = even_in_proj(x, attn_norm, w_in.astype(BF16), gains, cos_t, sin_t, seq)
    oa = dilated_attention(qk, vv, batch, seq)
    ob = diff_attention(qk, vv, lq1.astype(F32), lk1.astype(F32), lq2.astype(F32), lk2.astype(F32),
                        b_out_gain, lam_init, batch, seq)
    return outproj_router(x, oa, 0, ob, 0, w_out.astype(BF16), ffn_gain, *router)


def odd_layer(x, batch, seq, attn_norm, w_in, conv_w, a_log, dt_bias, out_gain, w_out, ffn_gain, router):
    n = batch * seq
    w_pad = jnp.pad(w_in, ((0, 0), (0, LANES - 2 * C_HEADS))).astype(BF16)
    proj = norm_matmul(x, attn_norm, w_pad, 1408)
    gb = gdn_gates(proj, a_log, dt_bias)
    proj3 = proj.reshape(batch, seq, proj.shape[1])
    conv_w = conv_w.astype(F32)
    qk3 = conv_silu(proj3, conv_w, 0, 2 * C_WIDTH, True, C_DK ** -0.5, C_WIDTH)
    v3 = conv_silu(proj3, conv_w, 2 * C_WIDTH, C_WIDTH, False)
    gb3 = gb.reshape(batch, seq, LANES)
    lmat = gdn_lmat(qk3, gb3)
    g = batch * C_HEADS * (seq // CHUNK)
    lmat_t = jnp.transpose(lmat.reshape(g, CHUNK, CHUNK), (1, 2, 0))
    tinv = jnp.transpose(gdn_solve(lmat_t), (2, 0, 1)).reshape(batch, C_HEADS, seq, CHUNK)
    mix = gdn_delta(qk3, v3, proj3, gb3, tinv, out_gain).reshape(n, C_WIDTH)
    return outproj_router(x, mix, 0, mix, 1, w_out.astype(BF16), ffn_gain, *router)


def kernel(x, even_attn_norm, even_w_in, a_q_norm, a_k_norm, b_q_norm, b_k_norm, b_lambda_q1, b_lambda_k1, b_lambda_q2, b_lambda_k2, b_out_norm, even_w_out, odd_attn_norm, odd_w_in, odd_conv_w, odd_a_log, odd_dt_bias, odd_out_norm, odd_w_out, ffn_norm, router_group_w, router_group_b, router_expert_w, router_expert_b, expert_w_gate, expert_w_up, expert_w_down):
    batch, seq, d = x.shape
    depth = ffn_norm.shape[0]
    xf = x.reshape(batch * seq, d).astype(F32)
    for layer in range(depth):
        i = layer // 2
        router = router_tables(router_group_w[layer], router_group_b[layer],
                               router_expert_w[layer], router_expert_b[layer])
        if layer % 2 == 0:
            xf, h, route = even_layer(
                xf, batch, seq, layer, even_attn_norm[i], even_w_in[i], a_q_norm[i], a_k_norm[i],
                b_q_norm[i], b_k_norm[i], b_lambda_q1[i], b_lambda_k1[i], b_lambda_q2[i],
                b_lambda_k2[i], b_out_norm[i], even_w_out[i], ffn_norm[layer], router)
        else:
            xf, h, route = odd_layer(
                xf, batch, seq, odd_attn_norm[i], odd_w_in[i], odd_conv_w[i], odd_a_log[i],
                odd_dt_bias[i], odd_out_norm[i], odd_w_out[i], ffn_norm[layer], router)
        xf = moe_layer(xf, h, route, layer, expert_w_gate, expert_w_up, expert_w_down)
    return xf.reshape(batch, seq, d).astype(x.dtype)
```

```python
import functools
import math

import jax
import jax.numpy as jnp
from jax import lax
from jax.experimental import pallas as pl
from jax.experimental.pallas import tpu as pltpu

F32 = jnp.float32
BF16 = jnp.bfloat16

D_MODEL = 1024
HEAD_DIM = 64
ROPE_THETA = 10000.0
A_HEADS = 8
A_BRANCHES = ((128, 1), (512, 4), (2048, 16))
A_BLOCK = 128
B_HEADS = 4
A_WIDTH = 512
B_WIDTH = 512
EVEN_IN = 3 * A_WIDTH + 3 * B_WIDTH
C_HEADS = 8
C_DK = 128
C_WIDTH = 1024
CONV_W = 4
CHUNK = 64
N_GROUPS = 4
EXPERTS_PER_GROUP = 8
N_EXPERTS = 32
TOP_K = 2
EXPERT_FF = 512
EPS = 1e-6
NEG_INF = -1e30

LANES = 128
VMEM_LIMIT = 48 * 1024 * 1024

ROW_TILE = 512
MOE_TILE = 512
DIFF_TQ = 512
GDN_ROWS = 512
SOLVE_LANES = 128


def _params(*sem):
    return pltpu.CompilerParams(dimension_semantics=sem, vmem_limit_bytes=VMEM_LIMIT)


def _dot(a, b):
    return jnp.dot(a, b, preferred_element_type=F32)


def _dot_nt(a, b):
    return lax.dot_general(a, b, (((1,), (1,)), ((), ())), preferred_element_type=F32)


def _dot_tn(a, b):
    return lax.dot_general(a, b, (((0,), (0,)), ((), ())), preferred_element_type=F32)


def _split_dot(a_f32, b_bf16):
    hi = a_f32.astype(BF16)
    lo = (a_f32 - hi.astype(F32)).astype(BF16)
    return _dot(hi, b_bf16) + _dot(lo, b_bf16)


def _sigmoid(x):
    return 1.0 / (1.0 + jnp.exp(-x))


def _rms_rows(xf, gain):
    ms = jnp.mean(xf * xf, axis=-1, keepdims=True)
    return xf * lax.rsqrt(ms + EPS) * gain


def _norm_matmul_kernel(x_ref, g_ref, w_ref, o_ref, *, col_chunk):
    h = _rms_rows(x_ref[...], g_ref[...]).astype(BF16)
    for c in range(0, o_ref.shape[1], col_chunk):
        o_ref[:, c:c + col_chunk] = _dot(h, w_ref[:, c:c + col_chunk]).astype(o_ref.dtype)


def norm_matmul(x, gain, w_bf16, col_chunk):
    n, d = x.shape
    m = w_bf16.shape[1]
    return pl.pallas_call(
        functools.partial(_norm_matmul_kernel, col_chunk=col_chunk),
        grid=(n // ROW_TILE,),
        in_specs=[pl.BlockSpec((ROW_TILE, d), lambda i: (i, 0)),
                  pl.BlockSpec((1, d), lambda i: (0, 0)),
                  pl.BlockSpec((d, m), lambda i: (0, 0))],
        out_specs=pl.BlockSpec((ROW_TILE, m), lambda i: (i, 0)),
        out_shape=jax.ShapeDtypeStruct((n, m), BF16),
        compiler_params=_params("parallel"),
        name="norm_matmul",
    )(x, gain.reshape(1, d), w_bf16)


def _even_in_kernel(x_ref, g_ref, w_ref, hg_ref, cos_ref, sin_ref, qk_ref, v_ref):
    h = _rms_rows(x_ref[...], g_ref[...]).astype(BF16)
    lane = lax.broadcasted_iota(jnp.int32, (1, LANES), 1)
    first_half = (lane % HEAD_DIM) < (HEAD_DIM // 2)
    r = lax.broadcasted_iota(jnp.int32, (LANES, LANES), 0) // HEAD_DIM
    c = lax.broadcasted_iota(jnp.int32, (LANES, LANES), 1) // HEAD_DIM
    same_head = jnp.where(r == c, 1.0, 0.0).astype(BF16)
    cos = cos_ref[...]
    sin = sin_ref[...]
    for chunk in range(EVEN_IN // A_WIDTH):
        y = _dot(h, w_ref[:, chunk * A_WIDTH:(chunk + 1) * A_WIDTH])
        if chunk % 3 == 2:
            v_ref[:, (chunk // 3) * A_WIDTH:(chunk // 3 + 1) * A_WIDTH] = y.astype(v_ref.dtype)
            continue
        group = chunk - chunk // 3
        for s in range(A_WIDTH // LANES):
            sl = slice(s * LANES, (s + 1) * LANES)
            x = y[:, sl]
            ssq = _split_dot(x * x, same_head)
            z = x * lax.rsqrt(ssq * (1.0 / HEAD_DIM) + EPS) * hg_ref[group:group + 1, sl]
            partner = jnp.where(first_half,
                                pltpu.roll(z, LANES - HEAD_DIM // 2, axis=1),
                                pltpu.roll(z, HEAD_DIM // 2, axis=1))
            qk_ref[:, group * A_WIDTH + s * LANES:group * A_WIDTH + (s + 1) * LANES] = (
                z * cos + partner * sin).astype(qk_ref.dtype)


def even_in_proj(x, gain, w_bf16, head_gains, cos_t, sin_t, seq):
    n, d = x.shape
    tiles_per_seq = seq // ROW_TILE
    return pl.pallas_call(
        _even_in_kernel,
        grid=(n // ROW_TILE,),
        in_specs=[pl.BlockSpec((ROW_TILE, d), lambda i: (i, 0)),
                  pl.BlockSpec((1, d), lambda i: (0, 0)),
                  pl.BlockSpec((d, EVEN_IN), lambda i: (0, 0)),
                  pl.BlockSpec((4, A_WIDTH), lambda i: (0, 0)),
                  pl.BlockSpec((ROW_TILE, LANES), lambda i: (i % tiles_per_seq, 0)),
                  pl.BlockSpec((ROW_TILE, LANES), lambda i: (i % tiles_per_seq, 0))],
        out_specs=[pl.BlockSpec((ROW_TILE, 4 * A_WIDTH), lambda i: (i, 0)),
                   pl.BlockSpec((ROW_TILE, 2 * A_WIDTH), lambda i: (i, 0))],
        out_shape=[jax.ShapeDtypeStruct((n, 4 * A_WIDTH), BF16),
                   jax.ShapeDtypeStruct((n, 2 * A_WIDTH), BF16)],
        compiler_params=_params("parallel"),
        name="even_in_proj",
    )(x, gain.reshape(1, d), w_bf16, head_gains, cos_t, sin_t)


DILATED_GROUP = 4


def _dilated_kernel(q_ref, k_ref, v_ref, o_ref, qf, kf, vf, of, lf, bias):
    seq = q_ref.shape[0]
    nb = A_BLOCK
    qf[...] = q_ref[...].astype(F32)
    kf[...] = k_ref[...].astype(F32)
    vf[...] = v_ref[...].astype(F32)
    lane = lax.broadcasted_iota(jnp.int32, (1, LANES), 1)
    low = lane < HEAD_DIM
    qi = lax.broadcasted_iota(jnp.int32, (2 * nb, 2 * nb), 0) % nb
    kj = lax.broadcasted_iota(jnp.int32, (2 * nb, 2 * nb), 1)
    band = (kj >= qi) & (kj <= qi + nb)
    bias[0] = jnp.where(band & (kj >= nb), 0.0, NEG_INF)
    bias[1] = jnp.where(band, 0.0, NEG_INF)
    zero = jnp.zeros((), BF16)
    for idx, (window, d) in enumerate(A_BRANCHES):
        assert window // d == nb
        nblk = seq // (d * nb)
        first, last = idx == 0, idx == len(A_BRANCHES) - 1

        group = min(DILATED_GROUP, nblk)
        nres = DILATED_GROUP // group
        tiles = [(a, j) for a in range(nres) for j in range(group)]

        def blocks(t, carry, d=d, nblk=nblk, first=first, last=last, group=group, nres=nres, tiles=tiles):
            r0 = (t // (nblk // group)) * nres
            c0 = (t % (nblk // group)) * group

            def rows(a, c):
                return pl.ds(r0 + a + c * (nb * d), nb, stride=d)

            kb = {(a, j): kf[rows(a, jnp.maximum(c0 - 1 + j, 0)), :].astype(BF16)
                  for a in range(nres) for j in range(group + 1)}
            vb = {(a, j): vf[rows(a, jnp.maximum(c0 - 1 + j, 0)), :].astype(BF16)
                  for a in range(nres) for j in range(group + 1)}
            qb = {(a, j): qf[rows(a, c0 + j), :].astype(BF16) for a, j in tiles}
            s = {(a, j): _dot_nt(
                jnp.concatenate([jnp.where(low, qb[a, j], zero), jnp.where(low, zero, qb[a, j])], axis=0),
                jnp.concatenate([kb[a, j], kb[a, j + 1]], axis=0))
                + bias[jnp.minimum(c0 + j, 1)] for a, j in tiles}
            m = {t_: jnp.max(s[t_], axis=1, keepdims=True) for t_ in tiles}
            p = {t_: jnp.exp(s[t_] - m[t_]) for t_ in tiles}
            den = {t_: jnp.sum(p[t_], axis=1, keepdims=True) for t_ in tiles}
            o2 = {(a, j): _dot(p[a, j].astype(BF16), jnp.concatenate([vb[a, j], vb[a, j + 1]], axis=0))
                  / den[a, j] for a, j in tiles}
            for a, j in tiles:
                cur = rows(a, c0 + j)
                lse2 = m[a, j] + jnp.log(den[a, j])
                o = jnp.where(low, o2[a, j][:nb], o2[a, j][nb:])
                lse = jnp.where(low, lse2[:nb], lse2[nb:])
                if not first:
                    o_prev = of[cur, :]
                    l_prev = lf[cur, :]
                    top = jnp.maximum(lse, l_prev)
                    wa = jnp.exp(lse - top)
                    wb = jnp.exp(l_prev - top)
                    tot = wa + wb
                    o = (wa * o + wb * o_prev) / tot
                    lse = top + jnp.log(tot)
                of[cur, :] = o
                if not last:
                    lf[cur, :] = lse
            return carry

        lax.fori_loop(0, seq // (nb * DILATED_GROUP), blocks, 0)
    o_ref[...] = of[...].astype(o_ref.dtype)


def dilated_attention(qk, vv, batch, seq):
    qk3 = qk.reshape(batch, seq, 2048)
    pj3 = vv.reshape(batch, seq, 2 * A_WIDTH)
    slabs = A_WIDTH // LANES
    blk = (None, seq, LANES)
    out = pl.pallas_call(
        _dilated_kernel,
        grid=(batch, slabs),
        in_specs=[pl.BlockSpec(blk, lambda b, s: (b, 0, s)),
                  pl.BlockSpec(blk, lambda b, s: (b, 0, slabs + s)),
                  pl.BlockSpec(blk, lambda b, s: (b, 0, s))],
        out_specs=pl.BlockSpec(blk, lambda b, s: (b, 0, s)),
        out_shape=jax.ShapeDtypeStruct((batch, seq, A_WIDTH), BF16),
        scratch_shapes=[pltpu.VMEM((seq, LANES), F32) for _ in range(5)]
        + [pltpu.VMEM((2, 2 * A_BLOCK, 2 * A_BLOCK), F32)],
        compiler_params=_params("parallel", "parallel"),
        name="dilated_attention",
    )(qk3, qk3, pj3)
    return out.reshape(batch * seq, A_WIDTH)


def _diff_attn_kernel(q_ref, k_ref, v_ref, lq1_ref, lk1_ref, lq2_ref, lk2_ref, g_ref, o_ref,
                      m_scr, l_scr, acc_scr, *, lam_init):
    h = pl.program_id(1)
    i = pl.program_id(2)
    tq = DIFF_TQ
    lane = lax.broadcasted_iota(jnp.int32, (1, LANES), 1)
    low = lane < HEAD_DIM
    zero = jnp.zeros((), BF16)
    q = q_ref[...]
    q2 = jnp.concatenate([jnp.where(low, q, zero), jnp.where(low, zero, q)], axis=0)
    m_scr[...] = jnp.full(m_scr.shape, NEG_INF, F32)
    l_scr[...] = jnp.zeros(l_scr.shape, F32)
    acc_scr[...] = jnp.zeros(acc_scr.shape, F32)

    def step(j, masked):
        start = pl.multiple_of(j * tq, tq)
        k = k_ref[pl.ds(start, tq), :]
        v = v_ref[pl.ds(start, tq), :]
        s = _dot_nt(q2, k)
        if masked:
            qi = lax.broadcasted_iota(jnp.int32, (2 * tq, tq), 0) % tq
            kj = lax.broadcasted_iota(jnp.int32, (2 * tq, tq), 1)
            s = jnp.where(kj <= qi, s, NEG_INF)
        m_old = m_scr[...]
        m_new = jnp.maximum(m_old, jnp.max(s, axis=1, keepdims=True))
        alpha = jnp.exp(m_old - m_new)
        p = jnp.exp(s - jnp.concatenate([m_new] * (tq // LANES), axis=1))
        l_scr[...] = alpha * l_scr[...] + jnp.sum(p, axis=1, keepdims=True)
        acc_scr[...] = alpha * acc_scr[...] + _dot(p.astype(BF16), v)
        m_scr[...] = m_new

    def body(j, carry):
        step(2 * j, False)
        step(2 * j + 1, False)
        return carry

    lax.fori_loop(0, i // 2, body, 0)

    @pl.when(i % 2 == 1)
    def _():
        step(i - 1, False)

    step(i, True)

    acc = acc_scr[...] / l_scr[...]
    d1 = jnp.sum(lq1_ref[pl.ds(h, 1), :] * lk1_ref[pl.ds(h, 1), :], axis=1, keepdims=True)
    d2 = jnp.sum(lq2_ref[pl.ds(h, 1), :] * lk2_ref[pl.ds(h, 1), :], axis=1, keepdims=True)
    lam = jnp.exp(d1) - jnp.exp(d2) + lam_init
    o = acc[:tq] - lam * acc[tq:]
    o_ref[...] = (_rms_rows(o, g_ref[...]) * (1.0 - lam_init)).astype(o_ref.dtype)


def diff_attention(qk, vv, lq1, lk1, lq2, lk2, out_gain, lam_init, batch, seq):
    n = batch * seq
    qk3 = qk.reshape(batch, seq, 2048)
    pj3 = vv.reshape(batch, seq, 2 * A_WIDTH)
    qcol = 2 * A_WIDTH // LANES
    kcol = (2 * A_WIDTH + B_WIDTH) // LANES
    vcol = A_WIDTH // LANES
    small = pl.BlockSpec((B_HEADS, HEAD_DIM), lambda b, h, i: (0, 0))
    out = pl.pallas_call(
        functools.partial(_diff_attn_kernel, lam_init=lam_init),
        grid=(batch, B_HEADS, seq // DIFF_TQ),
        in_specs=[pl.BlockSpec((None, DIFF_TQ, LANES), lambda b, h, i: (b, i, qcol + h)),
                  pl.BlockSpec((None, seq, LANES), lambda b, h, i: (b, 0, kcol + h)),
                  pl.BlockSpec((None, seq, LANES), lambda b, h, i: (b, 0, vcol + h)),
                  small, small, small, small,
                  pl.BlockSpec((1, LANES), lambda b, h, i: (0, 0))],
        out_specs=pl.BlockSpec((None, DIFF_TQ, LANES), lambda b, h, i: (b, i, h)),
        out_shape=jax.ShapeDtypeStruct((batch, seq, B_WIDTH), BF16),
        scratch_shapes=[pltpu.VMEM((2 * DIFF_TQ, LANES), F32),
                        pltpu.VMEM((2 * DIFF_TQ, LANES), F32),
                        pltpu.VMEM((2 * DIFF_TQ, LANES), F32)],
        compiler_params=_params("parallel", "parallel", "parallel"),
        name="diff_attention",
    )(qk3, qk3, pj3, lq1, lk1, lq2, lk2, out_gain.reshape(1, LANES))
    return out.reshape(n, B_WIDTH)


ROUTE_E1, ROUTE_E2, ROUTE_W1, ROUTE_W2 = 0, 1, 2, 3
EXPERT_LANE0 = 8


TOKEN_ROWS = D_MODEL // LANES


def _store_token_tiles(ref, rows):
    t = rows.shape[0]
    for s in range(TOKEN_ROWS):
        ref[pl.ds(s, t, stride=TOKEN_ROWS), :] = rows[:, s * LANES:(s + 1) * LANES]


def _load_token_tiles(ref, t):
    return jnp.concatenate([ref[pl.ds(s, t, stride=TOKEN_ROWS), :] for s in range(TOKEN_ROWS)], axis=1)


def _outproj_router_kernel(x_ref, a_ref, b_ref, wa_ref, wb_ref, g_ref, rhi_ref, rlo_ref, rb_ref,
                           xo_ref, h_ref, rt_ref):
    y = x_ref[...] + _dot(a_ref[...], wa_ref[...]) + _dot(b_ref[...], wb_ref[...])
    xo_ref[...] = y
    hn = _rms_rows(y, g_ref[...])
    _store_token_tiles(h_ref, hn)
    hi = hn.astype(BF16)
    lo = (hn - hi.astype(F32)).astype(BF16)
    both = _dot(hi, jnp.concatenate([rhi_ref[...], rlo_ref[...]], axis=1))
    logits = (both[:, :LANES] + _dot(lo, rhi_ref[...]) + both[:, LANES:]) + rb_ref[...]

    lane = lax.broadcasted_iota(jnp.int32, logits.shape, 1)
    big = jnp.int32(LANES)
    is_group = lane < N_GROUPS
    gl = jnp.where(is_group, logits, NEG_INF)
    gmax = jnp.max(gl, axis=1, keepdims=True)
    grp = jnp.min(jnp.where(is_group & (gl == gmax), lane, big), axis=1, keepdims=True)
    g_w = 1.0 / jnp.sum(jnp.where(is_group, jnp.exp(gl - gmax), 0.0), axis=1, keepdims=True)
    first = EXPERT_LANE0 + grp * EXPERTS_PER_GROUP
    in_group = (lane >= first) & (lane < first + EXPERTS_PER_GROUP)
    el = jnp.where(in_group, logits, NEG_INF)
    v1 = jnp.max(el, axis=1, keepdims=True)
    i1 = jnp.min(jnp.where(in_group & (el == v1), lane, big), axis=1, keepdims=True)
    rest = in_group & (lane != i1)
    el2 = jnp.where(rest, logits, NEG_INF)
    v2 = jnp.max(el2, axis=1, keepdims=True)
    i2 = jnp.min(jnp.where(rest & (el2 == v2), lane, big), axis=1, keepdims=True)
    e2w = jnp.exp(v2 - v1)
    w1 = g_w / (1.0 + e2w)
    w2 = g_w * e2w / (1.0 + e2w)
    e1 = (i1 - EXPERT_LANE0).astype(F32)
    e2 = (i2 - EXPERT_LANE0).astype(F32)
    rt_ref[...] = jnp.where(lane == ROUTE_E1, e1,
                            jnp.where(lane == ROUTE_E2, e2,
                                      jnp.where(lane == ROUTE_W1, w1,
                                                jnp.where(lane == ROUTE_W2, w2, 0.0))))


def outproj_router(x, mix_a, blk_a, mix_b, blk_b, w_out_bf16, ffn_gain, r_hi, r_lo, r_bias):
    n, d = x.shape
    half = w_out_bf16.shape[0] // 2
    row = lambda i: (i, 0)
    const = lambda i: (0, 0)
    return pl.pallas_call(
        _outproj_router_kernel,
        grid=(n // ROW_TILE,),
        in_specs=[pl.BlockSpec((ROW_TILE, d), row),
                  pl.BlockSpec((ROW_TILE, half), lambda i: (i, blk_a)),
                  pl.BlockSpec((ROW_TILE, half), lambda i: (i, blk_b)),
                  pl.BlockSpec((half, d), lambda i: (0, 0)),
                  pl.BlockSpec((half, d), lambda i: (1, 0)),
                  pl.BlockSpec((1, d), const),
                  pl.BlockSpec((d, LANES), const),
                  pl.BlockSpec((d, LANES), const),
                  pl.BlockSpec((1, LANES), const)],
        out_specs=[pl.BlockSpec((ROW_TILE, d), row),
                   pl.BlockSpec((ROW_TILE * TOKEN_ROWS, LANES), row),
                   pl.BlockSpec((ROW_TILE, LANES), row)],
        out_shape=[jax.ShapeDtypeStruct((n, d), F32),
                   jax.ShapeDtypeStruct((n * TOKEN_ROWS, LANES), F32),
                   jax.ShapeDtypeStruct((n, LANES), F32)],
        compiler_params=_params("parallel"),
        name="outproj_router",
    )(x, mix_a, mix_b, w_out_bf16, w_out_bf16, ffn_gain.reshape(1, d), r_hi, r_lo, r_bias)


def _rank_kernel(rt_ref, tri_ref, rk_ref, cnt_ref, carry_scr):
    @pl.when(pl.program_id(0) == 0)
    def _():
        carry_scr[...] = jnp.zeros(carry_scr.shape, F32)

    rt = rt_ref[...]
    lane = lax.broadcasted_iota(jnp.int32, rt.shape, 1)
    e1 = rt[:, ROUTE_E1:ROUTE_E1 + 1].astype(jnp.int32)
    e2 = rt[:, ROUTE_E2:ROUTE_E2 + 1].astype(jnp.int32)
    oh1 = lane == e1
    oh2 = lane == e2
    both = jnp.where(oh1 | oh2, 1.0, 0.0)
    before = _dot(tri_ref[...], both.astype(BF16)) + carry_scr[0:1, :]
    r1 = jnp.sum(jnp.where(oh1, before, 0.0), axis=1, keepdims=True)
    r2 = jnp.sum(jnp.where(oh2, before, 0.0), axis=1, keepdims=True)
    rk_ref[...] = jnp.where(lane == 0, r1, jnp.where(lane == 1, r2, 0.0))
    total = carry_scr[0:1, :] + jnp.sum(both, axis=0, keepdims=True)
    carry_scr[...] = jnp.broadcast_to(total, carry_scr.shape)
    cnt_ref[...] = jnp.broadcast_to(total, cnt_ref.shape)


def rank_slots(route):
    n = route.shape[0]
    tri = jnp.tril(jnp.ones((ROW_TILE, ROW_TILE), BF16), k=-1)
    return pl.pallas_call(
        _rank_kernel,
        grid=(n // ROW_TILE,),
        in_specs=[pl.BlockSpec((ROW_TILE, LANES), lambda i: (i, 0)),
                  pl.BlockSpec((ROW_TILE, ROW_TILE), lambda i: (0, 0))],
        out_specs=[pl.BlockSpec((ROW_TILE, LANES), lambda i: (i, 0)),
                   pl.BlockSpec((8, LANES), lambda i: (0, 0))],
        out_shape=[jax.ShapeDtypeStruct((n, LANES), F32),
                   jax.ShapeDtypeStruct((8, LANES), F32)],
        scratch_shapes=[pltpu.VMEM((8, LANES), F32)],
        compiler_params=_params("arbitrary"),
        name="moe_rank",
    )(route, tri)


DISPATCH_TOKENS = 512
DMA_UNROLL = 8


def _dispatch_kernel(dest_ref, zblk_ref, nz_ref, h_ref, xs_ref, zero_scr, sem, zsem):
    base = pl.program_id(0) * DISPATCH_TOKENS

    @pl.when(pl.program_id(0) == 0)
    def _():
        zero_scr[...] = jnp.zeros(zero_scr.shape, F32)

        def zero_copy(i):
            rows = MOE_TILE * TOKEN_ROWS
            start = pl.multiple_of(zblk_ref[i] * rows, rows)
            return pltpu.make_async_copy(zero_scr, xs_ref.at[pl.ds(start, rows)], zsem)

        def zissue(i, carry):
            zero_copy(i).start()
            return carry

        def zdrain(i, carry):
            zero_copy(i).wait()
            return carry

        lax.fori_loop(0, nz_ref[0], zissue, 0)
        lax.fori_loop(0, nz_ref[0], zdrain, 0)

    def tile_copy(i, k):
        src = pl.ds(pl.multiple_of(i * TOKEN_ROWS, TOKEN_ROWS), TOKEN_ROWS)
        dst = pl.ds(pl.multiple_of(dest_ref[TOP_K * (base + i) + k], TOKEN_ROWS), TOKEN_ROWS)
        return pltpu.make_async_copy(h_ref.at[src], xs_ref.at[dst], sem)

    def issue(i, carry):
        for k in range(TOP_K):
            tile_copy(i, k).start(priority=k % 2)
        return carry

    lax.fori_loop(0, DISPATCH_TOKENS, issue, 0, unroll=DMA_UNROLL)

    for k in range(TOP_K):
        pltpu.make_async_copy(h_ref, xs_ref.at[pl.ds(0, DISPATCH_TOKENS * TOKEN_ROWS)], sem).wait()


def dispatch_rows(h, dest, zero_blocks, n_zero, n_slots):
    n = h.shape[0] // TOKEN_ROWS
    return pl.pallas_call(
        _dispatch_kernel,
        grid_spec=pltpu.PrefetchScalarGridSpec(
            num_scalar_prefetch=3,
            grid=(n // DISPATCH_TOKENS,),
            in_specs=[pl.BlockSpec((DISPATCH_TOKENS * TOKEN_ROWS, LANES), lambda i, dst, zb, nz: (i, 0))],
            out_specs=pl.BlockSpec(memory_space=pl.ANY),
            scratch_shapes=[pltpu.VMEM((MOE_TILE * TOKEN_ROWS, LANES), F32),
                            pltpu.SemaphoreType.DMA(()),
                            pltpu.SemaphoreType.DMA(())]),
        out_shape=jax.ShapeDtypeStruct((n_slots * TOKEN_ROWS, LANES), F32),
        compiler_params=_params("arbitrary"),
        name="moe_dispatch",
    )(dest, zero_blocks, n_zero, h)


def _expert_kernel(be_ref, nb_ref, x_ref, wg_ref, wu_ref, wd_ref, y_ref, wg_b, wu_b, wd_b):
    i = pl.program_id(0)

    @pl.when((i == 0) | (be_ref[i] != be_ref[jnp.maximum(i - 1, 0)]))
    def _():
        wg_b[...] = wg_ref[...].astype(BF16)
        wu_b[...] = wu_ref[...].astype(BF16)
        wd_b[...] = wd_ref[...].astype(BF16)

    @pl.when(i < nb_ref[0])
    def _():
        xb = _load_token_tiles(x_ref, MOE_TILE).astype(BF16)
        gate = _dot(xb, wg_b[...])
        up = _dot(xb, wu_b[...])
        hidden = (gate * _sigmoid(gate) * up).astype(BF16)
        _store_token_tiles(y_ref, _dot(hidden, wd_b[...]))

    @pl.when(i >= nb_ref[0])
    def _():
        y_ref[...] = jnp.zeros(y_ref.shape, F32)


def expert_ffn(xs, block_expert, n_used, layer, w_gate, w_up, w_down):
    p = xs.shape[0] // TOKEN_ROWS
    d, ff = w_gate.shape[2], w_gate.shape[3]
    slot_block = pl.BlockSpec((MOE_TILE * TOKEN_ROWS, LANES), lambda i, be, nb: (i, 0))
    return pl.pallas_call(
        _expert_kernel,
        grid_spec=pltpu.PrefetchScalarGridSpec(
            num_scalar_prefetch=2,
            grid=(p // MOE_TILE,),
            in_specs=[slot_block,
                      pl.BlockSpec((None, None, d, ff), lambda i, be, nb: (layer, be[i], 0, 0)),
                      pl.BlockSpec((None, None, d, ff), lambda i, be, nb: (layer, be[i], 0, 0)),
                      pl.BlockSpec((None, None, ff, d), lambda i, be, nb: (layer, be[i], 0, 0))],
            out_specs=slot_block,
            scratch_shapes=[pltpu.VMEM((d, ff), BF16), pltpu.VMEM((d, ff), BF16),
                            pltpu.VMEM((ff, d), BF16)]),
        out_shape=jax.ShapeDtypeStruct(xs.shape, F32),
        compiler_params=_params("arbitrary"),
        name="moe_experts",
    )(block_expert, n_used, xs, w_gate, w_up, w_down)


COMBINE_TOKENS = 512


def _combine_kernel(dest_ref, x_ref, rt_ref, y_ref, o_ref, buf00, buf01, buf10, buf11, sem0, sem1):
    step = pl.program_id(0)
    last = pl.num_programs(0) - 1
    bufs = ((buf00, buf01), (buf10, buf11))
    sems = (sem0, sem1)

    def issue(for_step, pair):
        base = for_step * COMBINE_TOKENS

        def tile_copy(i, k):
            src = pl.ds(pl.multiple_of(dest_ref[TOP_K * (base + i) + k], TOKEN_ROWS), TOKEN_ROWS)
            dst = pl.ds(pl.multiple_of(i * TOKEN_ROWS, TOKEN_ROWS), TOKEN_ROWS)
            return pltpu.make_async_copy(y_ref.at[src], bufs[pair][k].at[dst], sems[pair])

        def body(i, carry):
            for k in range(TOP_K):
                tile_copy(i, k).start(priority=k % 2)
            return carry

        lax.fori_loop(0, COMBINE_TOKENS, body, 0, unroll=DMA_UNROLL)

    def consume(pair):
        for k in range(TOP_K):
            pltpu.make_async_copy(y_ref.at[pl.ds(0, COMBINE_TOKENS * TOKEN_ROWS)],
                                  bufs[pair][k], sems[pair]).wait()
        rt = rt_ref[...]
        w1 = rt[:, ROUTE_W1:ROUTE_W1 + 1]
        w2 = rt[:, ROUTE_W2:ROUTE_W2 + 1]
        o_ref[...] = x_ref[...] + (w1 * _load_token_tiles(bufs[pair][0], COMBINE_TOKENS)
                                   + w2 * _load_token_tiles(bufs[pair][1], COMBINE_TOKENS))

    @pl.when(step == 0)
    def _():
        issue(0, 0)

    for pair in range(2):
        @pl.when(step % 2 == pair)
        def _(pair=pair):
            @pl.when(step < last)
            def _():
                issue(step + 1, 1 - pair)

            consume(pair)


def combine_rows(x, route, y, dest):
    n, d = x.shape
    return pl.pallas_call(
        _combine_kernel,
        grid_spec=pltpu.PrefetchScalarGridSpec(
            num_scalar_prefetch=1,
            grid=(n // COMBINE_TOKENS,),
            in_specs=[pl.BlockSpec((COMBINE_TOKENS, d), lambda i, dst: (i, 0)),
                      pl.BlockSpec((COMBINE_TOKENS, LANES), lambda i, dst: (i, 0)),
                      pl.BlockSpec(memory_space=pl.ANY)],
            out_specs=pl.BlockSpec((COMBINE_TOKENS, d), lambda i, dst: (i, 0)),
            scratch_shapes=[pltpu.VMEM((COMBINE_TOKENS * TOKEN_ROWS, LANES), F32) for _ in range(4)]
            + [pltpu.SemaphoreType.DMA(()), pltpu.SemaphoreType.DMA(())]),
        out_shape=jax.ShapeDtypeStruct((n, d), F32),
        compiler_params=_params("arbitrary"),
        name="moe_combine",
    )(dest, x, route, y)


def moe_layer(x, h, route, layer, w_gate, w_up, w_down):
    n = x.shape[0]
    ranks, counts = rank_slots(route)
    counts = counts[0, :N_EXPERTS].astype(jnp.int32)
    padded = (counts + MOE_TILE - 1) // MOE_TILE * MOE_TILE
    pad_end = jnp.cumsum(padded)
    pad_start = pad_end - padded
    n_blocks = -(-(n * TOP_K + N_EXPERTS * (MOE_TILE - 1)) // MOE_TILE)
    eids = route[:, ROUTE_E1:ROUTE_E2 + 1].astype(jnp.int32)
    slot0 = jnp.sum(jnp.where(eids[..., None] == jnp.arange(N_EXPERTS), pad_start, 0), axis=-1)
    dest = ((slot0 + ranks[:, :TOP_K].astype(jnp.int32)) * TOKEN_ROWS).reshape(-1)
    block_start = jnp.arange(n_blocks, dtype=jnp.int32) * MOE_TILE
    block_expert = jnp.minimum(
        jnp.sum(block_start[:, None] >= pad_end[None, :], axis=1), N_EXPERTS - 1).astype(jnp.int32)
    n_used = (pad_end[-1:] // MOE_TILE).astype(jnp.int32)
    tail = n_used[0] + jnp.arange(N_EXPERTS + 1, dtype=jnp.int32)
    cand = jnp.concatenate([(pad_end // MOE_TILE - 1).astype(jnp.int32), tail])
    keep = jnp.concatenate([padded > 0, tail < n_blocks])
    zero_blocks = jnp.where(keep, cand, 0)[jnp.argsort(~keep, stable=True)].astype(jnp.int32)
    n_zero = jnp.sum(keep).astype(jnp.int32).reshape(1)
    xs = dispatch_rows(h, dest, zero_blocks, n_zero, n_blocks * MOE_TILE)
    y = expert_ffn(xs, block_expert, n_used, layer, w_gate, w_up, w_down)
    return combine_rows(x, route, y, dest)


def router_tables(w_group, b_group, w_erouter, b_erouter):
    d = w_group.shape[0]
    w_exp = jnp.transpose(w_erouter, (1, 0, 2)).reshape(d, N_EXPERTS)
    w = jnp.zeros((d, LANES), F32)
    w = w.at[:, :N_GROUPS].set(w_group.astype(F32))
    w = w.at[:, EXPERT_LANE0:EXPERT_LANE0 + N_EXPERTS].set(w_exp.astype(F32))
    bias = jnp.zeros((1, LANES), F32)
    bias = bias.at[0, :N_GROUPS].set(b_group.astype(F32))
    bias = bias.at[0, EXPERT_LANE0:EXPERT_LANE0 + N_EXPERTS].set(b_erouter.reshape(-1).astype(F32))
    hi = w.astype(BF16)
    lo = (w - hi.astype(F32)).astype(BF16)
    return hi, lo, bias


def _gates_kernel(ab_ref, a_ref, dt_ref, tri_ref, o_ref):
    x = ab_ref[...].astype(F32)
    lane = lax.broadcasted_iota(jnp.int32, x.shape, 1)
    z = x + dt_ref[...]
    softplus = jnp.maximum(z, 0.0) + jnp.log(1.0 + jnp.exp(-jnp.abs(z)))
    g = -jnp.exp(a_ref[...]) * softplus
    gcum = _split3_dot(tri_ref[...], g)
    o_ref[...] = jnp.where(lane < C_HEADS, gcum, _sigmoid(x))


def gdn_gates(proj, a_log, dt_bias):
    n = proj.shape[0]
    pad = lambda v: jnp.zeros((1, LANES), F32).at[0, :C_HEADS].set(v.astype(F32))
    r = jnp.arange(ROW_TILE)
    tri = ((r[:, None] >= r[None, :]) & (r[:, None] // CHUNK == r[None, :] // CHUNK)).astype(BF16)
    return pl.pallas_call(
        _gates_kernel,
        grid=(n // ROW_TILE,),
        in_specs=[pl.BlockSpec((ROW_TILE, LANES), lambda i: (i, 4 * C_WIDTH // LANES)),
                  pl.BlockSpec((1, LANES), lambda i: (0, 0)),
                  pl.BlockSpec((1, LANES), lambda i: (0, 0)),
                  pl.BlockSpec((ROW_TILE, ROW_TILE), lambda i: (0, 0))],
        out_specs=pl.BlockSpec((ROW_TILE, LANES), lambda i: (i, 0)),
        out_shape=jax.ShapeDtypeStruct((n, LANES), F32),
        compiler_params=_params("parallel"),
        name="gdn_gates",
    )(proj, pad(a_log), pad(dt_bias), tri)


CONV_COLS = 256
CONV_ROWS = 512
CONV_HALO = 8


def _conv_kernel(x_ref, w_ref, o_ref, pad_scr, *, normalize, scale, scale_blocks):
    seq = x_ref.shape[0]
    out_scale = jnp.where(pl.program_id(1) < scale_blocks, scale, 1.0)
    pad_scr[0:CONV_HALO, :] = jnp.zeros((CONV_HALO, CONV_COLS), F32)
    pad_scr[CONV_HALO:, :] = x_ref[...].astype(F32)
    w = w_ref[...]
    for r0 in range(0, seq, CONV_ROWS):
        xe = pad_scr[r0:r0 + CONV_ROWS + CONV_HALO, :]
        acc = xe * w[0:1, :]
        for j in range(1, CONV_W):
            acc = pltpu.roll(acc, 1, axis=0) + xe * w[j:j + 1, :]
        acc = acc[CONV_HALO:]
        y = acc * _sigmoid(acc)
        if normalize:
            for s in range(CONV_COLS // LANES):
                ys = y[:, s * LANES:(s + 1) * LANES]
                ssq = jnp.sum(ys * ys, axis=1, keepdims=True)
                o_ref[r0:r0 + CONV_ROWS, s * LANES:(s + 1) * LANES] = (
                    ys * lax.rsqrt(ssq + EPS) * out_scale).astype(o_ref.dtype)
        else:
            o_ref[r0:r0 + CONV_ROWS, :] = y.astype(o_ref.dtype)


def conv_silu(proj3, conv_w, col0, width, normalize, scale=1.0, scale_width=0):
    batch, seq, _ = proj3.shape
    cb0 = col0 // CONV_COLS
    return pl.pallas_call(
        functools.partial(_conv_kernel, normalize=normalize, scale=scale,
                          scale_blocks=scale_width // CONV_COLS),
        grid=(batch, width // CONV_COLS),
        in_specs=[pl.BlockSpec((None, seq, CONV_COLS), lambda b, c: (b, 0, cb0 + c)),
                  pl.BlockSpec((CONV_W, CONV_COLS), lambda b, c: (0, cb0 + c))],
        out_specs=pl.BlockSpec((None, seq, CONV_COLS), lambda b, c: (b, 0, c)),
        out_shape=jax.ShapeDtypeStruct((batch, seq, width), BF16),
        scratch_shapes=[pltpu.VMEM((seq + CONV_HALO, CONV_COLS), F32)],
        compiler_params=_params("parallel", "parallel"),
        name="conv_silu",
    )(proj3, conv_w)


def _head_column(gb, lane_index):
    lane = lax.broadcasted_iota(jnp.int32, gb.shape, 1)
    return jnp.sum(jnp.where(lane == lane_index, gb, 0.0), axis=1, keepdims=True)


def _chunk_decay(gcum_col):
    r = lax.broadcasted_iota(jnp.int32, (CHUNK, CHUNK), 0)
    c = lax.broadcasted_iota(jnp.int32, (CHUNK, CHUNK), 1)
    gcum = jnp.broadcast_to(gcum_col, (CHUNK, LANES))
    gcum_rows = jnp.transpose(gcum)[:CHUNK, :]
    gdiff = jnp.where(c <= r, gcum[:, :CHUNK] - gcum_rows, 0.0)
    return gcum, gdiff, r, c


def _split3_dot(a_bf16, b_f32):
    b0 = b_f32.astype(BF16)
    r1 = b_f32 - b0.astype(F32)
    b1 = r1.astype(BF16)
    b2 = (r1 - b1.astype(F32)).astype(BF16)
    return _dot(a_bf16, b0) + _dot(a_bf16, b1) + _dot(a_bf16, b2)


def _lmat_kernel(k_ref, gb_ref, l_ref):
    hg = pl.program_id(1)
    gb = gb_ref[...]
    tiles = [(ci, hh) for ci in range(GDN_ROWS // CHUNK) for hh in range(DELTA_HEADS)]
    rows = lambda ci: slice(ci * CHUNK, (ci + 1) * CHUNK)
    kk = {}
    for ci, hh in tiles:
        k = k_ref[rows(ci), hh * LANES:(hh + 1) * LANES]
        kk[ci, hh] = _dot_nt(k, k)
    for ci, hh in tiles:
        head = hg * DELTA_HEADS + hh
        gb_c = gb[rows(ci)]
        _, gdiff, r, c = _chunk_decay(_head_column(gb_c, head))
        beta = _head_column(gb_c, C_HEADS + head)
        l_ref[hh, rows(ci), :] = jnp.where(r > c, beta * kk[ci, hh] * jnp.exp(gdiff), 0.0)


def gdn_lmat(k3, gb3):
    batch, seq, _ = k3.shape
    width = DELTA_HEADS * LANES
    groups = C_HEADS // DELTA_HEADS
    return pl.pallas_call(
        _lmat_kernel,
        grid=(batch, groups, seq // GDN_ROWS),
        in_specs=[pl.BlockSpec((None, GDN_ROWS, width), lambda b, h, i: (b, i, groups + h)),
                  pl.BlockSpec((None, GDN_ROWS, LANES), lambda b, h, i: (b, i, 0))],
        out_specs=pl.BlockSpec((None, DELTA_HEADS, GDN_ROWS, CHUNK), lambda b, h, i: (b, h, i, 0)),
        out_shape=jax.ShapeDtypeStruct((batch, C_HEADS, seq, CHUNK), F32),
        compiler_params=_params("parallel", "parallel", "parallel"),
        name="gdn_lmat",
    )(k3, gb3)


def _solve_kernel(l_ref, t_ref):
    sub = 8
    col = lax.broadcasted_iota(jnp.int32, (sub, SOLVE_LANES), 0)
    for i in range(CHUNK):
        acc = [jnp.where(col + g * sub == i, 1.0, 0.0) for g in range(CHUNK // sub)]
        for j in range(i):
            lij = l_ref[i, j:j + 1, :]
            for g in range(j // sub + 1):
                acc[g] = acc[g] - lij * t_ref[j, g * sub:(g + 1) * sub, :]
        t_ref[i] = jnp.concatenate(acc, axis=0)


def gdn_solve(lmat_t):
    g = lmat_t.shape[2]
    return pl.pallas_call(
        _solve_kernel,
        grid=(g // SOLVE_LANES,),
        in_specs=[pl.BlockSpec((CHUNK, CHUNK, SOLVE_LANES), lambda i: (0, 0, i))],
        out_specs=pl.BlockSpec((CHUNK, CHUNK, SOLVE_LANES), lambda i: (0, 0, i)),
        out_shape=jax.ShapeDtypeStruct(lmat_t.shape, F32),
        compiler_params=_params("parallel"),
        name="gdn_solve",
    )(lmat_t)


DELTA_HEADS = 4
DELTA_ROWS = 512


def _delta_kernel(q_ref, k_ref, v_ref, z_ref, gb_ref, t_ref, gain_ref, o_ref, *state_scrs):
    hg = pl.program_id(1)

    @pl.when(pl.program_id(2) == 0)
    def _():
        for scr in state_scrs:
            scr[...] = jnp.zeros(scr.shape, F32)

    states = [scr[...] for scr in state_scrs]
    gb = gb_ref[...]
    gain = gain_ref[...]
    n_chunks = DELTA_ROWS // CHUNK
    tiles = [(ci, hh) for ci in range(n_chunks) for hh in range(DELTA_HEADS)]
    rows = lambda ci: slice(ci * CHUNK, (ci + 1) * CHUNK)
    cols = lambda hh: slice(hh * LANES, (hh + 1) * LANES)

    pre = {}
    for ci, hh in tiles:
        head = hg * DELTA_HEADS + hh
        gb_c = gb[rows(ci)]
        gcum, gdiff, r, c = _chunk_decay(_head_column(gb_c, head))
        beta = _head_column(gb_c, C_HEADS + head)
        q = q_ref[rows(ci), cols(hh)]
        k = k_ref[rows(ci), cols(hh)]
        kf = k.astype(F32)
        eg = jnp.exp(gcum)
        g_last = gcum[CHUNK - 1:CHUNK, :]
        pre[ci, hh] = dict(
            q=q, k=k, eg=eg, e_last=jnp.exp(g_last),
            decay=jnp.where(c <= r, jnp.exp(gdiff), 0.0),
            k_dec=(kf * jnp.exp(g_last - gcum)).astype(BF16),
            rhs=jnp.concatenate([(v_ref[rows(ci), cols(hh)].astype(F32) * beta).astype(BF16),
                                 (kf * beta * eg).astype(BF16)], axis=1))
    uw = {t: _dot(t_ref[t[1], rows(t[0]), :].astype(BF16), pre[t]["rhs"]).astype(BF16)
          for t in tiles}
    a_intra = {t: (_dot_nt(pre[t]["q"], pre[t]["k"]) * pre[t]["decay"]).astype(BF16) for t in tiles}
    a_uw = {t: _dot(a_intra[t], uw[t]) for t in tiles}
    kd_uw = {t: _dot_tn(pre[t]["k_dec"], uw[t]) for t in tiles}
    lhs = {t: jnp.concatenate(
        [(pre[t]["q"].astype(F32) * pre[t]["eg"] - a_uw[t][:, LANES:]).astype(BF16),
         kd_uw[t][:, LANES:].astype(BF16)], axis=0) for t in tiles}

    for ci in range(n_chunks):
        for hh in range(DELTA_HEADS):
            t = (ci, hh)
            state = states[hh]
            prod = _dot(lhs[t], state.astype(BF16))
            o = prod[:CHUNK] + a_uw[t][:, :LANES]
            states[hh] = state * pre[t]["e_last"] - prod[CHUNK:] + kd_uw[t][:, :LANES]
            z = z_ref[rows(ci), cols(hh)].astype(F32)
            o_ref[rows(ci), cols(hh)] = (_rms_rows(o, gain) * (z * _sigmoid(z))).astype(o_ref.dtype)
    for scr, state in zip(state_scrs, states):
        scr[...] = state


def gdn_delta(qk3, v3, proj3, gb3, tinv, out_gain):
    batch, seq, _ = v3.shape
    width = DELTA_HEADS * LANES
    groups = C_HEADS // DELTA_HEADS
    zcol = 3 * C_WIDTH // width
    blk = (None, DELTA_ROWS, width)
    return pl.pallas_call(
        _delta_kernel,
        grid=(batch, groups, seq // DELTA_ROWS),
        in_specs=[pl.BlockSpec(blk, lambda b, h, i: (b, i, h)),
                  pl.BlockSpec(blk, lambda b, h, i: (b, i, groups + h)),
                  pl.BlockSpec(blk, lambda b, h, i: (b, i, h)),
                  pl.BlockSpec(blk, lambda b, h, i: (b, i, zcol + h)),
                  pl.BlockSpec((None, DELTA_ROWS, LANES), lambda b, h, i: (b, i, 0)),
                  pl.BlockSpec((None, DELTA_HEADS, DELTA_ROWS, CHUNK), lambda b, h, i: (b, h, i, 0)),
                  pl.BlockSpec((1, LANES), lambda b, h, i: (0, 0))],
        out_specs=pl.BlockSpec(blk, lambda b, h, i: (b, i, h)),
        out_shape=jax.ShapeDtypeStruct((batch, seq, C_WIDTH), BF16),
        scratch_shapes=[pltpu.VMEM((C_DK, LANES), F32) for _ in range(DELTA_HEADS)],
        compiler_params=_params("parallel", "parallel", "arbitrary"),
        name="gdn_delta",
    )(qk3, qk3, v3, proj3, gb3, tinv, out_gain.reshape(1, LANES))


def rope_tables(seq):
    half = HEAD_DIM // 2
    inv_freq = 1.0 / (ROPE_THETA ** (jnp.arange(0, HEAD_DIM, 2, dtype=F32) / HEAD_DIM))
    ang = jnp.arange(seq, dtype=F32)[:, None] * inv_freq[None, :]
    cos, sin = jnp.cos(ang), jnp.sin(ang)
    cos_t = jnp.tile(cos, (1, LANES // half))
    sin_t = jnp.tile(jnp.concatenate([-sin, sin], axis=1), (1, LANES // HEAD_DIM))
    return cos_t, sin_t


def even_layer(x, batch, seq, layer, attn_norm, w_in, a_q_gain, a_k_gain, b_q_gain, b_k_gain,
               lq1, lk1, lq2, lk2, b_out_gain, w_out, ffn_gain, router):
    lam_init = 0.8 - 0.6 * math.exp(-0.3 * layer)
    scale = HEAD_DIM ** -0.5
    tile = lambda g, s: jnp.tile(g.astype(F32) * s, A_WIDTH // HEAD_DIM)
    gains = jnp.stack([tile(a_q_gain, scale), tile(a_k_gain, 1.0),
                       tile(b_q_gain, scale), tile(b_k_gain, 1.0)])
    cos_t, sin_t = rope_tables(seq)
    qk, vv = even_in_proj(x, attn_norm, w_in.astype(BF16), gains, cos_t, sin_t, seq)
    oa = dilated_attention(qk, vv, batch, seq)
    ob = diff_attention(qk, vv, lq1.astype(F32), lk1.astype(F32), lq2.astype(F32), lk2.astype(F32),
                        b_out_gain, lam_init, batch, seq)
    return outproj_router(x, oa, 0, ob, 0, w_out.astype(BF16), ffn_gain, *router)


def odd_layer(x, batch, seq, attn_norm, w_in, conv_w, a_log, dt_bias, out_gain, w_out, ffn_gain, router):
    n = batch * seq
    w_pad = jnp.pad(w_in, ((0, 0), (0, LANES - 2 * C_HEADS))).astype(BF16)
    proj = norm_matmul(x, attn_norm, w_pad, 1408)
    gb = gdn_gates(proj, a_log, dt_bias)
    proj3 = proj.reshape(batch, seq, proj.shape[1])
    conv_w = conv_w.astype(F32)
    qk3 = conv_silu(proj3, conv_w, 0, 2 * C_WIDTH, True, C_DK ** -0.5, C_WIDTH)
    v3 = conv_silu(proj3, conv_w, 2 * C_WIDTH, C_WIDTH, False)
    gb3 = gb.reshape(batch, seq, LANES)
    lmat = gdn_lmat(qk3, gb3)
    g = batch * C_HEADS * (seq // CHUNK)
    lmat_t = jnp.transpose(lmat.reshape(g, CHUNK, CHUNK), (1, 2, 0))
    tinv = jnp.transpose(gdn_solve(lmat_t), (2, 0, 1)).reshape(batch, C_HEADS, seq, CHUNK)
    mix = gdn_delta(qk3, v3, proj3, gb3, tinv, out_gain).reshape(n, C_WIDTH)
    return outproj_router(x, mix, 0, mix, 1, w_out.astype(BF16), ffn_gain, *router)


def kernel(x, even_attn_norm, even_w_in, a_q_norm, a_k_norm, b_q_norm, b_k_norm, b_lambda_q1, b_lambda_k1, b_lambda_q2, b_lambda_k2, b_out_norm, even_w_out, odd_attn_norm, odd_w_in, odd_conv_w, odd_a_log, odd_dt_bias, odd_out_norm, odd_w_out, ffn_norm, router_group_w, router_group_b, router_expert_w, router_expert_b, expert_w_gate, expert_w_up, expert_w_down):
    batch, seq, d = x.shape
    depth = ffn_norm.shape[0]
    xf = x.reshape(batch * seq, d).astype(F32)
    for layer in range(depth):
        i = layer // 2
        router = router_tables(router_group_w[layer], router_group_b[layer],
                               router_expert_w[layer], router_expert_b[layer])
        if layer % 2 == 0:
            xf, h, route = even_layer(
                xf, batch, seq, layer, even_attn_norm[i], even_w_in[i], a_q_norm[i], a_k_norm[i],
                b_q_norm[i], b_k_norm[i], b_lambda_q1[i], b_lambda_k1[i], b_lambda_q2[i],
                b_lambda_k2[i], b_out_norm[i], even_w_out[i], ffn_norm[layer], router)
        else:
            xf, h, route = odd_layer(
                xf, batch, seq, odd_attn_norm[i], odd_w_in[i], odd_conv_w[i], odd_a_log[i],
                odd_dt_bias[i], odd_out_norm[i], odd_w_out[i], ffn_norm[layer], router)
        xf = moe_layer(xf, h, route, layer, expert_w_gate, expert_w_up, expert_w_down)
    return xf.reshape(batch, seq, d).astype(x.dtype)
```

```python
import functools
import math

import jax
import jax.numpy as jnp
from jax import lax
from jax.experimental import pallas as pl
from jax.experimental.pallas import tpu as pltpu

F32 = jnp.float32
BF16 = jnp.bfloat16

D_MODEL = 1024
HEAD_DIM = 64
ROPE_THETA = 10000.0
A_HEADS = 8
A_BRANCHES = ((128, 1), (512, 4), (2048, 16))
A_BLOCK = 128
B_HEADS = 4
A_WIDTH = 512
B_WIDTH = 512
EVEN_IN = 3 * A_WIDTH + 3 * B_WIDTH
C_HEADS = 8
C_DK = 128
C_WIDTH = 1024
CONV_W = 4
CHUNK = 64
N_GROUPS = 4
EXPERTS_PER_GROUP = 8
N_EXPERTS = 32
TOP_K = 2
EXPERT_FF = 512
EPS = 1e-6
NEG_INF = -1e30

LANES = 128
VMEM_LIMIT = 48 * 1024 * 1024

ROW_TILE = 512
MOE_TILE = 512
DIFF_TQ = 512
GDN_ROWS = 512
SOLVE_LANES = 128


def _params(*sem):
    return pltpu.CompilerParams(dimension_semantics=sem, vmem_limit_bytes=VMEM_LIMIT)


def _dot(a, b):
    return jnp.dot(a, b, preferred_element_type=F32)


def _dot_nt(a, b):
    return lax.dot_general(a, b, (((1,), (1,)), ((), ())), preferred_element_type=F32)


def _dot_tn(a, b):
    return lax.dot_general(a, b, (((0,), (0,)), ((), ())), preferred_element_type=F32)


def _split_dot(a_f32, b_bf16):
    hi = a_f32.astype(BF16)
    lo = (a_f32 - hi.astype(F32)).astype(BF16)
    return _dot(hi, b_bf16) + _dot(lo, b_bf16)


def _sigmoid(x):
    return 1.0 / (1.0 + jnp.exp(-x))


def _rms_rows(xf, gain):
    ms = jnp.mean(xf * xf, axis=-1, keepdims=True)
    return xf * lax.rsqrt(ms + EPS) * gain


def _even_in_kernel(x_ref, g_ref, w_ref, hg_ref, cos_ref, sin_ref, qk_ref, v_ref):
    h = _rms_rows(x_ref[...], g_ref[...]).astype(BF16)
    lane = lax.broadcasted_iota(jnp.int32, (1, LANES), 1)
    first_half = (lane % HEAD_DIM) < (HEAD_DIM // 2)
    r = lax.broadcasted_iota(jnp.int32, (LANES, LANES), 0) // HEAD_DIM
    c = lax.broadcasted_iota(jnp.int32, (LANES, LANES), 1) // HEAD_DIM
    same_head = jnp.where(r == c, 1.0, 0.0).astype(BF16)
    cos = cos_ref[...]
    sin = sin_ref[...]
    for chunk in range(EVEN_IN // A_WIDTH):
        y = _dot(h, w_ref[:, chunk * A_WIDTH:(chunk + 1) * A_WIDTH])
        if chunk % 3 == 2:
            v_ref[:, (chunk // 3) * A_WIDTH:(chunk // 3 + 1) * A_WIDTH] = y.astype(v_ref.dtype)
            continue
        group = chunk - chunk // 3
        for s in range(A_WIDTH // LANES):
            sl = slice(s * LANES, (s + 1) * LANES)
            x = y[:, sl]
            ssq = _split_dot(x * x, same_head)
            z = x * lax.rsqrt(ssq * (1.0 / HEAD_DIM) + EPS) * hg_ref[group:group + 1, sl]
            partner = jnp.where(first_half,
                                pltpu.roll(z, LANES - HEAD_DIM // 2, axis=1),
                                pltpu.roll(z, HEAD_DIM // 2, axis=1))
            qk_ref[:, group * A_WIDTH + s * LANES:group * A_WIDTH + (s + 1) * LANES] = (
                z * cos + partner * sin).astype(qk_ref.dtype)


def even_in_proj(x, gain, w_bf16, head_gains, cos_t, sin_t, seq):
    n, d = x.shape
    tiles_per_seq = seq // ROW_TILE
    return pl.pallas_call(
        _even_in_kernel,
        grid=(n // ROW_TILE,),
        in_specs=[pl.BlockSpec((ROW_TILE, d), lambda i: (i, 0)),
                  pl.BlockSpec((1, d), lambda i: (0, 0)),
                  pl.BlockSpec((d, EVEN_IN), lambda i: (0, 0)),
                  pl.BlockSpec((4, A_WIDTH), lambda i: (0, 0)),
                  pl.BlockSpec((ROW_TILE, LANES), lambda i: (i % tiles_per_seq, 0)),
                  pl.BlockSpec((ROW_TILE, LANES), lambda i: (i % tiles_per_seq, 0))],
        out_specs=[pl.BlockSpec((ROW_TILE, 4 * A_WIDTH), lambda i: (i, 0)),
                   pl.BlockSpec((ROW_TILE, 2 * A_WIDTH), lambda i: (i, 0))],
        out_shape=[jax.ShapeDtypeStruct((n, 4 * A_WIDTH), BF16),
                   jax.ShapeDtypeStruct((n, 2 * A_WIDTH), BF16)],
        compiler_params=_params("parallel"),
        name="even_in_proj",
    )(x, gain.reshape(1, d), w_bf16, head_gains, cos_t, sin_t)


DILATED_GROUP = 4


def _dilated_kernel(q_ref, k_ref, v_ref, o_ref, qf, kf, vf, of, lf, bias):
    seq = q_ref.shape[0]
    nb = A_BLOCK
    qf[...] = q_ref[...].astype(F32)
    kf[...] = k_ref[...].astype(F32)
    vf[...] = v_ref[...].astype(F32)
    lane = lax.broadcasted_iota(jnp.int32, (1, LANES), 1)
    low = lane < HEAD_DIM
    qi = lax.broadcasted_iota(jnp.int32, (2 * nb, 2 * nb), 0) % nb
    kj = lax.broadcasted_iota(jnp.int32, (2 * nb, 2 * nb), 1)
    band = (kj >= qi) & (kj <= qi + nb)
    bias[0] = jnp.where(band & (kj >= nb), 0.0, NEG_INF)
    bias[1] = jnp.where(band, 0.0, NEG_INF)
    zero = jnp.zeros((), BF16)
    for idx, (window, d) in enumerate(A_BRANCHES):
        assert window // d == nb
        nblk = seq // (d * nb)
        first, last = idx == 0, idx == len(A_BRANCHES) - 1

        group = min(DILATED_GROUP, nblk)
        nres = DILATED_GROUP // group
        tiles = [(a, j) for a in range(nres) for j in range(group)]

        def blocks(t, carry, d=d, nblk=nblk, first=first, last=last, group=group, nres=nres, tiles=tiles):
            r0 = (t // (nblk // group)) * nres
            c0 = (t % (nblk // group)) * group

            def rows(a, c):
                return pl.ds(r0 + a + c * (nb * d), nb, stride=d)

            kb = {(a, j): kf[rows(a, jnp.maximum(c0 - 1 + j, 0)), :].astype(BF16)
                  for a in range(nres) for j in range(group + 1)}
            vb = {(a, j): vf[rows(a, jnp.maximum(c0 - 1 + j, 0)), :].astype(BF16)
                  for a in range(nres) for j in range(group + 1)}
            qb = {(a, j): qf[rows(a, c0 + j), :].astype(BF16) for a, j in tiles}
            s = {(a, j): _dot_nt(
                jnp.concatenate([jnp.where(low, qb[a, j], zero), jnp.where(low, zero, qb[a, j])], axis=0),
                jnp.concatenate([kb[a, j], kb[a, j + 1]], axis=0))
                + bias[jnp.minimum(c0 + j, 1)] for a, j in tiles}
            m = {t_: jnp.max(s[t_], axis=1, keepdims=True) for t_ in tiles}
            p = {t_: jnp.exp(s[t_] - m[t_]) for t_ in tiles}
            den = {t_: jnp.sum(p[t_], axis=1, keepdims=True) for t_ in tiles}
            o2 = {(a, j): _dot(p[a, j].astype(BF16), jnp.concatenate([vb[a, j], vb[a, j + 1]], axis=0))
                  / den[a, j] for a, j in tiles}
            for a, j in tiles:
                cur = rows(a, c0 + j)
                lse2 = m[a, j] + jnp.log(den[a, j])
                o = jnp.where(low, o2[a, j][:nb], o2[a, j][nb:])
                lse = jnp.where(low, lse2[:nb], lse2[nb:])
                if not first:
                    o_prev = of[cur, :]
                    l_prev = lf[cur, :]
                    top = jnp.maximum(lse, l_prev)
                    wa = jnp.exp(lse - top)
                    wb = jnp.exp(l_prev - top)
                    tot = wa + wb
                    o = (wa * o + wb * o_prev) / tot
                    lse = top + jnp.log(tot)
                of[cur, :] = o
                if not last:
                    lf[cur, :] = lse
            return carry

        lax.fori_loop(0, seq // (nb * DILATED_GROUP), blocks, 0)
    o_ref[...] = of[...].astype(o_ref.dtype)


def dilated_attention(qk, vv, batch, seq):
    qk3 = qk.reshape(batch, seq, 2048)
    pj3 = vv.reshape(batch, seq, 2 * A_WIDTH)
    slabs = A_WIDTH // LANES
    blk = (None, seq, LANES)
    out = pl.pallas_call(
        _dilated_kernel,
        grid=(batch, slabs),
        in_specs=[pl.BlockSpec(blk, lambda b, s: (b, 0, s)),
                  pl.BlockSpec(blk, lambda b, s: (b, 0, slabs + s)),
                  pl.BlockSpec(blk, lambda b, s: (b, 0, s))],
        out_specs=pl.BlockSpec(blk, lambda b, s: (b, 0, s)),
        out_shape=jax.ShapeDtypeStruct((batch, seq, A_WIDTH), BF16),
        scratch_shapes=[pltpu.VMEM((seq, LANES), F32) for _ in range(5)]
        + [pltpu.VMEM((2, 2 * A_BLOCK, 2 * A_BLOCK), F32)],
        compiler_params=_params("parallel", "parallel"),
        name="dilated_attention",
    )(qk3, qk3, pj3)
    return out.reshape(batch * seq, A_WIDTH)


def _diff_attn_kernel(q_ref, k_ref, v_ref, lq1_ref, lk1_ref, lq2_ref, lk2_ref, g_ref, o_ref,
                      m_scr, l_scr, acc_scr, *, lam_init):
    h = pl.program_id(1)
    tq = DIFF_TQ
    seq = q_ref.shape[0]
    lane = lax.broadcasted_iota(jnp.int32, (1, LANES), 1)
    low = lane < HEAD_DIM
    zero = jnp.zeros((), BF16)
    d1 = jnp.sum(lq1_ref[pl.ds(h, 1), :] * lk1_ref[pl.ds(h, 1), :], axis=1, keepdims=True)
    d2 = jnp.sum(lq2_ref[pl.ds(h, 1), :] * lk2_ref[pl.ds(h, 1), :], axis=1, keepdims=True)
    lam = jnp.exp(d1) - jnp.exp(d2) + lam_init

    def query_block(i, carry):
        rows = pl.ds(pl.multiple_of(i * tq, tq), tq)
        q = q_ref[rows, :]
        q2 = jnp.concatenate([jnp.where(low, q, zero), jnp.where(low, zero, q)], axis=0)
        m_scr[...] = jnp.full(m_scr.shape, NEG_INF, F32)
        l_scr[...] = jnp.zeros(l_scr.shape, F32)
        acc_scr[...] = jnp.zeros(acc_scr.shape, F32)

        def step(j, masked):
            start = pl.multiple_of(j * tq, tq)
            k = k_ref[pl.ds(start, tq), :]
            v = v_ref[pl.ds(start, tq), :]
            s = _dot_nt(q2, k)
            if masked:
                qi = lax.broadcasted_iota(jnp.int32, (2 * tq, tq), 0) % tq
                kj = lax.broadcasted_iota(jnp.int32, (2 * tq, tq), 1)
                s = jnp.where(kj <= qi, s, NEG_INF)
            m_old = m_scr[...]
            m_new = jnp.maximum(m_old, jnp.max(s, axis=1, keepdims=True))
            alpha = jnp.exp(m_old - m_new)
            p = jnp.exp(s - jnp.concatenate([m_new] * (tq // LANES), axis=1))
            l_scr[...] = alpha * l_scr[...] + jnp.sum(p, axis=1, keepdims=True)
            acc_scr[...] = alpha * acc_scr[...] + _dot(p.astype(BF16), v)
            m_scr[...] = m_new

        def body(j, c):
            step(2 * j, False)
            step(2 * j + 1, False)
            return c

        lax.fori_loop(0, i // 2, body, 0)

        @pl.when(i % 2 == 1)
        def _():
            step(i - 1, False)

        step(i, True)

        acc = acc_scr[...] / l_scr[...]
        o = acc[:tq] - lam * acc[tq:]
        o_ref[rows, :] = (_rms_rows(o, g_ref[...]) * (1.0 - lam_init)).astype(o_ref.dtype)
        return carry

    lax.fori_loop(0, seq // tq, query_block, 0)


def diff_attention(qk, vv, lq1, lk1, lq2, lk2, out_gain, lam_init, batch, seq):
    n = batch * seq
    qk3 = qk.reshape(batch, seq, 2048)
    pj3 = vv.reshape(batch, seq, 2 * A_WIDTH)
    qcol = 2 * A_WIDTH // LANES
    kcol = (2 * A_WIDTH + B_WIDTH) // LANES
    vcol = A_WIDTH // LANES
    small = pl.BlockSpec((B_HEADS, HEAD_DIM), lambda b, h: (0, 0))
    out = pl.pallas_call(
        functools.partial(_diff_attn_kernel, lam_init=lam_init),
        grid=(batch, B_HEADS),
        in_specs=[pl.BlockSpec((None, seq, LANES), lambda b, h: (b, 0, qcol + h)),
                  pl.BlockSpec((None, seq, LANES), lambda b, h: (b, 0, kcol + h)),
                  pl.BlockSpec((None, seq, LANES), lambda b, h: (b, 0, vcol + h)),
                  small, small, small, small,
                  pl.BlockSpec((1, LANES), lambda b, h: (0, 0))],
        out_specs=pl.BlockSpec((None, seq, LANES), lambda b, h: (b, 0, h)),
        out_shape=jax.ShapeDtypeStruct((batch, seq, B_WIDTH), BF16),
        scratch_shapes=[pltpu.VMEM((2 * DIFF_TQ, LANES), F32),
                        pltpu.VMEM((2 * DIFF_TQ, LANES), F32),
                        pltpu.VMEM((2 * DIFF_TQ, LANES), F32)],
        compiler_params=_params("parallel", "parallel"),
        name="diff_attention",
    )(qk3, qk3, pj3, lq1, lk1, lq2, lk2, out_gain.reshape(1, LANES))
    return out.reshape(n, B_WIDTH)


ROUTE_E1, ROUTE_E2, ROUTE_W1, ROUTE_W2 = 0, 1, 2, 3
EXPERT_LANE0 = 8


TOKEN_ROWS = D_MODEL // LANES


def _store_token_tiles(ref, rows):
    t = rows.shape[0]
    for s in range(TOKEN_ROWS):
        ref[pl.ds(s, t, stride=TOKEN_ROWS), :] = rows[:, s * LANES:(s + 1) * LANES]


def _load_token_tiles(ref, t):
    return jnp.concatenate([ref[pl.ds(s, t, stride=TOKEN_ROWS), :] for s in range(TOKEN_ROWS)], axis=1)


def _outproj_router_kernel(x_ref, a_ref, b_ref, wa_ref, wb_ref, g_ref, rhi_ref, rlo_ref, rb_ref,
                           xo_ref, h_ref, rt_ref):
    y = x_ref[...] + _dot(a_ref[...], wa_ref[...]) + _dot(b_ref[...], wb_ref[...])
    xo_ref[...] = y
    hn = _rms_rows(y, g_ref[...])
    _store_token_tiles(h_ref, hn)
    hi = hn.astype(BF16)
    lo = (hn - hi.astype(F32)).astype(BF16)
    both = _dot(hi, jnp.concatenate([rhi_ref[...], rlo_ref[...]], axis=1))
    logits = (both[:, :LANES] + _dot(lo, rhi_ref[...]) + both[:, LANES:]) + rb_ref[...]

    lane = lax.broadcasted_iota(jnp.int32, logits.shape, 1)
    big = jnp.int32(LANES)
    is_group = lane < N_GROUPS
    gl = jnp.where(is_group, logits, NEG_INF)
    gmax = jnp.max(gl, axis=1, keepdims=True)
    grp = jnp.min(jnp.where(is_group & (gl == gmax), lane, big), axis=1, keepdims=True)
    g_w = 1.0 / jnp.sum(jnp.where(is_group, jnp.exp(gl - gmax), 0.0), axis=1, keepdims=True)
    first = EXPERT_LANE0 + grp * EXPERTS_PER_GROUP
    in_group = (lane >= first) & (lane < first + EXPERTS_PER_GROUP)
    el = jnp.where(in_group, logits, NEG_INF)
    v1 = jnp.max(el, axis=1, keepdims=True)
    i1 = jnp.min(jnp.where(in_group & (el == v1), lane, big), axis=1, keepdims=True)
    rest = in_group & (lane != i1)
    el2 = jnp.where(rest, logits, NEG_INF)
    v2 = jnp.max(el2, axis=1, keepdims=True)
    i2 = jnp.min(jnp.where(rest & (el2 == v2), lane, big), axis=1, keepdims=True)
    e2w = jnp.exp(v2 - v1)
    w1 = g_w / (1.0 + e2w)
    w2 = g_w * e2w / (1.0 + e2w)
    e1 = (i1 - EXPERT_LANE0).astype(F32)
    e2 = (i2 - EXPERT_LANE0).astype(F32)
    rt_ref[...] = jnp.where(lane == ROUTE_E1, e1,
                            jnp.where(lane == ROUTE_E2, e2,
                                      jnp.where(lane == ROUTE_W1, w1,
                                                jnp.where(lane == ROUTE_W2, w2, 0.0))))


def outproj_router(x, mix_a, blk_a, mix_b, blk_b, w_out_bf16, ffn_gain, r_hi, r_lo, r_bias):
    n, d = x.shape
    half = w_out_bf16.shape[0] // 2
    row = lambda i: (i, 0)
    const = lambda i: (0, 0)
    return pl.pallas_call(
        _outproj_router_kernel,
        grid=(n // ROW_TILE,),
        in_specs=[pl.BlockSpec((ROW_TILE, d), row),
                  pl.BlockSpec((ROW_TILE, half), lambda i: (i, blk_a)),
                  pl.BlockSpec((ROW_TILE, half), lambda i: (i, blk_b)),
                  pl.BlockSpec((half, d), lambda i: (0, 0)),
                  pl.BlockSpec((half, d), lambda i: (1, 0)),
                  pl.BlockSpec((1, d), const),
                  pl.BlockSpec((d, LANES), const),
                  pl.BlockSpec((d, LANES), const),
                  pl.BlockSpec((1, LANES), const)],
        out_specs=[pl.BlockSpec((ROW_TILE, d), row),
                   pl.BlockSpec((ROW_TILE * TOKEN_ROWS, LANES), row),
                   pl.BlockSpec((ROW_TILE, LANES), row)],
        out_shape=[jax.ShapeDtypeStruct((n, d), F32),
                   jax.ShapeDtypeStruct((n * TOKEN_ROWS, LANES), F32),
                   jax.ShapeDtypeStruct((n, LANES), F32)],
        compiler_params=_params("parallel"),
        name="outproj_router",
    )(x, mix_a, mix_b, w_out_bf16, w_out_bf16, ffn_gain.reshape(1, d), r_hi, r_lo, r_bias)


def _rank_kernel(rt_ref, tri_ref, rk_ref, cnt_ref, carry_scr):
    @pl.when(pl.program_id(0) == 0)
    def _():
        carry_scr[...] = jnp.zeros(carry_scr.shape, F32)

    rt = rt_ref[...]
    lane = lax.broadcasted_iota(jnp.int32, rt.shape, 1)
    e1 = rt[:, ROUTE_E1:ROUTE_E1 + 1].astype(jnp.int32)
    e2 = rt[:, ROUTE_E2:ROUTE_E2 + 1].astype(jnp.int32)
    oh1 = lane == e1
    oh2 = lane == e2
    both = jnp.where(oh1 | oh2, 1.0, 0.0)
    before = _dot(tri_ref[...], both.astype(BF16)) + carry_scr[0:1, :]
    r1 = jnp.sum(jnp.where(oh1, before, 0.0), axis=1, keepdims=True)
    r2 = jnp.sum(jnp.where(oh2, before, 0.0), axis=1, keepdims=True)
    rk_ref[...] = jnp.where(lane == 0, r1, jnp.where(lane == 1, r2, 0.0))
    total = carry_scr[0:1, :] + jnp.sum(both, axis=0, keepdims=True)
    carry_scr[...] = jnp.broadcast_to(total, carry_scr.shape)
    cnt_ref[...] = jnp.broadcast_to(total, cnt_ref.shape)


def rank_slots(route):
    n = route.shape[0]
    tri = jnp.tril(jnp.ones((ROW_TILE, ROW_TILE), BF16), k=-1)
    return pl.pallas_call(
        _rank_kernel,
        grid=(n // ROW_TILE,),
        in_specs=[pl.BlockSpec((ROW_TILE, LANES), lambda i: (i, 0)),
                  pl.BlockSpec((ROW_TILE, ROW_TILE), lambda i: (0, 0))],
        out_specs=[pl.BlockSpec((ROW_TILE, LANES), lambda i: (i, 0)),
                   pl.BlockSpec((8, LANES), lambda i: (0, 0))],
        out_shape=[jax.ShapeDtypeStruct((n, LANES), F32),
                   jax.ShapeDtypeStruct((8, LANES), F32)],
        scratch_shapes=[pltpu.VMEM((8, LANES), F32)],
        compiler_params=_params("arbitrary"),
        name="moe_rank",
    )(route, tri)


DISPATCH_TOKENS = 512
DMA_UNROLL = 8


def _dispatch_kernel(dest_ref, zblk_ref, nz_ref, h_ref, xs_ref, zero_scr, sem, zsem):
    base = pl.program_id(0) * DISPATCH_TOKENS

    @pl.when(pl.program_id(0) == 0)
    def _():
        zero_scr[...] = jnp.zeros(zero_scr.shape, F32)

        def zero_copy(i):
            rows = MOE_TILE * TOKEN_ROWS
            start = pl.multiple_of(zblk_ref[i] * rows, rows)
            return pltpu.make_async_copy(zero_scr, xs_ref.at[pl.ds(start, rows)], zsem)

        def zissue(i, carry):
            zero_copy(i).start()
            return carry

        def zdrain(i, carry):
            zero_copy(i).wait()
            return carry

        lax.fori_loop(0, nz_ref[0], zissue, 0)
        lax.fori_loop(0, nz_ref[0], zdrain, 0)

    def tile_copy(i, k):
        src = pl.ds(pl.multiple_of(i * TOKEN_ROWS, TOKEN_ROWS), TOKEN_ROWS)
        dst = pl.ds(pl.multiple_of(dest_ref[TOP_K * (base + i) + k], TOKEN_ROWS), TOKEN_ROWS)
        return pltpu.make_async_copy(h_ref.at[src], xs_ref.at[dst], sem)

    def issue(i, carry):
        for k in range(TOP_K):
            tile_copy(i, k).start(priority=k % 2)
        return carry

    lax.fori_loop(0, DISPATCH_TOKENS, issue, 0, unroll=DMA_UNROLL)

    for k in range(TOP_K):
        pltpu.make_async_copy(h_ref, xs_ref.at[pl.ds(0, DISPATCH_TOKENS * TOKEN_ROWS)], sem).wait()


def dispatch_rows(h, dest, zero_blocks, n_zero, n_slots):
    n = h.shape[0] // TOKEN_ROWS
    return pl.pallas_call(
        _dispatch_kernel,
        grid_spec=pltpu.PrefetchScalarGridSpec(
            num_scalar_prefetch=3,
            grid=(n // DISPATCH_TOKENS,),
            in_specs=[pl.BlockSpec((DISPATCH_TOKENS * TOKEN_ROWS, LANES), lambda i, dst, zb, nz: (i, 0))],
            out_specs=pl.BlockSpec(memory_space=pl.ANY),
            scratch_shapes=[pltpu.VMEM((MOE_TILE * TOKEN_ROWS, LANES), F32),
                            pltpu.SemaphoreType.DMA(()),
                            pltpu.SemaphoreType.DMA(())]),
        out_shape=jax.ShapeDtypeStruct((n_slots * TOKEN_ROWS, LANES), F32),
        compiler_params=_params("arbitrary"),
        name="moe_dispatch",
    )(dest, zero_blocks, n_zero, h)


def _expert_kernel(be_ref, nb_ref, x_ref, wg_ref, wu_ref, wd_ref, y_ref, wg_b, wu_b, wd_b):
    i = pl.program_id(0)

    @pl.when((i == 0) | (be_ref[i] != be_ref[jnp.maximum(i - 1, 0)]))
    def _():
        wg_b[...] = wg_ref[...].astype(BF16)
        wu_b[...] = wu_ref[...].astype(BF16)
        wd_b[...] = wd_ref[...].astype(BF16)

    @pl.when(i < nb_ref[0])
    def _():
        xb = _load_token_tiles(x_ref, MOE_TILE).astype(BF16)
        gate = _dot(xb, wg_b[...])
        up = _dot(xb, wu_b[...])
        hidden = (gate * _sigmoid(gate) * up).astype(BF16)
        _store_token_tiles(y_ref, _dot(hidden, wd_b[...]))

    @pl.when(i >= nb_ref[0])
    def _():
        y_ref[...] = jnp.zeros(y_ref.shape, F32)


def expert_ffn(xs, block_expert, n_used, layer, w_gate, w_up, w_down):
    p = xs.shape[0] // TOKEN_ROWS
    d, ff = w_gate.shape[2], w_gate.shape[3]
    slot_block = pl.BlockSpec((MOE_TILE * TOKEN_ROWS, LANES), lambda i, be, nb: (i, 0))
    return pl.pallas_call(
        _expert_kernel,
        grid_spec=pltpu.PrefetchScalarGridSpec(
            num_scalar_prefetch=2,
            grid=(p // MOE_TILE,),
            in_specs=[slot_block,
                      pl.BlockSpec((None, None, d, ff), lambda i, be, nb: (layer, be[i], 0, 0)),
                      pl.BlockSpec((None, None, d, ff), lambda i, be, nb: (layer, be[i], 0, 0)),
                      pl.BlockSpec((None, None, ff, d), lambda i, be, nb: (layer, be[i], 0, 0))],
            out_specs=slot_block,
            scratch_shapes=[pltpu.VMEM((d, ff), BF16), pltpu.VMEM((d, ff), BF16),
                            pltpu.VMEM((ff, d), BF16)]),
        out_shape=jax.ShapeDtypeStruct(xs.shape, F32),
        compiler_params=_params("arbitrary"),
        name="moe_experts",
    )(block_expert, n_used, xs, w_gate, w_up, w_down)


COMBINE_TOKENS = 512


def _combine_kernel(dest_ref, x_ref, rt_ref, y_ref, o_ref, buf00, buf01, buf10, buf11, sem0, sem1):
    step = pl.program_id(0)
    last = pl.num_programs(0) - 1
    bufs = ((buf00, buf01), (buf10, buf11))
    sems = (sem0, sem1)

    def issue(for_step, pair):
        base = for_step * COMBINE_TOKENS

        def tile_copy(i, k):
            src = pl.ds(pl.multiple_of(dest_ref[TOP_K * (base + i) + k], TOKEN_ROWS), TOKEN_ROWS)
            dst = pl.ds(pl.multiple_of(i * TOKEN_ROWS, TOKEN_ROWS), TOKEN_ROWS)
            return pltpu.make_async_copy(y_ref.at[src], bufs[pair][k].at[dst], sems[pair])

        def body(i, carry):
            for k in range(TOP_K):
                tile_copy(i, k).start(priority=k % 2)
            return carry

        lax.fori_loop(0, COMBINE_TOKENS, body, 0, unroll=DMA_UNROLL)

    def consume(pair):
        for k in range(TOP_K):
            pltpu.make_async_copy(y_ref.at[pl.ds(0, COMBINE_TOKENS * TOKEN_ROWS)],
                                  bufs[pair][k], sems[pair]).wait()
        rt = rt_ref[...]
        w1 = rt[:, ROUTE_W1:ROUTE_W1 + 1]
        w2 = rt[:, ROUTE_W2:ROUTE_W2 + 1]
        o_ref[...] = x_ref[...] + (w1 * _load_token_tiles(bufs[pair][0], COMBINE_TOKENS)
                                   + w2 * _load_token_tiles(bufs[pair][1], COMBINE_TOKENS))

    @pl.when(step == 0)
    def _():
        issue(0, 0)

    for pair in range(2):
        @pl.when(step % 2 == pair)
        def _(pair=pair):
            @pl.when(step < last)
            def _():
                issue(step + 1, 1 - pair)

            consume(pair)


def combine_rows(x, route, y, dest):
    n, d = x.shape
    return pl.pallas_call(
        _combine_kernel,
        grid_spec=pltpu.PrefetchScalarGridSpec(
            num_scalar_prefetch=1,
            grid=(n // COMBINE_TOKENS,),
            in_specs=[pl.BlockSpec((COMBINE_TOKENS, d), lambda i, dst: (i, 0)),
                      pl.BlockSpec((COMBINE_TOKENS, LANES), lambda i, dst: (i, 0)),
                      pl.BlockSpec(memory_space=pl.ANY)],
            out_specs=pl.BlockSpec((COMBINE_TOKENS, d), lambda i, dst: (i, 0)),
            scratch_shapes=[pltpu.VMEM((COMBINE_TOKENS * TOKEN_ROWS, LANES), F32) for _ in range(4)]
            + [pltpu.SemaphoreType.DMA(()), pltpu.SemaphoreType.DMA(())]),
        out_shape=jax.ShapeDtypeStruct((n, d), F32),
        compiler_params=_params("arbitrary"),
        name="moe_combine",
    )(dest, x, route, y)


def moe_layer(x, h, route, layer, w_gate, w_up, w_down):
    n = x.shape[0]
    ranks, counts = rank_slots(route)
    counts = counts[0, :N_EXPERTS].astype(jnp.int32)
    padded = (counts + MOE_TILE - 1) // MOE_TILE * MOE_TILE
    pad_end = jnp.cumsum(padded)
    pad_start = pad_end - padded
    n_blocks = -(-(n * TOP_K + N_EXPERTS * (MOE_TILE - 1)) // MOE_TILE)
    eids = route[:, ROUTE_E1:ROUTE_E2 + 1].astype(jnp.int32)
    slot0 = jnp.sum(jnp.where(eids[..., None] == jnp.arange(N_EXPERTS), pad_start, 0), axis=-1)
    dest = ((slot0 + ranks[:, :TOP_K].astype(jnp.int32)) * TOKEN_ROWS).reshape(-1)
    block_start = jnp.arange(n_blocks, dtype=jnp.int32) * MOE_TILE
    block_expert = jnp.minimum(
        jnp.sum(block_start[:, None] >= pad_end[None, :], axis=1), N_EXPERTS - 1).astype(jnp.int32)
    n_used = (pad_end[-1:] // MOE_TILE).astype(jnp.int32)
    tail = n_used[0] + jnp.arange(N_EXPERTS + 1, dtype=jnp.int32)
    cand = jnp.concatenate([(pad_end // MOE_TILE - 1).astype(jnp.int32), tail])
    keep = jnp.concatenate([padded > 0, tail < n_blocks])
    zero_blocks = jnp.where(keep, cand, 0)[jnp.argsort(~keep, stable=True)].astype(jnp.int32)
    n_zero = jnp.sum(keep).astype(jnp.int32).reshape(1)
    xs = dispatch_rows(h, dest, zero_blocks, n_zero, n_blocks * MOE_TILE)
    y = expert_ffn(xs, block_expert, n_used, layer, w_gate, w_up, w_down)
    return combine_rows(x, route, y, dest)


def router_tables(w_group, b_group, w_erouter, b_erouter):
    d = w_group.shape[0]
    w_exp = jnp.transpose(w_erouter, (1, 0, 2)).reshape(d, N_EXPERTS)
    w = jnp.zeros((d, LANES), F32)
    w = w.at[:, :N_GROUPS].set(w_group.astype(F32))
    w = w.at[:, EXPERT_LANE0:EXPERT_LANE0 + N_EXPERTS].set(w_exp.astype(F32))
    bias = jnp.zeros((1, LANES), F32)
    bias = bias.at[0, :N_GROUPS].set(b_group.astype(F32))
    bias = bias.at[0, EXPERT_LANE0:EXPERT_LANE0 + N_EXPERTS].set(b_erouter.reshape(-1).astype(F32))
    hi = w.astype(BF16)
    lo = (w - hi.astype(F32)).astype(BF16)
    return hi, lo, bias


def _odd_in_kernel(x_ref, g_ref, w_ref, wab_ref, a_ref, dt_ref, tri_ref, o_ref, gb_ref, *, col_chunk):
    h = _rms_rows(x_ref[...], g_ref[...]).astype(BF16)
    for c in range(0, o_ref.shape[1], col_chunk):
        o_ref[:, c:c + col_chunk] = _dot(h, w_ref[:, c:c + col_chunk]).astype(o_ref.dtype)
    x = _dot(h, wab_ref[...])
    lane = lax.broadcasted_iota(jnp.int32, x.shape, 1)
    z = x + dt_ref[...]
    softplus = jnp.maximum(z, 0.0) + jnp.log(1.0 + jnp.exp(-jnp.abs(z)))
    g = -jnp.exp(a_ref[...]) * softplus
    gcum = _split3_dot(tri_ref[...], g)
    gb_ref[...] = jnp.where(lane < C_HEADS, gcum, _sigmoid(x))


def odd_in_proj(x, gain, w_main, w_ab, a_log, dt_bias):
    n, d = x.shape
    m = w_main.shape[1]
    pad = lambda v: jnp.zeros((1, LANES), F32).at[0, :C_HEADS].set(v.astype(F32))
    r = jnp.arange(ROW_TILE)
    tri = ((r[:, None] >= r[None, :]) & (r[:, None] // CHUNK == r[None, :] // CHUNK)).astype(BF16)
    const = lambda i: (0, 0)
    return pl.pallas_call(
        functools.partial(_odd_in_kernel, col_chunk=C_WIDTH),
        grid=(n // ROW_TILE,),
        in_specs=[pl.BlockSpec((ROW_TILE, d), lambda i: (i, 0)),
                  pl.BlockSpec((1, d), const),
                  pl.BlockSpec((d, m), const),
                  pl.BlockSpec((d, LANES), const),
                  pl.BlockSpec((1, LANES), const),
                  pl.BlockSpec((1, LANES), const),
                  pl.BlockSpec((ROW_TILE, ROW_TILE), const)],
        out_specs=[pl.BlockSpec((ROW_TILE, m), lambda i: (i, 0)),
                   pl.BlockSpec((ROW_TILE, LANES), lambda i: (i, 0))],
        out_shape=[jax.ShapeDtypeStruct((n, m), BF16),
                   jax.ShapeDtypeStruct((n, LANES), F32)],
        compiler_params=_params("parallel"),
        name="odd_in_proj",
    )(x, gain.reshape(1, d), w_main, w_ab, pad(a_log), pad(dt_bias), tri)


CONV_COLS = 256
CONV_ROWS = 512
CONV_HALO = 8


def _conv_kernel(x_ref, w_ref, o_ref, pad_scr, *, normalize, scale, scale_blocks):
    seq = x_ref.shape[0]
    out_scale = jnp.where(pl.program_id(1) < scale_blocks, scale, 1.0)
    pad_scr[0:CONV_HALO, :] = jnp.zeros((CONV_HALO, CONV_COLS), F32)
    pad_scr[CONV_HALO:, :] = x_ref[...].astype(F32)
    w = w_ref[...]
    for r0 in range(0, seq, CONV_ROWS):
        xe = pad_scr[r0:r0 + CONV_ROWS + CONV_HALO, :]
        acc = xe * w[0:1, :]
        for j in range(1, CONV_W):
            acc = pltpu.roll(acc, 1, axis=0) + xe * w[j:j + 1, :]
        acc = acc[CONV_HALO:]
        y = acc * _sigmoid(acc)
        if normalize:
            for s in range(CONV_COLS // LANES):
                ys = y[:, s * LANES:(s + 1) * LANES]
                ssq = jnp.sum(ys * ys, axis=1, keepdims=True)
                o_ref[r0:r0 + CONV_ROWS, s * LANES:(s + 1) * LANES] = (
                    ys * lax.rsqrt(ssq + EPS) * out_scale).astype(o_ref.dtype)
        else:
            o_ref[r0:r0 + CONV_ROWS, :] = y.astype(o_ref.dtype)


def conv_silu(proj3, conv_w, col0, width, normalize, scale=1.0, scale_width=0):
    batch, seq, _ = proj3.shape
    cb0 = col0 // CONV_COLS
    return pl.pallas_call(
        functools.partial(_conv_kernel, normalize=normalize, scale=scale,
                          scale_blocks=scale_width // CONV_COLS),
        grid=(batch, width // CONV_COLS),
        in_specs=[pl.BlockSpec((None, seq, CONV_COLS), lambda b, c: (b, 0, cb0 + c)),
                  pl.BlockSpec((CONV_W, CONV_COLS), lambda b, c: (0, cb0 + c))],
        out_specs=pl.BlockSpec((None, seq, CONV_COLS), lambda b, c: (b, 0, c)),
        out_shape=jax.ShapeDtypeStruct((batch, seq, width), BF16),
        scratch_shapes=[pltpu.VMEM((seq + CONV_HALO, CONV_COLS), F32)],
        compiler_params=_params("parallel", "parallel"),
        name="conv_silu",
    )(proj3, conv_w)


def _head_column(gb, lane_index):
    lane = lax.broadcasted_iota(jnp.int32, gb.shape, 1)
    return jnp.sum(jnp.where(lane == lane_index, gb, 0.0), axis=1, keepdims=True)


def _chunk_decay(gcum_col):
    r = lax.broadcasted_iota(jnp.int32, (CHUNK, CHUNK), 0)
    c = lax.broadcasted_iota(jnp.int32, (CHUNK, CHUNK), 1)
    gcum = jnp.broadcast_to(gcum_col, (CHUNK, LANES))
    gcum_rows = jnp.transpose(gcum)[:CHUNK, :]
    gdiff = jnp.where(c <= r, gcum[:, :CHUNK] - gcum_rows, 0.0)
    return gcum, gdiff, r, c


def _split3_dot(a_bf16, b_f32):
    b0 = b_f32.astype(BF16)
    r1 = b_f32 - b0.astype(F32)
    b1 = r1.astype(BF16)
    b2 = (r1 - b1.astype(F32)).astype(BF16)
    return _dot(a_bf16, b0) + _dot(a_bf16, b1) + _dot(a_bf16, b2)


def _lmat_kernel(k_ref, gb_ref, l_ref):
    hg = pl.program_id(1)
    gb = gb_ref[...]
    tiles = [(ci, hh) for ci in range(GDN_ROWS // CHUNK) for hh in range(DELTA_HEADS)]
    rows = lambda ci: slice(ci * CHUNK, (ci + 1) * CHUNK)
    kk = {}
    for ci, hh in tiles:
        k = k_ref[rows(ci), hh * LANES:(hh + 1) * LANES]
        kk[ci, hh] = _dot_nt(k, k)
    for ci, hh in tiles:
        head = hg * DELTA_HEADS + hh
        gb_c = gb[rows(ci)]
        _, gdiff, r, c = _chunk_decay(_head_column(gb_c, head))
        beta = _head_column(gb_c, C_HEADS + head)
        l_ref[hh, rows(ci), :] = jnp.where(r > c, beta * kk[ci, hh] * jnp.exp(gdiff), 0.0)


def gdn_lmat(k3, gb3):
    batch, seq, _ = k3.shape
    width = DELTA_HEADS * LANES
    groups = C_HEADS // DELTA_HEADS
    return pl.pallas_call(
        _lmat_kernel,
        grid=(batch, groups, seq // GDN_ROWS),
        in_specs=[pl.BlockSpec((None, GDN_ROWS, width), lambda b, h, i: (b, i, groups + h)),
                  pl.BlockSpec((None, GDN_ROWS, LANES), lambda b, h, i: (b, i, 0))],
        out_specs=pl.BlockSpec((None, DELTA_HEADS, GDN_ROWS, CHUNK), lambda b, h, i: (b, h, i, 0)),
        out_shape=jax.ShapeDtypeStruct((batch, C_HEADS, seq, CHUNK), F32),
        compiler_params=_params("parallel", "parallel", "parallel"),
        name="gdn_lmat",
    )(k3, gb3)


def _solve_kernel(l_ref, t_ref):
    sub = 8
    col = lax.broadcasted_iota(jnp.int32, (sub, SOLVE_LANES), 0)
    for i in range(CHUNK):
        acc = [jnp.where(col + g * sub == i, 1.0, 0.0) for g in range(CHUNK // sub)]
        for j in range(i):
            lij = l_ref[i, j:j + 1, :]
            for g in range(j // sub + 1):
                acc[g] = acc[g] - lij * t_ref[j, g * sub:(g + 1) * sub, :]
        t_ref[i] = jnp.concatenate(acc, axis=0)


def gdn_solve(lmat_t):
    g = lmat_t.shape[2]
    return pl.pallas_call(
        _solve_kernel,
        grid=(g // SOLVE_LANES,),
        in_specs=[pl.BlockSpec((CHUNK, CHUNK, SOLVE_LANES), lambda i: (0, 0, i))],
        out_specs=pl.BlockSpec((CHUNK, CHUNK, SOLVE_LANES), lambda i: (0, 0, i)),
        out_shape=jax.ShapeDtypeStruct(lmat_t.shape, F32),
        compiler_params=_params("parallel"),
        name="gdn_solve",
    )(lmat_t)


DELTA_HEADS = 4
DELTA_ROWS = 512


def _delta_kernel(q_ref, k_ref, v_ref, z_ref, gb_ref, t_ref, gain_ref, o_ref, *state_scrs):
    hg = pl.program_id(1)

    @pl.when(pl.program_id(2) == 0)
    def _():
        for scr in state_scrs:
            scr[...] = jnp.zeros(scr.shape, F32)

    states = [scr[...] for scr in state_scrs]
    gb = gb_ref[...]
    gain = gain_ref[...]
    n_chunks = DELTA_ROWS // CHUNK
    tiles = [(ci, hh) for ci in range(n_chunks) for hh in range(DELTA_HEADS)]
    rows = lambda ci: slice(ci * CHUNK, (ci + 1) * CHUNK)
    cols = lambda hh: slice(hh * LANES, (hh + 1) * LANES)

    pre = {}
    for ci, hh in tiles:
        head = hg * DELTA_HEADS + hh
        gb_c = gb[rows(ci)]
        gcum, gdiff, r, c = _chunk_decay(_head_column(gb_c, head))
        beta = _head_column(gb_c, C_HEADS + head)
        q = q_ref[rows(ci), cols(hh)]
        k = k_ref[rows(ci), cols(hh)]
        kf = k.astype(F32)
        eg = jnp.exp(gcum)
        g_last = gcum[CHUNK - 1:CHUNK, :]
        pre[ci, hh] = dict(
            q=q, k=k, eg=eg, e_last=jnp.exp(g_last),
            decay=jnp.where(c <= r, jnp.exp(gdiff), 0.0),
            k_dec=(kf * jnp.exp(g_last - gcum)).astype(BF16),
            rhs=jnp.concatenate([(v_ref[rows(ci), cols(hh)].astype(F32) * beta).astype(BF16),
                                 (kf * beta * eg).astype(BF16)], axis=1))
    uw = {t: _dot(t_ref[t[1], rows(t[0]), :].astype(BF16), pre[t]["rhs"]).astype(BF16)
          for t in tiles}
    a_intra = {t: (_dot_nt(pre[t]["q"], pre[t]["k"]) * pre[t]["decay"]).astype(BF16) for t in tiles}
    a_uw = {t: _dot(a_intra[t], uw[t]) for t in tiles}
    kd_uw = {t: _dot_tn(pre[t]["k_dec"], uw[t]) for t in tiles}
    lhs = {t: jnp.concatenate(
        [(pre[t]["q"].astype(F32) * pre[t]["eg"] - a_uw[t][:, LANES:]).astype(BF16),
         kd_uw[t][:, LANES:].astype(BF16)], axis=0) for t in tiles}

    for ci in range(n_chunks):
        for hh in range(DELTA_HEADS):
            t = (ci, hh)
            state = states[hh]
            prod = _dot(lhs[t], state.astype(BF16))
            o = prod[:CHUNK] + a_uw[t][:, :LANES]
            states[hh] = state * pre[t]["e_last"] - prod[CHUNK:] + kd_uw[t][:, :LANES]
            z = z_ref[rows(ci), cols(hh)].astype(F32)
            o_ref[rows(ci), cols(hh)] = (_rms_rows(o, gain) * (z * _sigmoid(z))).astype(o_ref.dtype)
    for scr, state in zip(state_scrs, states):
        scr[...] = state


def gdn_delta(qk3, v3, proj3, gb3, tinv, out_gain):
    batch, seq, _ = v3.shape
    width = DELTA_HEADS * LANES
    groups = C_HEADS // DELTA_HEADS
    zcol = 3 * C_WIDTH // width
    blk = (None, DELTA_ROWS, width)
    return pl.pallas_call(
        _delta_kernel,
        grid=(batch, groups, seq // DELTA_ROWS),
        in_specs=[pl.BlockSpec(blk, lambda b, h, i: (b, i, h)),
                  pl.BlockSpec(blk, lambda b, h, i: (b, i, groups + h)),
                  pl.BlockSpec(blk, lambda b, h, i: (b, i, h)),
                  pl.BlockSpec(blk, lambda b, h, i: (b, i, zcol + h)),
                  pl.BlockSpec((None, DELTA_ROWS, LANES), lambda b, h, i: (b, i, 0)),
                  pl.BlockSpec((None, DELTA_HEADS, DELTA_ROWS, CHUNK), lambda b, h, i: (b, h, i, 0)),
                  pl.BlockSpec((1, LANES), lambda b, h, i: (0, 0))],
        out_specs=pl.BlockSpec(blk, lambda b, h, i: (b, i, h)),
        out_shape=jax.ShapeDtypeStruct((batch, seq, C_WIDTH), BF16),
        scratch_shapes=[pltpu.VMEM((C_DK, LANES), F32) for _ in range(DELTA_HEADS)],
        compiler_params=_params("parallel", "parallel", "arbitrary"),
        name="gdn_delta",
    )(qk3, qk3, v3, proj3, gb3, tinv, out_gain.reshape(1, LANES))


def rope_tables(seq):
    half = HEAD_DIM // 2
    inv_freq = 1.0 / (ROPE_THETA ** (jnp.arange(0, HEAD_DIM, 2, dtype=F32) / HEAD_DIM))
    ang = jnp.arange(seq, dtype=F32)[:, None] * inv_freq[None, :]
    cos, sin = jnp.cos(ang), jnp.sin(ang)
    cos_t = jnp.tile(cos, (1, LANES // half))
    sin_t = jnp.tile(jnp.concatenate([-sin, sin], axis=1), (1, LANES // HEAD_DIM))
    return cos_t, sin_t


def even_layer(x, batch, seq, layer, attn_norm, w_in, a_q_gain, a_k_gain, b_q_gain, b_k_gain,
               lq1, lk1, lq2, lk2, b_out_gain, w_out, ffn_gain, router):
    lam_init = 0.8 - 0.6 * math.exp(-0.3 * layer)
    scale = HEAD_DIM ** -0.5
    tile = lambda g, s: jnp.tile(g.astype(F32) * s, A_WIDTH // HEAD_DIM)
    gains = jnp.stack([tile(a_q_gain, scale), tile(a_k_gain, 1.0),
                       tile(b_q_gain, scale), tile(b_k_gain, 1.0)])
    cos_t, sin_t = rope_tables(seq)
    qk, vv = even_in_proj(x, attn_norm, w_in.astype(BF16), gains, cos_t, sin_t, seq)
    oa = dilated_attention(qk, vv, batch, seq)
    ob = diff_attention(qk, vv, lq1.astype(F32), lk1.astype(F32), lq2.astype(F32), lk2.astype(F32),
                        b_out_gain, lam_init, batch, seq)
    return outproj_router(x, oa, 0, ob, 0, w_out.astype(BF16), ffn_gain, *router)


def odd_layer(x, batch, seq, attn_norm, w_in, conv_w, a_log, dt_bias, out_gain, w_out, ffn_gain, router):
    n = batch * seq
    w_main = w_in[:, :4 * C_WIDTH].astype(BF16)
    w_ab = jnp.pad(w_in[:, 4 * C_WIDTH:], ((0, 0), (0, LANES - 2 * C_HEADS))).astype(BF16)
    proj, gb = odd_in_proj(x, attn_norm, w_main, w_ab, a_log, dt_bias)
    proj3 = proj.reshape(batch, seq, proj.shape[1])
    conv_w = conv_w.astype(F32)
    qk3 = conv_silu(proj3, conv_w, 0, 2 * C_WIDTH, True, C_DK ** -0.5, C_WIDTH)
    v3 = conv_silu(proj3, conv_w, 2 * C_WIDTH, C_WIDTH, False)
    gb3 = gb.reshape(batch, seq, LANES)
    lmat = gdn_lmat(qk3, gb3)
    g = batch * C_HEADS * (seq // CHUNK)
    lmat_t = jnp.transpose(lmat.reshape(g, CHUNK, CHUNK), (1, 2, 0))
    tinv = jnp.transpose(gdn_solve(lmat_t), (2, 0, 1)).reshape(batch, C_HEADS, seq, CHUNK)
    mix = gdn_delta(qk3, v3, proj3, gb3, tinv, out_gain).reshape(n, C_WIDTH)
    return outproj_router(x, mix, 0, mix, 1, w_out.astype(BF16), ffn_gain, *router)


def kernel(x, even_attn_norm, even_w_in, a_q_norm, a_k_norm, b_q_norm, b_k_norm, b_lambda_q1, b_lambda_k1, b_lambda_q2, b_lambda_k2, b_out_norm, even_w_out, odd_attn_norm, odd_w_in, odd_conv_w, odd_a_log, odd_dt_bias, odd_out_norm, odd_w_out, ffn_norm, router_group_w, router_group_b, router_expert_w, router_expert_b, expert_w_gate, expert_w_up, expert_w_down):
    batch, seq, d = x.shape
    depth = ffn_norm.shape[0]
    xf = x.reshape(batch * seq, d).astype(F32)
    for layer in range(depth):
        i = layer // 2
        router = router_tables(router_group_w[layer], router_group_b[layer],
                               router_expert_w[layer], router_expert_b[layer])
        if layer % 2 == 0:
            xf, h, route = even_layer(
                xf, batch, seq, layer, even_attn_norm[i], even_w_in[i], a_q_norm[i], a_k_norm[i],
                b_q_norm[i], b_k_norm[i], b_lambda_q1[i], b_lambda_k1[i], b_lambda_q2[i],
                b_lambda_k2[i], b_out_norm[i], even_w_out[i], ffn_norm[layer], router)
        else:
            xf, h, route = odd_layer(
                xf, batch, seq, odd_attn_norm[i], odd_w_in[i], odd_conv_w[i], odd_a_log[i],
                odd_dt_bias[i], odd_out_norm[i], odd_w_out[i], ffn_norm[layer], router)
        xf = moe_layer(xf, h, route, layer, expert_w_gate, expert_w_up, expert_w_down)
    return xf.reshape(batch, seq, d).astype(x.dtype)
```
